```python
import jax, jax.numpy as jnp
from jax import lax
import numpy as np

D_MODEL = 4096
BATCH = 1
SEQ = 8192
DEPTH = 1

HEAD_DIM = 128
MIX_W = D_MODEL
N_MIX_HEADS = MIX_W // HEAD_DIM
DIL_CONFIGS = ((128, 1), (512, 4), (2048, 16))
DIL_HEADS_PER_CFG = 4
DIL_HEADS = DIL_HEADS_PER_CFG * len(DIL_CONFIGS)
NSA_HEADS = N_MIX_HEADS - DIL_HEADS
NSA_KV_GROUPS = 4
NSA_REP = NSA_HEADS // NSA_KV_GROUPS
CMP_LEN = 32
CMP_STRIDE = 16
CMP_HIDDEN = 256
SEL_BLOCK = 64
SEL_TOPK = 16
WIN = 512
MEM_LEN = 256
CROSS_HEADS = 4
CROSS_W = CROSS_HEADS * HEAD_DIM
N_GROUPS = 8
EXPERTS_PER_GROUP = 8
N_EXPERTS = N_GROUPS * EXPERTS_PER_GROUP
EXPERT_TOPK = 2
D_EXPERT = 1024
QBLOCK = 128
MOE_ROWS = 128
EPS = 1e-6
NEG = -1e30
FORCE = 1e4
SCALE = HEAD_DIM ** -0.5

NSA_Q_W = NSA_HEADS * HEAD_DIM
NSA_KV_W = NSA_KV_GROUPS * HEAD_DIM
NSA_GATE_W = NSA_HEADS * 3
DIL_W = DIL_HEADS * HEAD_DIM
IN_SIZES = (NSA_Q_W,) + (NSA_KV_W,) * 6 + (NSA_GATE_W,) + (DIL_W,) * 3
IN_COLS = sum(IN_SIZES)

kernel_name = 'hybrid_nsa_dilated_hmoe_block'


def rmsnorm(x, g):
    xf = x.astype(jnp.float32)
    y = xf * lax.rsqrt(jnp.mean(xf * xf, axis=-1, keepdims=True) + EPS)
    return (y * g.astype(jnp.float32)).astype(x.dtype)


def alibi_slopes(n):
    return jnp.exp2(-8.0 * jnp.arange(1, n + 1, dtype=jnp.float32) / n)


def _pad_seq(a, front, back):
    cfg = [(0, 0)] * (a.ndim - 2) + [(front, back), (0, 0)]
    return jnp.pad(a, cfg)


def banded_attention(q, k, v, slopes, max_back, dist_scale):
    L, dk = q.shape[-2], q.shape[-1]
    nb = -(-L // QBLOCK)
    pad_end = nb * QBLOCK - L
    nprev = -(-max_back // QBLOCK)
    qb = _pad_seq(q, 0, pad_end).reshape(*q.shape[:-2], nb, QBLOCK, dk)
    kp = _pad_seq(k, nprev * QBLOCK, pad_end).reshape(*k.shape[:-2], nb + nprev, QBLOCK, dk)
    vp = _pad_seq(v, nprev * QBLOCK, pad_end).reshape(*v.shape[:-2], nb + nprev, QBLOCK, dk)
    kb = jnp.concatenate([kp[..., i:i + nb, :, :] for i in range(nprev + 1)], axis=-2)
    vb = jnp.concatenate([vp[..., i:i + nb, :, :] for i in range(nprev + 1)], axis=-2)
    bk = (nprev + 1) * QBLOCK
    i = jnp.arange(QBLOCK)[:, None]
    j = jnp.arange(bk)[None, :]
    delta = nprev * QBLOCK + i - j
    kpos = (jnp.arange(nb)[:, None, None] - nprev) * QBLOCK + j[None]
    mask = (delta >= 0) & (delta <= max_back) & (kpos >= 0)
    bias = slopes[..., None, None, None] * (delta * dist_scale).astype(jnp.float32)
    s = jnp.einsum('...rnqd,...nkd->...rnqk', qb, kb).astype(jnp.float32) * SCALE
    s = jnp.where(mask, s - bias, NEG)
    m = jnp.max(s, axis=-1, keepdims=True)
    p = jnp.exp(s - m)
    l = jnp.sum(p, axis=-1, keepdims=True)
    o = jnp.einsum('...rnqk,...nkd->...rnqd', p, vb.astype(jnp.float32)) / l
    lse = (m + jnp.log(l))[..., 0]
    o = o.reshape(*o.shape[:-3], nb * QBLOCK, dk)[..., :L, :]
    lse = lse.reshape(*lse.shape[:-2], nb * QBLOCK)[..., :L]
    return o, lse


def nsa_compressed(q, k_raw, v_raw, pe_k, w1_k, w2_k, pe_v, w1_v, w2_v, slopes):
    T = q.shape[-2]
    n_cmp = (T - CMP_LEN) // CMP_STRIDE + 1
    idx = CMP_STRIDE * jnp.arange(n_cmp)[:, None] + jnp.arange(CMP_LEN)[None, :]

    def compress(a, pe, w1, w2):
        blk = a[:, :, idx, :] + pe
        blk = blk.reshape(*blk.shape[:3], CMP_LEN * HEAD_DIM)
        return jax.nn.gelu(blk @ w1) @ w2

    kc = compress(k_raw, pe_k, w1_k, w2_k)
    vc = compress(v_raw, pe_v, w1_v, w2_v)
    t = jnp.arange(T)[:, None]
    blk_end = (CMP_STRIDE * jnp.arange(n_cmp) + CMP_LEN - 1)[None, :]
    dist = (t - blk_end).astype(jnp.float32)
    valid = dist >= 0
    s = jnp.einsum('bgrtd,bgnd->bgrtn', q, kc).astype(jnp.float32) * SCALE
    s = jnp.where(valid, s - slopes[:, :, None, None] * dist, NEG)
    m = jnp.max(s, axis=-1, keepdims=True)
    p = jnp.exp(s - m) * valid
    l = jnp.sum(p, axis=-1, keepdims=True)
    p = p / jnp.where(l > 0, l, 1.0)
    o = jnp.einsum('bgrtn,bgnd->bgrtd', p, vc.astype(jnp.float32))
    return o, p


def nsa_selected(q, k_sel, v_sel, p_cmp, slopes):
    B, G, R, T, dk = q.shape
    n_blk = T // SEL_BLOCK
    n_sel = min(SEL_TOPK, n_blk)
    n_cmp = p_cmp.shape[-1]
    ci = CMP_STRIDE * jnp.arange(n_cmp)[:, None]
    sj = SEL_BLOCK * jnp.arange(n_blk)[None, :]
    overlap = ((ci < sj + SEL_BLOCK) & (ci + CMP_LEN > sj)).astype(jnp.float32)
    imp = jnp.einsum('bgrtn,nj->bgtj', p_cmp, overlap)
    t = jnp.arange(T)[:, None]
    blk = jnp.arange(n_blk)[None, :]
    cur = t // SEL_BLOCK
    causal = blk * SEL_BLOCK <= t
    forced = (blk == 0) | (blk == cur) | (blk == cur - 1)
    imp = jnp.where(causal, jnp.where(forced, FORCE, imp), NEG)
    vals, sel_idx = lax.top_k(imp, n_sel)
    sel_ok = vals > NEG / 2
    kb = k_sel.reshape(B, G, n_blk, SEL_BLOCK, dk)
    vb = v_sel.reshape(B, G, n_blk, SEL_BLOCK, dk)
    nc = T // QBLOCK
    bi = jnp.arange(B)[:, None, None]
    gi = jnp.arange(G)[None, :, None]

    def chunk(args):
        qc, ic, okc, c = args
        tq = c * QBLOCK + jnp.arange(QBLOCK)
        flat = ic.reshape(B, G, QBLOCK * n_sel)
        kg = kb[bi, gi, flat].reshape(B, G, QBLOCK, n_sel * SEL_BLOCK, dk)
        vg = vb[bi, gi, flat].reshape(B, G, QBLOCK, n_sel * SEL_BLOCK, dk)
        kpos = (ic[..., None] * SEL_BLOCK + jnp.arange(SEL_BLOCK)).reshape(B, G, QBLOCK, n_sel * SEL_BLOCK)
        dist = tq[:, None] - kpos
        ok = (dist >= 0) & jnp.repeat(okc, SEL_BLOCK, axis=-1)
        s = jnp.einsum('bgrqd,bgqkd->bgrqk', qc, kg).astype(jnp.float32) * SCALE
        s = s - slopes[None, :, :, None, None] * dist[:, :, None].astype(jnp.float32)
        s = jnp.where(ok[:, :, None], s, NEG)
        p = jax.nn.softmax(s, axis=-1)
        return jnp.einsum('bgrqk,bgqkd->bgrqd', p, vg.astype(jnp.float32))

    xs = (jnp.moveaxis(q.reshape(B, G, R, nc, QBLOCK, dk), 3, 0),
          jnp.moveaxis(sel_idx.reshape(B, G, nc, QBLOCK, n_sel), 2, 0),
          jnp.moveaxis(sel_ok.reshape(B, G, nc, QBLOCK, n_sel), 2, 0),
          jnp.arange(nc))
    o = lax.map(chunk, xs)
    return jnp.moveaxis(o, 0, 3).reshape(B, G, R, T, dk)


def dilated_attention(qd, kd, vd, slopes):
    B, T, _ = qd.shape
    q = qd.reshape(B, T, len(DIL_CONFIGS), DIL_HEADS_PER_CFG, HEAD_DIM)
    k = kd.reshape(B, T, len(DIL_CONFIGS), DIL_HEADS_PER_CFG, HEAD_DIM)
    v = vd.reshape(B, T, len(DIL_CONFIGS), DIL_HEADS_PER_CFG, HEAD_DIM)
    outs, lses = [], []
    for g, (window, dil) in enumerate(DIL_CONFIGS):
        L = T // dil

        def to_res(a):
            return a[:, :, g].reshape(B, L, dil, DIL_HEADS_PER_CFG, HEAD_DIM).transpose(0, 3, 2, 1, 4)

        o, lse = banded_attention(to_res(q)[:, :, :, None], to_res(k), to_res(v),
                                  slopes[g][:, None, None], window // dil, dil)
        outs.append(o[:, :, :, 0].transpose(0, 3, 2, 1, 4).reshape(B, T, DIL_HEADS_PER_CFG, HEAD_DIM))
        lses.append(lse[:, :, :, 0].transpose(0, 3, 2, 1).reshape(B, T, DIL_HEADS_PER_CFG))
    alpha = jax.nn.softmax(jnp.stack(lses, axis=0), axis=0)
    o = jnp.stack(outs, axis=0) * alpha[..., None]
    return o.transpose(1, 2, 0, 3, 4).reshape(B, T, DIL_W)


def hybrid_mixer(h, w_in, w_out, pe_k, w1_k, w2_k, pe_v, w1_v, w2_v):
    B, T, _ = h.shape
    G, R, dk = NSA_KV_GROUPS, NSA_REP, HEAD_DIM
    proj = h @ w_in
    offs = np.cumsum(IN_SIZES)[:-1].tolist()
    q_n, kc, vc, ks, vs, kw, vw, gates, qd, kd, vd = jnp.split(proj, offs, axis=-1)
    q_n = q_n.reshape(B, T, G, R, dk).transpose(0, 2, 3, 1, 4)

    def kv_heads(a):
        return a.reshape(B, T, G, dk).transpose(0, 2, 1, 3)

    slopes = alibi_slopes(N_MIX_HEADS)
    s_nsa = slopes[:NSA_HEADS].reshape(G, R)
    s_dil = slopes[NSA_HEADS:].reshape(len(DIL_CONFIGS), DIL_HEADS_PER_CFG)
    o_cmp, p_cmp = nsa_compressed(q_n, kv_heads(kc), kv_heads(vc), pe_k, w1_k, w2_k, pe_v, w1_v, w2_v, s_nsa)
    o_sel = nsa_selected(q_n, kv_heads(ks), kv_heads(vs), p_cmp, s_nsa)
    o_win, _ = banded_attention(q_n, kv_heads(kw), kv_heads(vw), s_nsa, WIN - 1, 1)
    g = jax.nn.sigmoid(gates.astype(jnp.float32)).reshape(B, T, G, R, 3).transpose(0, 2, 3, 1, 4)
    o_nsa = g[..., 0:1] * o_cmp + g[..., 1:2] * o_sel + g[..., 2:3] * o_win
    o_nsa = o_nsa.transpose(0, 3, 1, 2, 4).reshape(B, T, NSA_Q_W)
    o_dil = dilated_attention(qd, kd, vd, s_dil)
    o = jnp.concatenate([o_nsa, o_dil], axis=-1).astype(h.dtype)
    return o @ w_out


def memory_cross_attention(h, mem, norm_mem, wq, wkv, wo):
    B, T, _ = h.shape
    S = mem.shape[1]
    m = rmsnorm(mem, norm_mem)
    q = (h @ wq).reshape(B, T, CROSS_HEADS, HEAD_DIM)
    kv = (m @ wkv).reshape(B, S, 2, CROSS_HEADS, HEAD_DIM)
    s = jnp.einsum('bthd,bshd->bhts', q, kv[:, :, 0]).astype(jnp.float32) * SCALE
    p = jax.nn.softmax(s, axis=-1)
    o = jnp.einsum('bhts,bshd->bthd', p, kv[:, :, 1].astype(jnp.float32))
    return o.reshape(B, T, CROSS_W).astype(h.dtype) @ wo


def hier_moe(h, w_rg, b_rg, w_re, b_re, w_gate, w_up, w_down):
    B, T, D = h.shape
    n_tok = B * T
    hf = h.reshape(n_tok, D)
    lg = (hf @ w_rg).astype(jnp.float32) + b_rg
    grp = jnp.argmax(lg, axis=-1)
    p_grp = jnp.take_along_axis(jax.nn.softmax(lg, axis=-1), grp[:, None], axis=-1)
    le = ((hf @ w_re).astype(jnp.float32) + b_re).reshape(n_tok, N_GROUPS, EXPERTS_PER_GROUP)
    le = jnp.take_along_axis(le, grp[:, None, None], axis=1)[:, 0]
    top_v, top_j = lax.top_k(le, EXPERT_TOPK)
    gate = p_grp * jax.nn.softmax(top_v, axis=-1)
    expert = grp[:, None] * EXPERTS_PER_GROUP + top_j
    n_assign = n_tok * EXPERT_TOPK
    e_flat = expert.reshape(n_assign)
    tok_flat = jnp.repeat(jnp.arange(n_tok, dtype=jnp.int32), EXPERT_TOPK)
    w_flat = gate.reshape(n_assign)
    order = jnp.argsort(e_flat)
    e_s, tok_s, w_s = e_flat[order], tok_flat[order], w_flat[order]
    counts = jnp.zeros((N_EXPERTS,), jnp.int32).at[e_flat].add(1)
    starts = jnp.cumsum(counts) - counts
    pcounts = (counts + MOE_ROWS - 1) // MOE_ROWS * MOE_ROWS
    pends = jnp.cumsum(pcounts)
    pstarts = pends - pcounts
    row = pstarts[e_s] + (jnp.arange(n_assign) - starts[e_s])
    n_blk = n_assign // MOE_ROWS + N_EXPERTS
    n_rows = n_blk * MOE_ROWS
    row_tok = jnp.zeros((n_rows,), jnp.int32).at[row].set(tok_s)
    row_w = jnp.zeros((n_rows,), jnp.float32).at[row].set(w_s)
    blk_start = jnp.arange(n_blk) * MOE_ROWS
    blk_e = jnp.minimum(jnp.sum(pends[None, :] <= blk_start[:, None], axis=1), N_EXPERTS - 1)

    def expert_block(args):
        e_b, toks, ws = args
        xb = hf[toks]
        a = jax.nn.silu(xb @ w_gate[e_b]) * (xb @ w_up[e_b])
        return (a @ w_down[e_b]).astype(jnp.float32) * ws[:, None]

    ys = lax.map(expert_block, (blk_e, row_tok.reshape(n_blk, MOE_ROWS), row_w.reshape(n_blk, MOE_ROWS)))
    out = jnp.zeros((n_tok, D), jnp.float32).at[row_tok].add(ys.reshape(n_rows, D))
    return out.reshape(B, T, D).astype(h.dtype)


def setup_inputs(seed: int = 0) -> dict:
    key = jax.random.key(seed)
    ks = jax.random.split(key, 25)
    f32 = jnp.float32

    def nrm(k, shape, scale):
        return jax.random.normal(k, shape, f32) * scale

    def gain(k, shape):
        return 1.0 + 0.02 * jax.random.normal(k, shape, f32)

    L = DEPTH
    return {
        'x': nrm(ks[0], (BATCH, SEQ, D_MODEL), 1.0),
        'mem': nrm(ks[1], (BATCH, MEM_LEN, D_MODEL), 1.0),
        'norm_mix': gain(ks[2], (L, D_MODEL)),
        'w_in': nrm(ks[3], (L, D_MODEL, IN_COLS), D_MODEL ** -0.5),
        'w_out': nrm(ks[4], (L, MIX_W, D_MODEL), MIX_W ** -0.5),
        'cmp_pe_k': nrm(ks[5], (L, CMP_LEN, HEAD_DIM), 0.02),
        'cmp_w1_k': nrm(ks[6], (L, CMP_LEN * HEAD_DIM, CMP_HIDDEN), (CMP_LEN * HEAD_DIM) ** -0.5),
        'cmp_w2_k': nrm(ks[7], (L, CMP_HIDDEN, HEAD_DIM), CMP_HIDDEN ** -0.5),
        'cmp_pe_v': nrm(ks[8], (L, CMP_LEN, HEAD_DIM), 0.02),
        'cmp_w1_v': nrm(ks[9], (L, CMP_LEN * HEAD_DIM, CMP_HIDDEN), (CMP_LEN * HEAD_DIM) ** -0.5),
        'cmp_w2_v': nrm(ks[10], (L, CMP_HIDDEN, HEAD_DIM), CMP_HIDDEN ** -0.5),
        'norm_cross': gain(ks[11], (L, D_MODEL)),
        'norm_mem': gain(ks[12], (L, D_MODEL)),
        'w_q_cross': nrm(ks[13], (L, D_MODEL, CROSS_W), D_MODEL ** -0.5),
        'w_kv_cross': nrm(ks[14], (L, D_MODEL, 2 * CROSS_W), D_MODEL ** -0.5),
        'w_o_cross': nrm(ks[15], (L, CROSS_W, D_MODEL), CROSS_W ** -0.5),
        'norm_ffn': gain(ks[16], (L, D_MODEL)),
        'w_router_group': nrm(ks[17], (L, D_MODEL, N_GROUPS), D_MODEL ** -0.5),
        'b_router_group': nrm(ks[18], (L, N_GROUPS), 0.01),
        'w_router_expert': nrm(ks[19], (L, D_MODEL, N_EXPERTS), D_MODEL ** -0.5),
        'b_router_expert': nrm(ks[20], (L, N_EXPERTS), 0.01),
        'w_gate': nrm(ks[21], (L, N_EXPERTS, D_MODEL, D_EXPERT), D_MODEL ** -0.5),
        'w_up': nrm(ks[22], (L, N_EXPERTS, D_MODEL, D_EXPERT), D_MODEL ** -0.5),
        'w_down': nrm(ks[23], (L, N_EXPERTS, D_EXPERT, D_MODEL), D_EXPERT ** -0.5),
        'norm_final': gain(ks[24], (D_MODEL,)),
    }


def reference(x, mem, norm_mix, w_in, w_out, cmp_pe_k, cmp_w1_k, cmp_w2_k, cmp_pe_v, cmp_w1_v, cmp_w2_v,
              norm_cross, norm_mem, w_q_cross, w_kv_cross, w_o_cross, norm_ffn, w_router_group,
              b_router_group, w_router_expert, b_router_expert, w_gate, w_up, w_down, norm_final):
    for l in range(DEPTH):
        h = rmsnorm(x, norm_mix[l])
        x = x + hybrid_mixer(h, w_in[l], w_out[l], cmp_pe_k[l], cmp_w1_k[l], cmp_w2_k[l],
                             cmp_pe_v[l], cmp_w1_v[l], cmp_w2_v[l]).astype(x.dtype)
        h = rmsnorm(x, norm_cross[l])
        x = x + memory_cross_attention(h, mem, norm_mem[l], w_q_cross[l], w_kv_cross[l],
                                       w_o_cross[l]).astype(x.dtype)
        h = rmsnorm(x, norm_ffn[l])
        x = x + hier_moe(h, w_router_group[l], b_router_group[l], w_router_expert[l], b_router_expert[l],
                         w_gate[l], w_up[l], w_down[l]).astype(x.dtype)
    return rmsnorm(x, norm_final)
```

```python
import functools

import numpy as np
import jax
import jax.numpy as jnp
from jax import lax
from jax.experimental import pallas as pl
from jax.experimental.pallas import tpu as pltpu

F32 = jnp.float32
BF16 = jnp.bfloat16

HEAD_DIM = 128
DIL_CONFIGS = ((128, 1), (512, 4), (2048, 16))
DIL_HEADS_PER_CFG = 4
DIL_HEADS = DIL_HEADS_PER_CFG * len(DIL_CONFIGS)
N_MIX_HEADS = 32
NSA_HEADS = N_MIX_HEADS - DIL_HEADS
NSA_KV_GROUPS = 4
NSA_REP = NSA_HEADS // NSA_KV_GROUPS
CMP_LEN = 32
CMP_STRIDE = 16
SEL_BLOCK = 64
SEL_TOPK = 16
WIN = 512
CROSS_HEADS = 4
N_GROUPS = 8
EXPERTS_PER_GROUP = 8
N_EXPERTS = N_GROUPS * EXPERTS_PER_GROUP
EXPERT_TOPK = 2
MOE_ROWS = 128
EPS = 1e-6
NEG = -1e30
FORCE = 1e4
SCALE = HEAD_DIM ** -0.5

V7X_LANES = 128
V7X_VMEM_BYTES = 64 * 1024 * 1024
VMEM_BIG = 52 * 1024 * 1024
VMEM_MID = 40 * 1024 * 1024

QN_BLK = 0
KC_BLK = NSA_HEADS
VC_BLK = KC_BLK + NSA_KV_GROUPS
KS_BLK = VC_BLK + NSA_KV_GROUPS
VS_BLK = KS_BLK + NSA_KV_GROUPS
KW_BLK = VS_BLK + NSA_KV_GROUPS
VW_BLK = KW_BLK + NSA_KV_GROUPS
QD_BLK = VW_BLK + NSA_KV_GROUPS
KD_BLK = QD_BLK + DIL_HEADS
VD_BLK = KD_BLK + DIL_HEADS
PROJ_BLKS = VD_BLK + DIL_HEADS
NSA_Q_W = NSA_HEADS * HEAD_DIM
NSA_KV_W = NSA_KV_GROUPS * HEAD_DIM
NSA_GATE_W = NSA_HEADS * 3
DIL_W = DIL_HEADS * HEAD_DIM


def _cparams(sem, vmem=None):
    return pltpu.CompilerParams(dimension_semantics=sem, vmem_limit_bytes=vmem)


def _dot(a, b):
    return jnp.dot(a, b, preferred_element_type=F32)


def _dot_nt(a, b):
    return lax.dot_general(a, b, (((1,), (1,)), ((), ())), preferred_element_type=F32)


def _split3(a):
    hi = a.astype(BF16)
    r1 = a - hi.astype(F32)
    mid = r1.astype(BF16)
    lo = (r1 - mid.astype(F32)).astype(BF16)
    return hi, mid, lo


def _rms_kernel(x_ref, g_ref, o_ref):
    x = x_ref[...]
    ms = jnp.mean(x * x, axis=-1, keepdims=True)
    o_ref[...] = (x * lax.rsqrt(ms + EPS) * g_ref[...]).astype(o_ref.dtype)


def _rmsnorm(x, g, out_dtype=BF16):
    n, d = x.shape
    tr = min(256, n)
    return pl.pallas_call(
        _rms_kernel,
        grid=(n // tr,),
        in_specs=[pl.BlockSpec((tr, d), lambda i: (i, 0)), pl.BlockSpec((1, d), lambda i: (0, 0))],
        out_specs=pl.BlockSpec((tr, d), lambda i: (i, 0)),
        out_shape=jax.ShapeDtypeStruct((n, d), out_dtype),
        compiler_params=_cparams(("parallel",)),
    )(x, g.reshape(1, d))


def _mm_kernel(x_ref, w_ref, o_ref):
    o_ref[...] = _dot(x_ref[...], w_ref[...]).astype(o_ref.dtype)


def _mm_res_kernel(x_ref, w_ref, r_ref, o_ref):
    o_ref[...] = (r_ref[...] + _dot(x_ref[...], w_ref[...])).astype(o_ref.dtype)


def _matmul(x, w, out_dtype, residual=None):
    m, k = x.shape
    n = w.shape[1]
    tm = min(1024, m)
    tn = min(512, n)
    in_specs = [pl.BlockSpec((tm, k), lambda i, j: (i, 0)), pl.BlockSpec((k, tn), lambda i, j: (0, j))]
    args = [x, w]
    kern = _mm_kernel
    if residual is not None:
        in_specs.append(pl.BlockSpec((tm, tn), lambda i, j: (i, j)))
        args.append(residual)
        kern = _mm_res_kernel
    return pl.pallas_call(
        kern,
        grid=(m // tm, n // tn),
        in_specs=in_specs,
        out_specs=pl.BlockSpec((tm, tn), lambda i, j: (i, j)),
        out_shape=jax.ShapeDtypeStruct((m, n), out_dtype),
        compiler_params=_cparams(("parallel", "parallel"), VMEM_BIG),
    )(*args)


def _gelu_tanh(x):
    return 0.5 * x * (1.0 + jnp.tanh(np.sqrt(2.0 / np.pi) * (x + 0.044715 * (x * x * x))))


def _cmp_kernel(a_ref, w1_ref, pe_ref, w2_ref, o_ref):
    a = a_ref[0, 0]
    nc = a.shape[0]
    half = CMP_STRIDE * HEAD_DIM
    ha = _dot(a, w1_ref[0, :half, :])
    hb = _dot(a, w1_ref[0, half:, :])
    c = _dot(pe_ref[0], w1_ref[0])[0:1]
    pre = ha + pltpu.roll(hb, nc - 1, 0) + c
    hid = _gelu_tanh(pre)
    o_ref[0, 0] = _dot(hid.astype(BF16), w2_ref[0]).astype(o_ref.dtype)


def _compress(a2, w1, pe, w2):
    _, g, nc, ck = a2.shape
    hid = w1.shape[-1]
    return pl.pallas_call(
        _cmp_kernel,
        grid=(2, g),
        in_specs=[
            pl.BlockSpec((1, 1, nc, ck), lambda s, gg: (s, gg, 0, 0)),
            pl.BlockSpec((1, 2 * ck, hid), lambda s, gg: (s, 0, 0)),
            pl.BlockSpec((1, 8, 2 * ck), lambda s, gg: (s, 0, 0)),
            pl.BlockSpec((1, hid, HEAD_DIM), lambda s, gg: (s, 0, 0)),
        ],
        out_specs=pl.BlockSpec((1, 1, nc, HEAD_DIM), lambda s, gg: (s, gg, 0, 0)),
        out_shape=jax.ShapeDtypeStruct((2, g, nc, HEAD_DIM), BF16),
        compiler_params=_cparams(("parallel", "parallel"), VMEM_MID),
    )(a2, w1, pe, w2)


def _cmpattn_kernel(slopes_ref, q_ref, kc_ref, vc_ref, ov_ref, o_ref, sel_ref, *, n_sel):
    g = pl.program_id(0)
    i = pl.program_id(1)
    tq = q_ref.shape[0]
    nc = kc_ref.shape[2]
    t = i * tq + lax.broadcasted_iota(jnp.int32, (tq, 1), 0)
    n = lax.broadcasted_iota(jnp.int32, (1, nc), 1)
    dist = (t - (CMP_STRIDE * n + CMP_LEN - 1)).astype(F32)
    valid = dist >= 0
    kc = kc_ref[0, 0]
    vc = vc_ref[0, 0]
    psum = jnp.zeros((tq, nc), F32)
    for r in range(NSA_REP):
        q = q_ref[:, r * HEAD_DIM:(r + 1) * HEAD_DIM]
        slope = slopes_ref[g * NSA_REP + r]
        s = _dot_nt(q, kc) * SCALE
        s = jnp.where(valid, s - slope * dist, NEG)
        m = jnp.max(s, axis=-1, keepdims=True)
        p = jnp.where(valid, jnp.exp(s - m), 0.0)
        l = jnp.sum(p, axis=-1, keepdims=True)
        p = p / jnp.where(l > 0, l, 1.0)
        o_ref[:, r * HEAD_DIM:(r + 1) * HEAD_DIM] = _dot(p.astype(BF16), vc).astype(o_ref.dtype)
        psum = psum + p
    ov = ov_ref[...]
    hi, mid, lo = _split3(psum)
    imp = _dot(hi, ov) + _dot(mid, ov) + _dot(lo, ov)
    blk = lax.broadcasted_iota(jnp.int32, (1, V7X_LANES), 1)
    cur = t // SEL_BLOCK
    causal = blk * SEL_BLOCK <= t
    forced = (blk == 0) | (blk == cur) | (blk == cur - 1)
    work = jnp.where(causal, jnp.where(forced, FORCE, imp), NEG)
    sel = jnp.zeros((tq, V7X_LANES), F32)
    for _ in range(n_sel):
        mx = jnp.max(work, axis=-1, keepdims=True)
        idx = jnp.min(jnp.where(work == mx, blk, V7X_LANES), axis=-1, keepdims=True)
        pick = blk == idx
        sel = jnp.where(pick, 1.0, sel)
        work = jnp.where(pick, -jnp.inf, work)
    sel_ref[0] = jnp.where(causal, sel, 0.0)


def _cmp_attention(proj, kvc, slopes, overlap, n_sel):
    t = proj.shape[0]
    g = NSA_KV_GROUPS
    nc = kvc.shape[2]
    tq = min(256, t)
    qw = NSA_REP * HEAD_DIM
    return pl.pallas_call(
        functools.partial(_cmpattn_kernel, n_sel=n_sel),
        grid=(g, t // tq),
        in_specs=[
            pl.BlockSpec(memory_space=pltpu.SMEM),
            pl.BlockSpec((tq, qw), lambda gg, i: (i, gg)),
            pl.BlockSpec((1, 1, nc, HEAD_DIM), lambda gg, i: (0, gg, 0, 0)),
            pl.BlockSpec((1, 1, nc, HEAD_DIM), lambda gg, i: (1, gg, 0, 0)),
            pl.BlockSpec((nc, V7X_LANES), lambda gg, i: (0, 0)),
        ],
        out_specs=[
            pl.BlockSpec((tq, qw), lambda gg, i: (i, gg)),
            pl.BlockSpec((1, tq, V7X_LANES), lambda gg, i: (gg, i, 0)),
        ],
        out_shape=[
            jax.ShapeDtypeStruct((t, NSA_Q_W), BF16),
            jax.ShapeDtypeStruct((g, t, V7X_LANES), F32),
        ],
        compiler_params=_cparams(("parallel", "parallel"), VMEM_MID),
    )(slopes, proj, kvc, kvc, overlap)


def _flash_init(m_sc, l_sc, acc_sc):
    m_sc[...] = jnp.full(m_sc.shape, NEG, F32)
    l_sc[...] = jnp.zeros(l_sc.shape, F32)
    acc_sc[...] = jnp.zeros(acc_sc.shape, F32)


def _flash_update(r, s, v, m_sc, l_sc, acc_sc):
    m_prev = m_sc[r]
    m_new = jnp.maximum(m_prev, jnp.max(s, axis=-1, keepdims=True))
    alpha = jnp.exp(m_prev - m_new)
    p = jnp.exp(s - m_new)
    l_sc[r] = alpha * l_sc[r] + jnp.sum(p, axis=-1, keepdims=True)
    acc_sc[r] = alpha * acc_sc[r] + _dot(p.astype(BF16), v)
    m_sc[r] = m_new


def _selattn_kernel(slopes_ref, q_ref, k_ref, v_ref, sel_ref, e_ref, o_ref, m_sc, l_sc, acc_sc):
    g = pl.program_id(0)
    i = pl.program_id(1)
    j = pl.program_id(2)
    tq = q_ref.shape[0]
    tk = k_ref.shape[0]

    @pl.when(j == 0)
    def _():
        _flash_init(m_sc, l_sc, acc_sc)

    @pl.when(j <= i)
    def _():
        k = k_ref[...]
        v = v_ref[...]
        mexp = _dot(sel_ref[0].astype(BF16), e_ref[0])
        tpos = i * tq + lax.broadcasted_iota(jnp.int32, (tq, 1), 0)
        kpos = j * tk + lax.broadcasted_iota(jnp.int32, (1, tk), 1)
        dist = tpos - kpos
        ok = (mexp > 0.5) & (dist >= 0)
        distf = dist.astype(F32)
        for r in range(NSA_REP):
            slope = slopes_ref[g * NSA_REP + r]
            s = _dot_nt(q_ref[:, r * HEAD_DIM:(r + 1) * HEAD_DIM], k) * SCALE - slope * distf
            s = jnp.where(ok, s, NEG)
            _flash_update(r, s, v, m_sc, l_sc, acc_sc)

    @pl.when(j == i)
    def _():
        for r in range(NSA_REP):
            o_ref[:, r * HEAD_DIM:(r + 1) * HEAD_DIM] = (acc_sc[r] / l_sc[r]).astype(o_ref.dtype)


def _sel_attention(proj, sel, expand, slopes):
    t = proj.shape[0]
    g = NSA_KV_GROUPS
    tq = min(512, t)
    nq = t // tq
    qw = NSA_REP * HEAD_DIM
    return pl.pallas_call(
        _selattn_kernel,
        grid=(g, nq, nq),
        in_specs=[
            pl.BlockSpec(memory_space=pltpu.SMEM),
            pl.BlockSpec((tq, qw), lambda gg, i, j: (i, gg)),
            pl.BlockSpec((tq, HEAD_DIM), lambda gg, i, j: (jnp.minimum(j, i), KS_BLK + gg)),
            pl.BlockSpec((tq, HEAD_DIM), lambda gg, i, j: (jnp.minimum(j, i), VS_BLK + gg)),
            pl.BlockSpec((1, tq, V7X_LANES), lambda gg, i, j: (gg, i, 0)),
            pl.BlockSpec((1, V7X_LANES, tq), lambda gg, i, j: (jnp.minimum(j, i), 0, 0)),
        ],
        out_specs=pl.BlockSpec((tq, qw), lambda gg, i, j: (i, gg)),
        out_shape=jax.ShapeDtypeStruct((t, NSA_Q_W), BF16),
        scratch_shapes=[
            pltpu.VMEM((NSA_REP, tq, 1), F32),
            pltpu.VMEM((NSA_REP, tq, 1), F32),
            pltpu.VMEM((NSA_REP, tq, HEAD_DIM), F32),
        ],
        compiler_params=_cparams(("parallel", "parallel", "arbitrary"), VMEM_MID),
    )(slopes, proj, proj, proj, sel, expand)


def _band_kernel(slopes_ref, q_ref, k_ref, v_ref, *rest, n_rep, max_back, dist_scale, slope_base, want_lse):
    if want_lse:
        o_ref, lse_ref, m_sc, l_sc, acc_sc = rest
    else:
        o_ref, m_sc, l_sc, acc_sc = rest
    g = pl.program_id(0)
    i = pl.program_id(2)
    j = pl.program_id(3)
    tq = q_ref.shape[0]
    kt = i - 1 + j

    @pl.when(j == 0)
    def _():
        _flash_init(m_sc, l_sc, acc_sc)

    @pl.when(kt >= 0)
    def _():
        k = k_ref[...]
        v = v_ref[...]
        upos = i * tq + lax.broadcasted_iota(jnp.int32, (tq, 1), 0)
        kpos = kt * tq + lax.broadcasted_iota(jnp.int32, (1, tq), 1)
        delta = upos - kpos
        ok = (delta >= 0) & (delta <= max_back)
        bias = (delta * dist_scale).astype(F32)
        for r in range(n_rep):
            slope = slopes_ref[slope_base + g * n_rep + r]
            s = _dot_nt(q_ref[:, r * HEAD_DIM:(r + 1) * HEAD_DIM], k) * SCALE - slope * bias
            s = jnp.where(ok, s, NEG)
            _flash_update(r, s, v, m_sc, l_sc, acc_sc)

    @pl.when(j == 1)
    def _():
        for r in range(n_rep):
            l = l_sc[r]
            o_ref[:, r * HEAD_DIM:(r + 1) * HEAD_DIM] = (acc_sc[r] / l).astype(o_ref.dtype)
            if want_lse:
                lse_ref[:, r * HEAD_DIM:(r + 1) * HEAD_DIM] = jnp.broadcast_to(m_sc[r] + jnp.log(l), (tq, HEAD_DIM))


def _band_attention(qkv, slopes, *, seq, n_res, n_groups, n_rep, tq, max_back, dist_scale, slope_base,
                    q_col, k_col, v_col, o_col, o_cols, want_lse):
    assert max_back <= tq and seq % tq == 0
    qw = n_rep * HEAD_DIM
    kern = functools.partial(_band_kernel, n_rep=n_rep, max_back=max_back, dist_scale=dist_scale,
                             slope_base=slope_base, want_lse=want_lse)
    o_spec = pl.BlockSpec((tq, qw), lambda g, r, i, j: (i, o_col(g, r)))
    out_specs = [o_spec]
    out_shape = [jax.ShapeDtypeStruct((seq, o_cols), BF16)]
    if want_lse:
        out_specs.append(o_spec)
        out_shape.append(jax.ShapeDtypeStruct((seq, o_cols), F32))
    return pl.pallas_call(
        kern,
        grid=(n_groups, n_res, seq // tq, 2),
        in_specs=[
            pl.BlockSpec(memory_space=pltpu.SMEM),
            pl.BlockSpec((tq, qw), lambda g, r, i, j: (i, q_col(g, r))),
            pl.BlockSpec((tq, HEAD_DIM), lambda g, r, i, j: (jnp.maximum(i - 1 + j, 0), k_col(g, r))),
            pl.BlockSpec((tq, HEAD_DIM), lambda g, r, i, j: (jnp.maximum(i - 1 + j, 0), v_col(g, r))),
        ],
        out_specs=out_specs,
        out_shape=out_shape,
        scratch_shapes=[
            pltpu.VMEM((n_rep, tq, 1), F32),
            pltpu.VMEM((n_rep, tq, 1), F32),
            pltpu.VMEM((n_rep, tq, HEAD_DIM), F32),
        ],
        compiler_params=_cparams(("parallel", "parallel", "parallel", "arbitrary"), VMEM_MID),
    )(slopes, qkv, qkv, qkv)


def _combine_kernel(gates_ref, ocmp_ref, osel_ref, owin_ref, odil_ref, lse_ref, o_ref):
    sg = 1.0 / (1.0 + jnp.exp(-gates_ref[...]))
    for h in range(NSA_HEADS):
        cs = slice(h * HEAD_DIM, (h + 1) * HEAD_DIM)
        o = (sg[:, 3 * h:3 * h + 1] * ocmp_ref[:, cs].astype(F32)
             + sg[:, 3 * h + 1:3 * h + 2] * osel_ref[:, cs].astype(F32)
             + sg[:, 3 * h + 2:3 * h + 3] * owin_ref[:, cs].astype(F32))
        o_ref[:, cs] = o.astype(o_ref.dtype)
    n_cfg = len(DIL_CONFIGS)
    for hh in range(DIL_HEADS_PER_CFG):
        cols = [slice((c * DIL_HEADS_PER_CFG + hh) * HEAD_DIM, (c * DIL_HEADS_PER_CFG + hh + 1) * HEAD_DIM)
                for c in range(n_cfg)]
        ls = [lse_ref[:, cs] for cs in cols]
        mx = functools.reduce(jnp.maximum, ls)
        es = [jnp.exp(l - mx) for l in ls]
        den = functools.reduce(lambda a, b: a + b, es)
        for c in range(n_cfg):
            oc = odil_ref[:, cols[c]].astype(F32) * (es[c] / den)
            o_ref[:, NSA_Q_W + cols[c].start:NSA_Q_W + cols[c].stop] = oc.astype(o_ref.dtype)


def _combine(gates, o_cmp, o_sel, o_win, o_dil, lse_dil):
    t = gates.shape[0]
    tr = min(256, t)
    row = lambda w: pl.BlockSpec((tr, w), lambda i: (i, 0))
    return pl.pallas_call(
        _combine_kernel,
        grid=(t // tr,),
        in_specs=[row(V7X_LANES), row(NSA_Q_W), row(NSA_Q_W), row(NSA_Q_W), row(DIL_W), row(DIL_W)],
        out_specs=row(NSA_Q_W + DIL_W),
        out_shape=jax.ShapeDtypeStruct((t, NSA_Q_W + DIL_W), BF16),
        compiler_params=_cparams(("parallel",), VMEM_MID),
    )(gates, o_cmp, o_sel, o_win, o_dil, lse_dil)


def _cross_kernel(q_ref, kv_ref, wo_ref, x_ref, o_ref):
    cw = CROSS_HEADS * HEAD_DIM
    outs = []
    for h in range(CROSS_HEADS):
        q = q_ref[:, h * HEAD_DIM:(h + 1) * HEAD_DIM]
        k = kv_ref[:, h * HEAD_DIM:(h + 1) * HEAD_DIM]
        v = kv_ref[:, cw + h * HEAD_DIM:cw + (h + 1) * HEAD_DIM]
        s = _dot_nt(q, k) * SCALE
        m = jnp.max(s, axis=-1, keepdims=True)
        p = jnp.exp(s - m)
        l = jnp.sum(p, axis=-1, keepdims=True)
        outs.append((_dot(p.astype(BF16), v) / l).astype(BF16))
    o = jnp.concatenate(outs, axis=-1)
    o_ref[...] = x_ref[...] + _dot(o, wo_ref[...])


def _cross_attention(q, kv, wo, x):
    t, d = x.shape
    s_len = kv.shape[0]
    cw = CROSS_HEADS * HEAD_DIM
    tq = min(256, t)
    return pl.pallas_call(
        _cross_kernel,
        grid=(t // tq,),
        in_specs=[
            pl.BlockSpec((tq, cw), lambda i: (i, 0)),
            pl.BlockSpec((s_len, 2 * cw), lambda i: (0, 0)),
            pl.BlockSpec((cw, d), lambda i: (0, 0)),
            pl.BlockSpec((tq, d), lambda i: (i, 0)),
        ],
        out_specs=pl.BlockSpec((tq, d), lambda i: (i, 0)),
        out_shape=jax.ShapeDtypeStruct((t, d), F32),
        compiler_params=_cparams(("parallel",), VMEM_MID),
    )(q, kv, wo, x)


def _router_kernel(x_ref, g_ref, whi_ref, wmid_ref, wlo_ref, b_ref, h_ref, eid_ref, gate_ref):
    x = x_ref[...]
    ms = jnp.mean(x * x, axis=-1, keepdims=True)
    h = x * lax.rsqrt(ms + EPS) * g_ref[...]
    h_ref[...] = h.astype(h_ref.dtype)
    hh, hm, hl = _split3(h)
    whi, wmid, wlo = whi_ref[...], wmid_ref[...], wlo_ref[...]
    logits = (_dot(hh, whi) + (_dot(hh, wmid) + _dot(hm, whi))
              + (_dot(hh, wlo) + _dot(hm, wmid) + _dot(hl, whi))) + b_ref[...]
    lane = lax.broadcasted_iota(jnp.int32, (1, V7X_LANES), 1)
    ninf = -jnp.inf
    is_g = lane < N_GROUPS
    lg = jnp.where(is_g, logits, ninf)
    mg = jnp.max(lg, axis=-1, keepdims=True)
    grp = jnp.min(jnp.where(lg == mg, lane, V7X_LANES), axis=-1, keepdims=True)
    p_grp = 1.0 / jnp.sum(jnp.where(is_g, jnp.exp(lg - mg), 0.0), axis=-1, keepdims=True)
    lo_lane = N_GROUPS + grp * EXPERTS_PER_GROUP
    in_grp = (lane >= lo_lane) & (lane < lo_lane + EXPERTS_PER_GROUP)
    le = jnp.where(in_grp, logits, ninf)
    v1 = jnp.max(le, axis=-1, keepdims=True)
    i1 = jnp.min(jnp.where(le == v1, lane, V7X_LANES), axis=-1, keepdims=True)
    le2 = jnp.where(lane == i1, ninf, le)
    v2 = jnp.max(le2, axis=-1, keepdims=True)
    i2 = jnp.min(jnp.where(le2 == v2, lane, V7X_LANES), axis=-1, keepdims=True)
    e2 = jnp.exp(v2 - v1)
    den = 1.0 + e2
    eid_ref[...] = jnp.where(lane == 0, i1 - N_GROUPS, jnp.where(lane == 1, i2 - N_GROUPS, 0))
    gate_ref[...] = jnp.where(lane == 0, p_grp / den, jnp.where(lane == 1, p_grp * e2 / den, 0.0))


def _router(x, g, w_hi, w_mid, w_lo, b):
    t, d = x.shape
    tr = min(256, t)
    full = lambda shp: pl.BlockSpec(shp, lambda i: (0, 0))
    return pl.pallas_call(
        _router_kernel,
        grid=(t // tr,),
        in_specs=[pl.BlockSpec((tr, d), lambda i: (i, 0)), full((1, d)),
                  full((d, V7X_LANES)), full((d, V7X_LANES)), full((d, V7X_LANES)), full((1, V7X_LANES))],
        out_specs=[pl.BlockSpec((tr, d), lambda i: (i, 0)),
                   pl.BlockSpec((tr, V7X_LANES), lambda i: (i, 0)),
                   pl.BlockSpec((tr, V7X_LANES), lambda i: (i, 0))],
        out_shape=[jax.ShapeDtypeStruct((t, d), BF16),
                   jax.ShapeDtypeStruct((t, V7X_LANES), jnp.int32),
                   jax.ShapeDtypeStruct((t, V7X_LANES), F32)],
        compiler_params=_cparams(("parallel",), VMEM_MID),
    )(x, g.reshape(1, d), w_hi, w_mid, w_lo, b)


GATHER_ROWS = 256


def _row_copy(src_ref, dst_ref, sem, src_row, dst_row):
    return pltpu.make_async_copy(src_ref.at[src_row], dst_ref.at[dst_row], sem)


def _gather_kernel(tok_ref, h_ref, o_ref, sem, *, rows):
    base = pl.program_id(0) * rows

    def issue(k, c):
        _row_copy(h_ref, o_ref, sem, tok_ref[base + k], base + k).start()
        return c

    lax.fori_loop(0, rows, issue, 0)

    def drain(k, c):
        _row_copy(h_ref, o_ref, sem, 0, base + k).wait()
        return c

    lax.fori_loop(0, rows, drain, 0)


def _gather_rows(h3, row_tok):
    t, d = h3.shape
    n_rows = row_tok.shape[0]
    rows = GATHER_ROWS
    sub = d // V7X_LANES
    out = pl.pallas_call(
        functools.partial(_gather_kernel, rows=rows),
        grid_spec=pltpu.PrefetchScalarGridSpec(
            num_scalar_prefetch=1,
            grid=(n_rows // rows,),
            in_specs=[pl.BlockSpec(memory_space=pl.ANY)],
            out_specs=pl.BlockSpec(memory_space=pl.ANY),
            scratch_shapes=[pltpu.SemaphoreType.DMA(())],
        ),
        out_shape=jax.ShapeDtypeStruct((n_rows, sub, V7X_LANES), h3.dtype),
        compiler_params=_cparams(("arbitrary",)),
    )(row_tok, h3.reshape(t, sub, V7X_LANES))
    return out.reshape(n_rows, d)


def _new_chunk(s, a_ref, b_ref):
    prev = jnp.maximum(s - 1, 0)
    return (s == 0) | (a_ref[s] != a_ref[prev]) | (b_ref[s] != b_ref[prev])


def _moe_up_kernel(se_ref, sf_ref, si_ref, so_ref, oc_ref, on_ref, x_ref, wg_ref, wu_ref, a_ref, wg_sc, wu_sc):
    s = pl.program_id(0)

    @pl.when(_new_chunk(s, se_ref, sf_ref))
    def _():
        wg_sc[...] = wg_ref[0].astype(BF16)
        wu_sc[...] = wu_ref[0].astype(BF16)

    @pl.when(on_ref[s] == 1)
    def _():
        x = x_ref[...]
        gt = _dot(x, wg_sc[...])
        up = _dot(x, wu_sc[...])
        a_ref[...] = (gt * (1.0 / (1.0 + jnp.exp(-gt))) * up).astype(a_ref.dtype)

    @pl.when(on_ref[s] == 0)
    def _():
        a_ref[...] = jnp.zeros(a_ref.shape, a_ref.dtype)


def _moe_down_kernel(se_ref, sn_ref, si_ref, so_ref, oc_ref, on_ref, a_ref, wd_ref, rw_ref, y_ref, wd_sc):
    s = pl.program_id(0)

    @pl.when(_new_chunk(s, se_ref, sn_ref))
    def _():
        wd_sc[...] = wd_ref[0].astype(BF16)

    @pl.when(on_ref[s] == 1)
    def _():
        y_ref[...] = (_dot(a_ref[...], wd_sc[...]) * rw_ref[...]).astype(y_ref.dtype)

    @pl.when(on_ref[s] == 0)
    def _():
        y_ref[...] = jnp.zeros(y_ref.shape, y_ref.dtype)


def _moe_schedule(blk_e, n_active, n_chunks):
    n_blk = blk_e.shape[0]
    pair = jnp.arange(n_blk * n_chunks, dtype=jnp.int32)
    b = pair // n_chunks
    c = pair % n_chunks
    active = b < n_active
    key = jnp.where(active, (blk_e[b] * n_chunks + c) * n_blk + b, jnp.iinfo(jnp.int32).max)
    order = jnp.argsort(key)
    so, oc, on = b[order], c[order], active[order]
    last = n_active * n_chunks - 1
    si = jnp.where(on, so, so[last])
    sc = jnp.where(on, oc, oc[last])
    return blk_e[si], sc, si, so, oc, on.astype(jnp.int32)


def _moe_experts(xs, row_w, blk_e, n_active, w_gate, w_up, w_down):
    n_rows, d = xs.shape
    n_blk = n_rows // MOE_ROWS
    d_exp = w_gate.shape[-1]
    tf = min(256, d_exp)
    nf = d_exp // tf
    sched = _moe_schedule(blk_e, n_active, nf)
    act = pl.pallas_call(
        _moe_up_kernel,
        grid_spec=pltpu.PrefetchScalarGridSpec(
            num_scalar_prefetch=6,
            grid=(n_blk * nf,),
            in_specs=[
                pl.BlockSpec((MOE_ROWS, d), lambda s, se, sc, si, so, oc, on: (si[s], 0)),
                pl.BlockSpec((1, d, tf), lambda s, se, sc, si, so, oc, on: (se[s], 0, sc[s])),
                pl.BlockSpec((1, d, tf), lambda s, se, sc, si, so, oc, on: (se[s], 0, sc[s])),
            ],
            out_specs=pl.BlockSpec((MOE_ROWS, tf), lambda s, se, sc, si, so, oc, on: (so[s], oc[s])),
            scratch_shapes=[pltpu.VMEM((d, tf), BF16), pltpu.VMEM((d, tf), BF16)],
        ),
        out_shape=jax.ShapeDtypeStruct((n_rows, d_exp), BF16),
        compiler_params=_cparams(("arbitrary",), VMEM_BIG),
    )(*sched, xs, w_gate, w_up)

    tn = min(1024, d)
    nn = d // tn
    sched = _moe_schedule(blk_e, n_active, nn)
    return pl.pallas_call(
        _moe_down_kernel,
        grid_spec=pltpu.PrefetchScalarGridSpec(
            num_scalar_prefetch=6,
            grid=(n_blk * nn,),
            in_specs=[
                pl.BlockSpec((MOE_ROWS, d_exp), lambda s, se, sc, si, so, oc, on: (si[s], 0)),
                pl.BlockSpec((1, d_exp, tn), lambda s, se, sc, si, so, oc, on: (se[s], 0, sc[s])),
                pl.BlockSpec((MOE_ROWS, 1), lambda s, se, sc, si, so, oc, on: (si[s], 0)),
            ],
            out_specs=pl.BlockSpec((MOE_ROWS, tn), lambda s, se, sc, si, so, oc, on: (so[s], oc[s])),
            scratch_shapes=[pltpu.VMEM((d_exp, tn), BF16)],
        ),
        out_shape=jax.ShapeDtypeStruct((n_rows, d), F32),
        compiler_params=_cparams(("arbitrary",), VMEM_BIG),
    )(*sched, act, w_down, row_w.reshape(n_rows, 1))


FINAL_TOKENS = 128


def _final_kernel(pos_ref, x_ref, y_ref, g_ref, o_ref, ybuf, sem, *, tq):
    base = pl.program_id(0) * tq

    def issue(k, c):
        for kk in range(EXPERT_TOPK):
            _row_copy(y_ref, ybuf, sem, pos_ref[(base + k) * EXPERT_TOPK + kk], kk * tq + k).start()
        return c

    lax.fori_loop(0, tq, issue, 0)

    def drain(k, c):
        for kk in range(EXPERT_TOPK):
            _row_copy(y_ref, ybuf, sem, 0, kk * tq + k).wait()
        return c

    lax.fori_loop(0, tq, drain, 0)
    x = x_ref[...]
    for kk in range(EXPERT_TOPK):
        x = x + ybuf[kk * tq:(kk + 1) * tq]
    sq = jnp.sum(jnp.sum(x * x, axis=2, keepdims=True), axis=1, keepdims=True)
    inv = lax.rsqrt(sq / (x.shape[1] * x.shape[2]) + EPS)
    o_ref[...] = x * inv * g_ref[...]


def _final(x2, y, pos, g_final):
    t, d = x2.shape
    n_rows = y.shape[0]
    sub = d // V7X_LANES
    tq = min(FINAL_TOKENS, t)
    out = pl.pallas_call(
        functools.partial(_final_kernel, tq=tq),
        grid_spec=pltpu.PrefetchScalarGridSpec(
            num_scalar_prefetch=1,
            grid=(t // tq,),
            in_specs=[
                pl.BlockSpec((tq, sub, V7X_LANES), lambda i, pos: (i, 0, 0)),
                pl.BlockSpec(memory_space=pl.ANY),
                pl.BlockSpec((1, sub, V7X_LANES), lambda i, pos: (0, 0, 0)),
            ],
            out_specs=pl.BlockSpec((tq, sub, V7X_LANES), lambda i, pos: (i, 0, 0)),
            scratch_shapes=[pltpu.VMEM((EXPERT_TOPK * tq, sub, V7X_LANES), F32), pltpu.SemaphoreType.DMA(())],
        ),
        out_shape=jax.ShapeDtypeStruct((t, sub, V7X_LANES), F32),
        compiler_params=_cparams(("arbitrary",), VMEM_MID),
    )(pos, x2.reshape(t, sub, V7X_LANES), y.reshape(n_rows, sub, V7X_LANES), g_final.reshape(1, sub, V7X_LANES))
    return out.reshape(t, d)


def _alibi_slopes(n):
    return jnp.exp2(-8.0 * jnp.arange(1, n + 1, dtype=F32) / n)


def _hybrid_mixer(x, norm_g, w_in, w_out, pe_k, w1_k, w2_k, pe_v, w1_v, w2_v):
    t, d = x.shape
    assert t % (CMP_STRIDE * 8) == 0 and t // SEL_BLOCK <= V7X_LANES
    h = _rmsnorm(x, norm_g)

    g0 = NSA_Q_W + 6 * NSA_KV_W
    w_main = jnp.concatenate([w_in[:, :g0], w_in[:, g0 + NSA_GATE_W:]], axis=1).astype(BF16)
    w_gates = jnp.pad(w_in[:, g0:g0 + NSA_GATE_W], ((0, 0), (0, V7X_LANES - NSA_GATE_W))).astype(BF16)
    proj = _matmul(h, w_main, BF16)
    gates = _matmul(h, w_gates, F32)
    slopes = _alibi_slopes(N_MIX_HEADS)

    nc = t // CMP_STRIDE
    kv_raw = proj[:, KC_BLK * HEAD_DIM:KS_BLK * HEAD_DIM]
    a2 = kv_raw.reshape(nc, CMP_STRIDE, 2, NSA_KV_GROUPS, HEAD_DIM).transpose(2, 3, 0, 1, 4)
    a2 = a2.reshape(2, NSA_KV_GROUPS, nc, CMP_STRIDE * HEAD_DIM)
    w1 = jnp.stack([w1_k, w1_v]).astype(BF16)
    w2 = jnp.stack([w2_k, w2_v]).astype(BF16)
    pe = jnp.stack([pe_k, pe_v]).reshape(2, 1, CMP_LEN * HEAD_DIM)
    pe = jnp.broadcast_to(pe, (2, 8, CMP_LEN * HEAD_DIM)).astype(BF16)
    kvc = _compress(a2, w1, pe, w2)

    n_blk = t // SEL_BLOCK
    n_sel = min(SEL_TOPK, n_blk)
    ci = CMP_STRIDE * np.arange(nc)[:, None]
    sj = SEL_BLOCK * np.arange(V7X_LANES)[None, :]
    overlap = jnp.asarray(((ci < sj + SEL_BLOCK) & (ci + CMP_LEN > sj)).astype(np.float32), dtype=BF16)
    o_cmp, sel = _cmp_attention(proj, kvc, slopes, overlap, n_sel)

    tq = min(512, t)
    kblk = (np.arange(t) // SEL_BLOCK).reshape(t // tq, 1, tq)
    expand = jnp.asarray((kblk == np.arange(V7X_LANES)[None, :, None]).astype(np.float32), dtype=BF16)
    o_sel = _sel_attention(proj, sel, expand, slopes)

    (o_win,) = _band_attention(
        proj, slopes, seq=t, n_res=1, n_groups=NSA_KV_GROUPS, n_rep=NSA_REP, tq=min(512, t), max_back=WIN - 1,
        dist_scale=1, slope_base=0, q_col=lambda g, r: g, k_col=lambda g, r: KW_BLK + g,
        v_col=lambda g, r: VW_BLK + g, o_col=lambda g, r: g, o_cols=NSA_Q_W, want_lse=False)

    o_dil = []
    lse_dil = []
    for c, (window, dil) in enumerate(DIL_CONFIGS):
        seq = t // dil
        hb = c * DIL_HEADS_PER_CFG
        o_c, lse_c = _band_attention(
            proj.reshape(seq, dil * PROJ_BLKS * HEAD_DIM), slopes, seq=seq, n_res=dil,
            n_groups=DIL_HEADS_PER_CFG, n_rep=1, tq=min(256, seq), max_back=window // dil, dist_scale=dil,
            slope_base=NSA_HEADS + hb,
            q_col=lambda g, r, hb=hb: r * PROJ_BLKS + QD_BLK + hb + g,
            k_col=lambda g, r, hb=hb: r * PROJ_BLKS + KD_BLK + hb + g,
            v_col=lambda g, r, hb=hb: r * PROJ_BLKS + VD_BLK + hb + g,
            o_col=lambda g, r: r * DIL_HEADS_PER_CFG + g, o_cols=dil * DIL_HEADS_PER_CFG * HEAD_DIM,
            want_lse=True)
        o_dil.append(o_c.reshape(t, DIL_HEADS_PER_CFG * HEAD_DIM))
        lse_dil.append(lse_c.reshape(t, DIL_HEADS_PER_CFG * HEAD_DIM))
    o_dil = jnp.concatenate(o_dil, axis=1)
    lse_dil = jnp.concatenate(lse_dil, axis=1)

    o = _combine(gates, o_cmp, o_sel, o_win, o_dil, lse_dil)
    return _matmul(o, w_out.astype(BF16), F32, residual=x)


def _cross_block(x, mem, norm_cross, norm_mem, wq, wkv, wo):
    h = _rmsnorm(x, norm_cross)
    m = _rmsnorm(mem, norm_mem)
    q = _matmul(h, wq.astype(BF16), BF16)
    kv = _matmul(m, wkv.astype(BF16), BF16)
    return _cross_attention(q, kv, wo.astype(BF16), x)


def _moe_block(x, norm_ffn, w_rg, b_rg, w_re, b_re, w_gate, w_up, w_down, norm_final):
    t, d = x.shape
    pad = V7X_LANES - N_GROUPS - N_EXPERTS
    w_r = jnp.pad(jnp.concatenate([w_rg, w_re], axis=1), ((0, 0), (0, pad)))
    b_r = jnp.pad(jnp.concatenate([b_rg, b_re]), (0, pad)).reshape(1, V7X_LANES)
    w_hi, w_mid, w_lo = _split3(w_r)
    h3, eid, gate = _router(x, norm_ffn, w_hi, w_mid, w_lo, b_r)

    n_assign = t * EXPERT_TOPK
    e_flat = eid[:, :EXPERT_TOPK].reshape(n_assign)
    w_flat = gate[:, :EXPERT_TOPK].reshape(n_assign)
    tok_flat = jnp.repeat(jnp.arange(t, dtype=jnp.int32), EXPERT_TOPK)
    order = jnp.argsort(e_flat)
    e_s = e_flat[order]
    counts = jnp.zeros((N_EXPERTS,), jnp.int32).at[e_flat].add(1)
    starts = jnp.cumsum(counts) - counts
    pcounts = (counts + MOE_ROWS - 1) // MOE_ROWS * MOE_ROWS
    pends = jnp.cumsum(pcounts)
    pstarts = pends - pcounts
    row = (pstarts[e_s] + (jnp.arange(n_assign, dtype=jnp.int32) - starts[e_s])).astype(jnp.int32)
    n_blk = n_assign // MOE_ROWS + N_EXPERTS
    n_rows = n_blk * MOE_ROWS
    row_tok = jnp.zeros((n_rows,), jnp.int32).at[row].set(tok_flat[order])
    row_w = jnp.zeros((n_rows,), F32).at[row].set(w_flat[order])
    pos = jnp.zeros((n_assign,), jnp.int32).at[order].set(row)
    blk_start = jnp.arange(n_blk, dtype=jnp.int32) * MOE_ROWS
    blk_e = jnp.minimum(jnp.sum(pends[None, :] <= blk_start[:, None], axis=1), N_EXPERTS - 1).astype(jnp.int32)
    n_active = (pends[-1] // MOE_ROWS).astype(jnp.int32)

    xs = _gather_rows(h3, row_tok)
    y = _moe_experts(xs, row_w, blk_e, n_active, w_gate, w_up, w_down)
    return _final(x, y, pos, norm_final)


def kernel(x, mem, norm_mix, w_in, w_out, cmp_pe_k, cmp_w1_k, cmp_w2_k, cmp_pe_v, cmp_w1_v, cmp_w2_v, norm_cross, norm_mem, w_q_cross, w_kv_cross, w_o_cross, norm_ffn, w_router_group, b_router_group, w_router_expert, b_router_expert, w_gate, w_up, w_down, norm_final):
    b, t, d = x.shape
    depth = norm_mix.shape[0]
    assert b == 1 and depth == 1
    xs = x.reshape(t, d)
    l = 0
    xs = _hybrid_mixer(xs, norm_mix[l], w_in[l], w_out[l], cmp_pe_k[l], cmp_w1_k[l], cmp_w2_k[l],
                       cmp_pe_v[l], cmp_w1_v[l], cmp_w2_v[l])
    xs = _cross_block(xs, mem.reshape(mem.shape[1], d), norm_cross[l], norm_mem[l], w_q_cross[l],
                      w_kv_cross[l], w_o_cross[l])
    out = _moe_block(xs, norm_ffn[l], w_router_group[l], b_router_group[l], w_router_expert[l],
                     b_router_expert[l], w_gate[l], w_up[l], w_down[l], norm_final)
    return out.reshape(b, t, d)
```

```python
import functools

import numpy as np
import jax
import jax.numpy as jnp
from jax import lax
from jax.experimental import pallas as pl
from jax.experimental.pallas import tpu as pltpu

F32 = jnp.float32
BF16 = jnp.bfloat16
U32 = jnp.uint32

HEAD_DIM = 128
DIL_CONFIGS = ((128, 1), (512, 4), (2048, 16))
DIL_HEADS_PER_CFG = 4
DIL_HEADS = DIL_HEADS_PER_CFG * len(DIL_CONFIGS)
N_MIX_HEADS = 32
NSA_HEADS = N_MIX_HEADS - DIL_HEADS
NSA_KV_GROUPS = 4
NSA_REP = NSA_HEADS // NSA_KV_GROUPS
CMP_LEN = 32
CMP_STRIDE = 16
SEL_BLOCK = 64
SEL_TOPK = 16
WIN = 512
CROSS_HEADS = 4
N_GROUPS = 8
EXPERTS_PER_GROUP = 8
N_EXPERTS = N_GROUPS * EXPERTS_PER_GROUP
EXPERT_TOPK = 2
EPS = 1e-6
NEG = -1e30
FORCE = 1e4
SCALE = HEAD_DIM ** -0.5

V7X_LANES = 128
V7X_SUBLANES = 8
V7X_VMEM_BYTES = 64 * 1024 * 1024
VMEM_BIG = 56 * 1024 * 1024
VMEM_MID = 40 * 1024 * 1024

QN_BLK = 0
KC_BLK = NSA_HEADS
VC_BLK = KC_BLK + NSA_KV_GROUPS
KS_BLK = VC_BLK + NSA_KV_GROUPS
VS_BLK = KS_BLK + NSA_KV_GROUPS
KW_BLK = VS_BLK + NSA_KV_GROUPS
VW_BLK = KW_BLK + NSA_KV_GROUPS
QD_BLK = VW_BLK + NSA_KV_GROUPS
KD_BLK = QD_BLK + DIL_HEADS
VD_BLK = KD_BLK + DIL_HEADS
PROJ_BLKS = VD_BLK + DIL_HEADS
NSA_Q_W = NSA_HEADS * HEAD_DIM
NSA_KV_W = NSA_KV_GROUPS * HEAD_DIM
NSA_GATE_W = NSA_HEADS * 3
DIL_CFG_W = DIL_HEADS_PER_CFG * HEAD_DIM

X_SLAB_ROWS = 16
Y_SLAB_ROWS = 32
Y_SLAB_PITCH = 40
MOE_UNIT_ROWS = 320
MOE_SUB_ROWS = 256


def _cparams(sem, vmem=None):
    return pltpu.CompilerParams(dimension_semantics=sem, vmem_limit_bytes=vmem)


def _dot(a, b):
    return jnp.dot(a, b, preferred_element_type=F32)


def _dot_nt(a, b):
    return lax.dot_general(a, b, (((1,), (1,)), ((), ())), preferred_element_type=F32)


def _split3(a):
    hi = a.astype(BF16)
    r1 = a - hi.astype(F32)
    mid = r1.astype(BF16)
    lo = (r1 - mid.astype(F32)).astype(BF16)
    return hi, mid, lo


def _rms_kernel(x_ref, g_ref, o_ref):
    x = x_ref[...]
    ms = jnp.mean(x * x, axis=-1, keepdims=True)
    o_ref[...] = (x * lax.rsqrt(ms + EPS) * g_ref[...]).astype(o_ref.dtype)


def _rmsnorm(x, g, out_dtype=BF16):
    n, d = x.shape
    tr = min(256, n)
    return pl.pallas_call(
        _rms_kernel,
        grid=(n // tr,),
        in_specs=[pl.BlockSpec((tr, d), lambda i: (i, 0)), pl.BlockSpec((1, d), lambda i: (0, 0))],
        out_specs=pl.BlockSpec((tr, d), lambda i: (i, 0)),
        out_shape=jax.ShapeDtypeStruct((n, d), out_dtype),
        compiler_params=_cparams(("parallel",)),
    )(x, g.reshape(1, d))


def _mm_kernel(*refs, widths, has_res):
    n_x = len(widths)
    x_refs = refs[:n_x]
    w_ref = refs[n_x]
    o_ref = refs[-1]
    acc = None
    off = 0
    for x_ref, kw in zip(x_refs, widths):
        part = _dot(x_ref[...], w_ref[off:off + kw, :])
        acc = part if acc is None else acc + part
        off += kw
    if has_res:
        acc = refs[n_x + 1][...] + acc
    o_ref[...] = acc.astype(o_ref.dtype)


def _matmul(xs, w, out_dtype, residual=None):
    if not isinstance(xs, (list, tuple)):
        xs = [xs]
    m = xs[0].shape[0]
    widths = tuple(x.shape[1] for x in xs)
    k, n = w.shape
    assert sum(widths) == k
    tm = min(1024, m)
    tn = min(512, n)
    in_specs = [pl.BlockSpec((tm, kw), lambda i, j: (i, 0)) for kw in widths]
    in_specs.append(pl.BlockSpec((k, tn), lambda i, j: (0, j)))
    args = list(xs) + [w]
    if residual is not None:
        in_specs.append(pl.BlockSpec((tm, tn), lambda i, j: (i, j)))
        args.append(residual)
    return pl.pallas_call(
        functools.partial(_mm_kernel, widths=widths, has_res=residual is not None),
        grid=(m // tm, n // tn),
        in_specs=in_specs,
        out_specs=pl.BlockSpec((tm, tn), lambda i, j: (i, j)),
        out_shape=jax.ShapeDtypeStruct((m, n), out_dtype),
        compiler_params=_cparams(("parallel", "parallel"), VMEM_BIG),
    )(*args)


def _gelu_tanh(x):
    return 0.5 * x * (1.0 + jnp.tanh(np.sqrt(2.0 / np.pi) * (x + 0.044715 * (x * x * x))))


def _cmp_kernel(a_ref, w1_ref, pe_ref, w2_ref, o_ref):
    a = a_ref[0, 0]
    nc = a.shape[0]
    half = CMP_STRIDE * HEAD_DIM
    ha = _dot(a, w1_ref[0, :half, :])
    hb = _dot(a, w1_ref[0, half:, :])
    c = _dot(pe_ref[0], w1_ref[0])[0:1]
    pre = ha + pltpu.roll(hb, nc - 1, 0) + c
    hid = _gelu_tanh(pre)
    o_ref[0, 0] = _dot(hid.astype(BF16), w2_ref[0]).astype(o_ref.dtype)


def _compress(a2, w1, pe, w2):
    _, g, nc, ck = a2.shape
    hid = w1.shape[-1]
    return pl.pallas_call(
        _cmp_kernel,
        grid=(2, g),
        in_specs=[
            pl.BlockSpec((1, 1, nc, ck), lambda s, gg: (s, gg, 0, 0)),
            pl.BlockSpec((1, 2 * ck, hid), lambda s, gg: (s, 0, 0)),
            pl.BlockSpec((1, 8, 2 * ck), lambda s, gg: (s, 0, 0)),
            pl.BlockSpec((1, hid, HEAD_DIM), lambda s, gg: (s, 0, 0)),
        ],
        out_specs=pl.BlockSpec((1, 1, nc, HEAD_DIM), lambda s, gg: (s, gg, 0, 0)),
        out_shape=jax.ShapeDtypeStruct((2, g, nc, HEAD_DIM), BF16),
        compiler_params=_cparams(("parallel", "parallel"), VMEM_MID),
    )(a2, w1, pe, w2)


def _cmpattn_kernel(slopes_ref, q_ref, kc_ref, vc_ref, ov_ref, o_ref, sel_ref, *, n_sel):
    g = pl.program_id(0)
    i = pl.program_id(1)
    tq = q_ref.shape[0]
    nc = kc_ref.shape[2]
    t = i * tq + lax.broadcasted_iota(jnp.int32, (tq, 1), 0)
    n = lax.broadcasted_iota(jnp.int32, (1, nc), 1)
    dist = (t - (CMP_STRIDE * n + CMP_LEN - 1)).astype(F32)
    valid = dist >= 0
    kc = kc_ref[0, 0]
    vc = vc_ref[0, 0]
    psum = jnp.zeros((tq, nc), F32)
    for r in range(NSA_REP):
        q = q_ref[:, r * HEAD_DIM:(r + 1) * HEAD_DIM]
        slope = slopes_ref[g * NSA_REP + r]
        s = _dot_nt(q, kc) * SCALE
        s = jnp.where(valid, s - slope * dist, NEG)
        m = jnp.max(s, axis=-1, keepdims=True)
        p = jnp.where(valid, jnp.exp(s - m), 0.0)
        l = jnp.sum(p, axis=-1, keepdims=True)
        p = p / jnp.where(l > 0, l, 1.0)
        o_ref[:, r * HEAD_DIM:(r + 1) * HEAD_DIM] = _dot(p.astype(BF16), vc).astype(o_ref.dtype)
        psum = psum + p
    ov = ov_ref[...]
    hi, mid, lo = _split3(psum)
    imp = _dot(hi, ov) + _dot(mid, ov) + _dot(lo, ov)
    blk = lax.broadcasted_iota(jnp.int32, (1, V7X_LANES), 1)
    cur = t // SEL_BLOCK
    causal = blk * SEL_BLOCK <= t
    forced = (blk == 0) | (blk == cur) | (blk == cur - 1)
    work = jnp.where(causal, jnp.where(forced, FORCE, imp), NEG)
    sel = jnp.zeros((tq, V7X_LANES), F32)
    for _ in range(n_sel):
        mx = jnp.max(work, axis=-1, keepdims=True)
        idx = jnp.min(jnp.where(work == mx, blk, V7X_LANES), axis=-1, keepdims=True)
        pick = blk == idx
        sel = jnp.where(pick, 1.0, sel)
        work = jnp.where(pick, -jnp.inf, work)
    sel_ref[0] = jnp.where(causal, sel, 0.0)


def _cmp_attention(proj, kvc, slopes, overlap, n_sel):
    t = proj.shape[0]
    g = NSA_KV_GROUPS
    nc = kvc.shape[2]
    tq = min(256, t)
    qw = NSA_REP * HEAD_DIM
    return pl.pallas_call(
        functools.partial(_cmpattn_kernel, n_sel=n_sel),
        grid=(g, t // tq),
        in_specs=[
            pl.BlockSpec(memory_space=pltpu.SMEM),
            pl.BlockSpec((tq, qw), lambda gg, i: (i, gg)),
            pl.BlockSpec((1, 1, nc, HEAD_DIM), lambda gg, i: (0, gg, 0, 0)),
            pl.BlockSpec((1, 1, nc, HEAD_DIM), lambda gg, i: (1, gg, 0, 0)),
            pl.BlockSpec((nc, V7X_LANES), lambda gg, i: (0, 0)),
        ],
        out_specs=[
            pl.BlockSpec((tq, qw), lambda gg, i: (i, gg)),
            pl.BlockSpec((1, tq, V7X_LANES), lambda gg, i: (gg, i, 0)),
        ],
        out_shape=[
            jax.ShapeDtypeStruct((t, NSA_Q_W), BF16),
            jax.ShapeDtypeStruct((g, t, V7X_LANES), F32),
        ],
        compiler_params=_cparams(("parallel", "parallel"), VMEM_MID),
    )(slopes, proj, kvc, kvc, overlap)


def _flash_init(m_sc, l_sc, acc_sc):
    m_sc[...] = jnp.full(m_sc.shape, NEG, F32)
    l_sc[...] = jnp.zeros(l_sc.shape, F32)
    acc_sc[...] = jnp.zeros(acc_sc.shape, F32)


def _flash_update(r, s, v, m_sc, l_sc, acc_sc):
    m_prev = m_sc[r]
    m_new = jnp.maximum(m_prev, jnp.max(s, axis=-1, keepdims=True))
    alpha = jnp.exp(m_prev - m_new)
    p = jnp.exp(s - m_new)
    l_sc[r] = alpha * l_sc[r] + jnp.sum(p, axis=-1, keepdims=True)
    acc_sc[r] = alpha * acc_sc[r] + _dot(p.astype(BF16), v)
    m_sc[r] = m_new


def _selattn_kernel(slopes_ref, q_ref, k_ref, v_ref, sel_ref, e_ref, o_ref, m_sc, l_sc, acc_sc):
    g = pl.program_id(0)
    i = pl.program_id(1)
    j = pl.program_id(2)
    tq = q_ref.shape[0]
    tk = k_ref.shape[0]

    @pl.when(j == 0)
    def _():
        _flash_init(m_sc, l_sc, acc_sc)

    @pl.when(j <= i)
    def _():
        k = k_ref[...]
        v = v_ref[...]
        mexp = _dot(sel_ref[0].astype(BF16), e_ref[0])
        tpos = i * tq + lax.broadcasted_iota(jnp.int32, (tq, 1), 0)
        kpos = j * tk + lax.broadcasted_iota(jnp.int32, (1, tk), 1)
        dist = tpos - kpos
        ok = (mexp > 0.5) & (dist >= 0)
        distf = dist.astype(F32)
        for r in range(NSA_REP):
            slope = slopes_ref[g * NSA_REP + r]
            s = _dot_nt(q_ref[:, r * HEAD_DIM:(r + 1) * HEAD_DIM], k) * SCALE - slope * distf
            s = jnp.where(ok, s, NEG)
            _flash_update(r, s, v, m_sc, l_sc, acc_sc)

    @pl.when(j == i)
    def _():
        for r in range(NSA_REP):
            o_ref[:, r * HEAD_DIM:(r + 1) * HEAD_DIM] = (acc_sc[r] / l_sc[r]).astype(o_ref.dtype)


def _sel_attention(proj, sel, expand, slopes):
    t = proj.shape[0]
    g = NSA_KV_GROUPS
    tq = min(512, t)
    nq = t // tq
    qw = NSA_REP * HEAD_DIM
    return pl.pallas_call(
        _selattn_kernel,
        grid=(g, nq, nq),
        in_specs=[
            pl.BlockSpec(memory_space=pltpu.SMEM),
            pl.BlockSpec((tq, qw), lambda gg, i, j: (i, gg)),
            pl.BlockSpec((tq, HEAD_DIM), lambda gg, i, j: (jnp.minimum(j, i), KS_BLK + gg)),
            pl.BlockSpec((tq, HEAD_DIM), lambda gg, i, j: (jnp.minimum(j, i), VS_BLK + gg)),
            pl.BlockSpec((1, tq, V7X_LANES), lambda gg, i, j: (gg, i, 0)),
            pl.BlockSpec((1, V7X_LANES, tq), lambda gg, i, j: (jnp.minimum(j, i), 0, 0)),
        ],
        out_specs=pl.BlockSpec((tq, qw), lambda gg, i, j: (i, gg)),
        out_shape=jax.ShapeDtypeStruct((t, NSA_Q_W), BF16),
        scratch_shapes=[
            pltpu.VMEM((NSA_REP, tq, 1), F32),
            pltpu.VMEM((NSA_REP, tq, 1), F32),
            pltpu.VMEM((NSA_REP, tq, HEAD_DIM), F32),
        ],
        compiler_params=_cparams(("parallel", "parallel", "arbitrary"), VMEM_MID),
    )(slopes, proj, proj, proj, sel, expand)


def _tile_dist(i, kt, tq):
    qpos = i * tq + lax.broadcasted_iota(jnp.int32, (tq, 1), 0)
    kpos = kt * tq + lax.broadcasted_iota(jnp.int32, (1, tq), 1)
    return qpos - kpos


def _win_kernel(slopes_ref, q_ref, k_ref, v_ref, o_ref, m_sc, l_sc, acc_sc):
    g = pl.program_id(0)
    i = pl.program_id(1)
    j = pl.program_id(2)
    tq = q_ref.shape[0]
    kt = i - 1 + j

    @pl.when(j == 0)
    def _():
        _flash_init(m_sc, l_sc, acc_sc)

    @pl.when(kt >= 0)
    def _():
        k = k_ref[...]
        v = v_ref[...]
        dist = _tile_dist(i, kt, tq)
        ok = (dist >= 0) & (dist <= WIN - 1)
        distf = dist.astype(F32)
        for r in range(NSA_REP):
            slope = slopes_ref[g * NSA_REP + r]
            s = _dot_nt(q_ref[:, r * HEAD_DIM:(r + 1) * HEAD_DIM], k) * SCALE - slope * distf
            s = jnp.where(ok, s, NEG)
            _flash_update(r, s, v, m_sc, l_sc, acc_sc)

    @pl.when(j == 1)
    def _():
        for r in range(NSA_REP):
            o_ref[:, r * HEAD_DIM:(r + 1) * HEAD_DIM] = (acc_sc[r] / l_sc[r]).astype(o_ref.dtype)


def _win_attention(proj, slopes):
    t = proj.shape[0]
    tq = min(512, t)
    assert WIN - 1 <= tq
    qw = NSA_REP * HEAD_DIM
    kv_tile = lambda i, j: jnp.maximum(i - 1 + j, 0)
    return pl.pallas_call(
        _win_kernel,
        grid=(NSA_KV_GROUPS, t // tq, 2),
        in_specs=[
            pl.BlockSpec(memory_space=pltpu.SMEM),
            pl.BlockSpec((tq, qw), lambda g, i, j: (i, g)),
            pl.BlockSpec((tq, HEAD_DIM), lambda g, i, j: (kv_tile(i, j), KW_BLK + g)),
            pl.BlockSpec((tq, HEAD_DIM), lambda g, i, j: (kv_tile(i, j), VW_BLK + g)),
        ],
        out_specs=pl.BlockSpec((tq, qw), lambda g, i, j: (i, g)),
        out_shape=jax.ShapeDtypeStruct((t, NSA_Q_W), BF16),
        scratch_shapes=[
            pltpu.VMEM((NSA_REP, tq, 1), F32),
            pltpu.VMEM((NSA_REP, tq, 1), F32),
            pltpu.VMEM((NSA_REP, tq, HEAD_DIM), F32),
        ],
        compiler_params=_cparams(("parallel", "parallel", "arbitrary"), VMEM_MID),
    )(slopes, proj, proj, proj)


def _dil_tiles(tq):
    return tuple(-(-window // tq) + 1 for window, _ in DIL_CONFIGS)


def _dil_kernel(slopes_ref, *refs):
    n_cfg = len(DIL_CONFIGS)
    q_refs = refs[0:n_cfg]
    k_refs = refs[n_cfg:2 * n_cfg]
    v_refs = refs[2 * n_cfg:3 * n_cfg]
    o_refs = refs[3 * n_cfg:4 * n_cfg]
    m_sc, l_sc, acc_sc = refs[4 * n_cfg:]
    h = pl.program_id(0)
    i = pl.program_id(1)
    j = pl.program_id(2)
    tq = q_refs[0].shape[0]
    tiles = _dil_tiles(tq)
    n_steps = max(tiles)
    kt = i - (n_steps - 1) + j

    @pl.when(j == 0)
    def _():
        _flash_init(m_sc, l_sc, acc_sc)

    for c, (window, dil) in enumerate(DIL_CONFIGS):
        @pl.when((j >= n_steps - tiles[c]) & (kt >= 0))
        def _(c=c, window=window, dil=dil):
            dist = _tile_dist(i, kt, tq)
            ok = (dist >= 0) & (dist <= window) & ((dist & (dil - 1)) == 0)
            slope = slopes_ref[NSA_HEADS + c * DIL_HEADS_PER_CFG + h]
            s = _dot_nt(q_refs[c][...], k_refs[c][...]) * SCALE - slope * dist.astype(F32)
            s = jnp.where(ok, s, NEG)
            _flash_update(c, s, v_refs[c][...], m_sc, l_sc, acc_sc)

    @pl.when(j == n_steps - 1)
    def _():
        lses = [m_sc[c] + jnp.log(l_sc[c]) for c in range(n_cfg)]
        mx = functools.reduce(jnp.maximum, lses)
        es = [jnp.exp(x - mx) for x in lses]
        den = functools.reduce(lambda a, b: a + b, es)
        for c in range(n_cfg):
            o_refs[c][...] = (acc_sc[c] / l_sc[c] * (es[c] / den)).astype(o_refs[c].dtype)


def _dil_attention(proj, slopes):
    t = proj.shape[0]
    tq = min(512, t)
    n_cfg = len(DIL_CONFIGS)
    for _, dil in DIL_CONFIGS:
        assert dil & (dil - 1) == 0
    tiles = _dil_tiles(tq)
    n_steps = max(tiles)

    def kv_spec(col0, c):
        def idx(h, i, j):
            step = jnp.maximum(j - (n_steps - tiles[c]), 0)
            return (jnp.maximum(i - (tiles[c] - 1) + step, 0), col0 + c * DIL_HEADS_PER_CFG + h)
        return pl.BlockSpec((tq, HEAD_DIM), idx)

    q_specs = [pl.BlockSpec((tq, HEAD_DIM), lambda h, i, j, c=c: (i, QD_BLK + c * DIL_HEADS_PER_CFG + h))
               for c in range(n_cfg)]
    return pl.pallas_call(
        _dil_kernel,
        grid=(DIL_HEADS_PER_CFG, t // tq, n_steps),
        in_specs=[pl.BlockSpec(memory_space=pltpu.SMEM)] + q_specs
        + [kv_spec(KD_BLK, c) for c in range(n_cfg)] + [kv_spec(VD_BLK, c) for c in range(n_cfg)],
        out_specs=[pl.BlockSpec((tq, HEAD_DIM), lambda h, i, j: (i, h)) for _ in range(n_cfg)],
        out_shape=[jax.ShapeDtypeStruct((t, DIL_CFG_W), BF16) for _ in range(n_cfg)],
        scratch_shapes=[
            pltpu.VMEM((n_cfg, tq, 1), F32),
            pltpu.VMEM((n_cfg, tq, 1), F32),
            pltpu.VMEM((n_cfg, tq, HEAD_DIM), F32),
        ],
        compiler_params=_cparams(("parallel", "parallel", "arbitrary"), VMEM_MID),
    )(slopes, *([proj] * (3 * n_cfg)))


def _combine_kernel(gates_ref, ocmp_ref, osel_ref, owin_ref, o_ref):
    sg = 1.0 / (1.0 + jnp.exp(-gates_ref[...]))
    for h in range(NSA_HEADS):
        cs = slice(h * HEAD_DIM, (h + 1) * HEAD_DIM)
        o = (sg[:, 3 * h:3 * h + 1] * ocmp_ref[:, cs].astype(F32)
             + sg[:, 3 * h + 1:3 * h + 2] * osel_ref[:, cs].astype(F32)
             + sg[:, 3 * h + 2:3 * h + 3] * owin_ref[:, cs].astype(F32))
        o_ref[:, cs] = o.astype(o_ref.dtype)


def _combine(gates, o_cmp, o_sel, o_win):
    t = gates.shape[0]
    tr = min(256, t)
    row = lambda w: pl.BlockSpec((tr, w), lambda i: (i, 0))
    return pl.pallas_call(
        _combine_kernel,
        grid=(t // tr,),
        in_specs=[row(V7X_LANES), row(NSA_Q_W), row(NSA_Q_W), row(NSA_Q_W)],
        out_specs=row(NSA_Q_W),
        out_shape=jax.ShapeDtypeStruct((t, NSA_Q_W), BF16),
        compiler_params=_cparams(("parallel",), VMEM_MID),
    )(gates, o_cmp, o_sel, o_win)


def _cross_kernel(q_ref, kv_ref, wo_ref, x_ref, o_ref):
    cw = CROSS_HEADS * HEAD_DIM
    outs = []
    for h in range(CROSS_HEADS):
        q = q_ref[:, h * HEAD_DIM:(h + 1) * HEAD_DIM]
        k = kv_ref[:, h * HEAD_DIM:(h + 1) * HEAD_DIM]
        v = kv_ref[:, cw + h * HEAD_DIM:cw + (h + 1) * HEAD_DIM]
        s = _dot_nt(q, k) * SCALE
        m = jnp.max(s, axis=-1, keepdims=True)
        p = jnp.exp(s - m)
        l = jnp.sum(p, axis=-1, keepdims=True)
        outs.append((_dot(p.astype(BF16), v) / l).astype(BF16))
    o = jnp.concatenate(outs, axis=-1)
    o_ref[...] = x_ref[...] + _dot(o, wo_ref[...])


def _cross_attention(q, kv, wo, x):
    t, d = x.shape
    s_len = kv.shape[0]
    cw = CROSS_HEADS * HEAD_DIM
    tq = min(256, t)
    return pl.pallas_call(
        _cross_kernel,
        grid=(t // tq,),
        in_specs=[
            pl.BlockSpec((tq, cw), lambda i: (i, 0)),
            pl.BlockSpec((s_len, 2 * cw), lambda i: (0, 0)),
            pl.BlockSpec((cw, d), lambda i: (0, 0)),
            pl.BlockSpec((tq, d), lambda i: (i, 0)),
        ],
        out_specs=pl.BlockSpec((tq, d), lambda i: (i, 0)),
        out_shape=jax.ShapeDtypeStruct((t, d), F32),
        compiler_params=_cparams(("parallel",), VMEM_MID),
    )(q, kv, wo, x)


def _router_kernel(x_ref, g_ref, whi_ref, wmid_ref, wlo_ref, b_ref, h_ref, eid_ref, gate_ref):
    x = x_ref[...]
    tr, d = x.shape
    ms = jnp.mean(x * x, axis=-1, keepdims=True)
    h = x * lax.rsqrt(ms + EPS) * g_ref[...]
    bits = lax.bitcast_convert_type(h.astype(BF16).astype(F32), U32)
    half = d // 2
    for s in range(X_SLAB_ROWS):
        lo = bits[:, s * V7X_LANES:(s + 1) * V7X_LANES] >> 16
        hi = bits[:, half + s * V7X_LANES:half + (s + 1) * V7X_LANES] & jnp.uint32(0xFFFF0000)
        h_ref[pl.ds(s, tr, stride=X_SLAB_ROWS), :] = hi | lo
    hh, hm, hl = _split3(h)
    whi, wmid, wlo = whi_ref[...], wmid_ref[...], wlo_ref[...]
    logits = (_dot(hh, whi) + (_dot(hh, wmid) + _dot(hm, whi))
              + (_dot(hh, wlo) + _dot(hm, wmid) + _dot(hl, whi))) + b_ref[...]
    lane = lax.broadcasted_iota(jnp.int32, (1, V7X_LANES), 1)
    ninf = -jnp.inf
    is_g = lane < N_GROUPS
    lg = jnp.where(is_g, logits, ninf)
    mg = jnp.max(lg, axis=-1, keepdims=True)
    grp = jnp.min(jnp.where(lg == mg, lane, V7X_LANES), axis=-1, keepdims=True)
    p_grp = 1.0 / jnp.sum(jnp.where(is_g, jnp.exp(lg - mg), 0.0), axis=-1, keepdims=True)
    lo_lane = N_GROUPS + grp * EXPERTS_PER_GROUP
    in_grp = (lane >= lo_lane) & (lane < lo_lane + EXPERTS_PER_GROUP)
    le = jnp.where(in_grp, logits, ninf)
    v1 = jnp.max(le, axis=-1, keepdims=True)
    i1 = jnp.min(jnp.where(le == v1, lane, V7X_LANES), axis=-1, keepdims=True)
    le2 = jnp.where(lane == i1, ninf, le)
    v2 = jnp.max(le2, axis=-1, keepdims=True)
    i2 = jnp.min(jnp.where(le2 == v2, lane, V7X_LANES), axis=-1, keepdims=True)
    e2 = jnp.exp(v2 - v1)
    den = 1.0 + e2
    eid_ref[...] = jnp.where(lane == 0, i1 - N_GROUPS, jnp.where(lane == 1, i2 - N_GROUPS, 0))
    gate_ref[...] = jnp.where(lane == 0, p_grp / den, jnp.where(lane == 1, p_grp * e2 / den, 0.0))


def _router(x, g, w_hi, w_mid, w_lo, b):
    t, d = x.shape
    assert d == 2 * X_SLAB_ROWS * V7X_LANES
    tr = min(256, t)
    full = lambda shp: pl.BlockSpec(shp, lambda i: (0, 0))
    return pl.pallas_call(
        _router_kernel,
        grid=(t // tr,),
        in_specs=[pl.BlockSpec((tr, d), lambda i: (i, 0)), full((1, d)),
                  full((d, V7X_LANES)), full((d, V7X_LANES)), full((d, V7X_LANES)), full((1, V7X_LANES))],
        out_specs=[pl.BlockSpec((tr * X_SLAB_ROWS, V7X_LANES), lambda i: (i, 0)),
                   pl.BlockSpec((tr, V7X_LANES), lambda i: (i, 0)),
                   pl.BlockSpec((tr, V7X_LANES), lambda i: (i, 0))],
        out_shape=[jax.ShapeDtypeStruct((t * X_SLAB_ROWS, V7X_LANES), U32),
                   jax.ShapeDtypeStruct((t, V7X_LANES), jnp.int32),
                   jax.ShapeDtypeStruct((t, V7X_LANES), F32)],
        compiler_params=_cparams(("parallel",), VMEM_MID),
    )(x, g.reshape(1, d), w_hi, w_mid, w_lo, b)


def _slab_copy(src_ref, dst_ref, sem, src_slab, dst_slab, src_pitch, dst_pitch, rows):
    src0 = pl.multiple_of(src_slab * src_pitch, V7X_SUBLANES)
    dst0 = pl.multiple_of(dst_slab * dst_pitch, V7X_SUBLANES)
    return pltpu.make_async_copy(src_ref.at[pl.ds(src0, rows)], dst_ref.at[pl.ds(dst0, rows)], sem)


def _unpack_rows(xg, r0, rows):
    lo_parts, hi_parts = [], []
    for s in range(X_SLAB_ROWS):
        w = xg[pl.ds(r0 * X_SLAB_ROWS + s, rows, stride=X_SLAB_ROWS), :]
        lo_parts.append(lax.bitcast_convert_type(w << 16, F32).astype(BF16))
        hi_parts.append(lax.bitcast_convert_type(w & jnp.uint32(0xFFFF0000), F32).astype(BF16))
    return jnp.concatenate(lo_parts + hi_parts, axis=1)


def _moe_kernel(ue_ref, un_ref, tab_ref, hp_ref, wg_ref, wu_ref, wd_ref, rw_ref, yt_ref,
                xg, acc, yst, gsem, ssem, cnt):
    u = pl.program_id(0)
    f = pl.program_id(1)
    n_units = pl.num_programs(0)
    n_f = pl.num_programs(1)
    n = un_ref[u]
    unit_rows = acc.shape[0]

    def x_copy(k, tok):
        return _slab_copy(hp_ref, xg, gsem, tok, k, X_SLAB_ROWS, X_SLAB_ROWS, X_SLAB_ROWS)

    def y_copy(k, slot):
        return _slab_copy(yst, yt_ref, ssem, k, slot, Y_SLAB_PITCH, Y_SLAB_PITCH, Y_SLAB_PITCH)

    def drain_scatter():
        def body(k, c):
            y_copy(0, 0).wait()
            return c
        lax.fori_loop(0, cnt[0], body, 0)
        cnt[0] = 0

    @pl.when((u == 0) & (f == 0))
    def _():
        xg[...] = jnp.zeros(xg.shape, xg.dtype)
        yst[...] = jnp.zeros(yst.shape, yst.dtype)
        cnt[0] = 0

    @pl.when((f == 0) & (n > 0))
    def _():
        def issue(k, c):
            x_copy(k, tab_ref[0, 0, k]).start()
            return c
        lax.fori_loop(0, n, issue, 0)

        def drain(k, c):
            x_copy(0, 0).wait()
            return c
        lax.fori_loop(0, n, drain, 0)
        acc[...] = jnp.zeros(acc.shape, acc.dtype)

    @pl.when(n > 0)
    def _():
        wg = wg_ref[0].astype(BF16)
        wu = wu_ref[0].astype(BF16)
        wd = wd_ref[0].astype(BF16)
        d = acc.shape[1]
        col_chunk = min(1024, d)

        def sub_block(r0, rows):
            x = _unpack_rows(xg, r0, rows)
            gt = _dot(x, wg)
            up = _dot(x, wu)
            a = (gt * (1.0 / (1.0 + jnp.exp(-gt))) * up).astype(BF16)
            for c0 in range(0, d, col_chunk):
                acc[r0:r0 + rows, c0:c0 + col_chunk] += _dot(a, wd[:, c0:c0 + col_chunk])

        sub_block(0, MOE_SUB_ROWS)

        @pl.when(n > MOE_SUB_ROWS)
        def _():
            sub_block(MOE_SUB_ROWS, unit_rows - MOE_SUB_ROWS)

    @pl.when((f == n_f - 1) & (n > 0))
    def _():
        drain_scatter()
        y = acc[...] * rw_ref[0]
        for s in range(Y_SLAB_ROWS):
            yst[pl.ds(s, unit_rows, stride=Y_SLAB_PITCH), :] = y[:, s * V7X_LANES:(s + 1) * V7X_LANES]

        def scatter(k, c):
            y_copy(k, tab_ref[0, 0, unit_rows + k]).start()
            return c
        lax.fori_loop(0, n, scatter, 0)
        cnt[0] = n

    @pl.when((u == n_units - 1) & (f == n_f - 1))
    def _():
        drain_scatter()


def _moe_experts(hp, tab, row_w, unit_e, unit_n, w_gate, w_up, w_down, t):
    n_units = unit_e.shape[0]
    r = MOE_UNIT_ROWS
    _, d, d_exp = w_gate.shape
    assert d == Y_SLAB_ROWS * V7X_LANES
    tf = min(256, d_exp)
    nf = d_exp // tf

    def wchunk(u, f, ue, un):
        return jnp.where(un[u] > 0, f, nf - 1)

    return pl.pallas_call(
        _moe_kernel,
        grid_spec=pltpu.PrefetchScalarGridSpec(
            num_scalar_prefetch=2,
            grid=(n_units, nf),
            in_specs=[
                pl.BlockSpec((1, 1, 2 * r), lambda u, f, ue, un: (u, 0, 0), memory_space=pltpu.SMEM),
                pl.BlockSpec(memory_space=pl.ANY),
                pl.BlockSpec((1, d, tf), lambda u, f, ue, un: (ue[u], 0, wchunk(u, f, ue, un))),
                pl.BlockSpec((1, d, tf), lambda u, f, ue, un: (ue[u], 0, wchunk(u, f, ue, un))),
                pl.BlockSpec((1, tf, d), lambda u, f, ue, un: (ue[u], wchunk(u, f, ue, un), 0)),
                pl.BlockSpec((1, r, 1), lambda u, f, ue, un: (u, 0, 0)),
            ],
            out_specs=pl.BlockSpec(memory_space=pl.ANY),
            scratch_shapes=[
                pltpu.VMEM((r * X_SLAB_ROWS, V7X_LANES), U32),
                pltpu.VMEM((r, d), F32),
                pltpu.VMEM((r * Y_SLAB_PITCH, V7X_LANES), F32),
                pltpu.SemaphoreType.DMA(()),
                pltpu.SemaphoreType.DMA(()),
                pltpu.SMEM((1,), jnp.int32),
            ],
        ),
        out_shape=jax.ShapeDtypeStruct((EXPERT_TOPK * t * Y_SLAB_PITCH, V7X_LANES), F32),
        compiler_params=_cparams(("arbitrary", "arbitrary"), VMEM_BIG),
    )(unit_e, unit_n, tab, hp, w_gate, w_up, w_down, row_w)


def _final_kernel(x_ref, y0_ref, y1_ref, g_ref, o_ref):
    tq = x_ref.shape[0]
    parts = []
    for s in range(Y_SLAB_ROWS):
        parts.append(y0_ref[pl.ds(s, tq, stride=Y_SLAB_PITCH), :] + y1_ref[pl.ds(s, tq, stride=Y_SLAB_PITCH), :])
    x = x_ref[...] + jnp.concatenate(parts, axis=1)
    ms = jnp.mean(x * x, axis=-1, keepdims=True)
    o_ref[...] = x * lax.rsqrt(ms + EPS) * g_ref[...]


def _final(x2, yt, g_final):
    t, d = x2.shape
    tq = min(128, t)
    nt = t // tq
    return pl.pallas_call(
        _final_kernel,
        grid=(nt,),
        in_specs=[
            pl.BlockSpec((tq, d), lambda i: (i, 0)),
            pl.BlockSpec((tq * Y_SLAB_PITCH, V7X_LANES), lambda i: (i, 0)),
            pl.BlockSpec((tq * Y_SLAB_PITCH, V7X_LANES), lambda i: (nt + i, 0)),
            pl.BlockSpec((1, d), lambda i: (0, 0)),
        ],
        out_specs=pl.BlockSpec((tq, d), lambda i: (i, 0)),
        out_shape=jax.ShapeDtypeStruct((t, d), F32),
        compiler_params=_cparams(("parallel",), VMEM_MID),
    )(x2, yt, yt, g_final.reshape(1, d))


def _alibi_slopes(n):
    return jnp.exp2(-8.0 * jnp.arange(1, n + 1, dtype=F32) / n)


def _hybrid_mixer(x, norm_g, w_in, w_out, pe_k, w1_k, w2_k, pe_v, w1_v, w2_v):
    t, d = x.shape
    assert t % (CMP_STRIDE * 8) == 0 and t // SEL_BLOCK <= V7X_LANES
    h = _rmsnorm(x, norm_g)

    g0 = NSA_Q_W + 6 * NSA_KV_W
    w_main = jnp.concatenate([w_in[:, :g0], w_in[:, g0 + NSA_GATE_W:]], axis=1).astype(BF16)
    w_gates = jnp.pad(w_in[:, g0:g0 + NSA_GATE_W], ((0, 0), (0, V7X_LANES - NSA_GATE_W))).astype(BF16)
    proj = _matmul(h, w_main, BF16)
    gates = _matmul(h, w_gates, F32)
    slopes = _alibi_slopes(N_MIX_HEADS)

    nc = t // CMP_STRIDE
    kv_raw = proj[:, KC_BLK * HEAD_DIM:KS_BLK * HEAD_DIM]
    a2 = kv_raw.reshape(nc, CMP_STRIDE, 2, NSA_KV_GROUPS, HEAD_DIM).transpose(2, 3, 0, 1, 4)
    a2 = a2.reshape(2, NSA_KV_GROUPS, nc, CMP_STRIDE * HEAD_DIM)
    w1 = jnp.stack([w1_k, w1_v]).astype(BF16)
    w2 = jnp.stack([w2_k, w2_v]).astype(BF16)
    pe = jnp.stack([pe_k, pe_v]).reshape(2, 1, CMP_LEN * HEAD_DIM)
    pe = jnp.broadcast_to(pe, (2, 8, CMP_LEN * HEAD_DIM)).astype(BF16)
    kvc = _compress(a2, w1, pe, w2)

    n_blk = t // SEL_BLOCK
    n_sel = min(SEL_TOPK, n_blk)
    ci = CMP_STRIDE * np.arange(nc)[:, None]
    sj = SEL_BLOCK * np.arange(V7X_LANES)[None, :]
    overlap = jnp.asarray(((ci < sj + SEL_BLOCK) & (ci + CMP_LEN > sj)).astype(np.float32), dtype=BF16)
    o_cmp, sel = _cmp_attention(proj, kvc, slopes, overlap, n_sel)

    tq = min(512, t)
    kblk = (np.arange(t) // SEL_BLOCK).reshape(t // tq, 1, tq)
    expand = jnp.asarray((kblk == np.arange(V7X_LANES)[None, :, None]).astype(np.float32), dtype=BF16)
    o_sel = _sel_attention(proj, sel, expand, slopes)

    o_win = _win_attention(proj, slopes)
    o_dil = _dil_attention(proj, slopes)
    o_nsa = _combine(gates, o_cmp, o_sel, o_win)
    return _matmul([o_nsa] + list(o_dil), w_out.astype(BF16), F32, residual=x)


def _cross_block(x, mem, norm_cross, norm_mem, wq, wkv, wo):
    h = _rmsnorm(x, norm_cross)
    m = _rmsnorm(mem, norm_mem)
    q = _matmul(h, wq.astype(BF16), BF16)
    kv = _matmul(m, wkv.astype(BF16), BF16)
    return _cross_attention(q, kv, wo.astype(BF16), x)


def _moe_tables(eid, gate, t):
    r = MOE_UNIT_ROWS
    n_assign = t * EXPERT_TOPK
    e_flat = eid[:, :EXPERT_TOPK].reshape(n_assign)
    w_flat = gate[:, :EXPERT_TOPK].reshape(n_assign)
    order = jnp.argsort(e_flat).astype(jnp.int32)
    e_s = e_flat[order]
    counts = jnp.zeros((N_EXPERTS,), jnp.int32).at[e_flat].add(1)
    starts = jnp.cumsum(counts) - counts
    pcounts = (counts + r - 1) // r * r
    pends = jnp.cumsum(pcounts)
    pstarts = pends - pcounts
    row = (pstarts[e_s] + (jnp.arange(n_assign, dtype=jnp.int32) - starts[e_s])).astype(jnp.int32)
    n_units = n_assign // r + N_EXPERTS
    n_rows = n_units * r
    tok_s = order // EXPERT_TOPK
    slot_s = (order % EXPERT_TOPK) * t + tok_s
    row_tok = jnp.zeros((n_rows,), jnp.int32).at[row].set(tok_s)
    row_slot = jnp.zeros((n_rows,), jnp.int32).at[row].set(slot_s)
    row_w = jnp.zeros((n_rows,), F32).at[row].set(w_flat[order])
    unit_start = jnp.arange(n_units, dtype=jnp.int32) * r
    active = unit_start < pends[-1]
    e_raw = jnp.minimum(jnp.sum(pends[None, :] <= unit_start[:, None], axis=1), N_EXPERTS - 1).astype(jnp.int32)
    e_last = e_raw[pends[-1] // r - 1]
    unit_e = jnp.where(active, e_raw, e_last)
    unit_n = jnp.where(active, jnp.clip(counts[unit_e] - (unit_start - pstarts[unit_e]), 0, r), 0).astype(jnp.int32)
    tab = jnp.concatenate([row_tok.reshape(n_units, 1, r), row_slot.reshape(n_units, 1, r)], axis=2)
    return tab, row_w.reshape(n_units, r, 1), unit_e, unit_n


def _moe_block(x, norm_ffn, w_rg, b_rg, w_re, b_re, w_gate, w_up, w_down, norm_final):
    t, d = x.shape
    pad = V7X_LANES - N_GROUPS - N_EXPERTS
    w_r = jnp.pad(jnp.concatenate([w_rg, w_re], axis=1), ((0, 0), (0, pad)))
    b_r = jnp.pad(jnp.concatenate([b_rg, b_re]), (0, pad)).reshape(1, V7X_LANES)
    w_hi, w_mid, w_lo = _split3(w_r)
    hp, eid, gate = _router(x, norm_ffn, w_hi, w_mid, w_lo, b_r)
    tab, row_w, unit_e, unit_n = _moe_tables(eid, gate, t)
    yt = _moe_experts(hp, tab, row_w, unit_e, unit_n, w_gate, w_up, w_down, t)
    return _final(x, yt, norm_final)


def kernel(x, mem, norm_mix, w_in, w_out, cmp_pe_k, cmp_w1_k, cmp_w2_k, cmp_pe_v, cmp_w1_v, cmp_w2_v, norm_cross, norm_mem, w_q_cross, w_kv_cross, w_o_cross, norm_ffn, w_router_group, b_router_group, w_router_expert, b_router_expert, w_gate, w_up, w_down, norm_final):
    b, t, d = x.shape
    depth = norm_mix.shape[0]
    assert b == 1 and depth == 1
    xs = x.reshape(t, d)
    l = 0
    xs = _hybrid_mixer(xs, norm_mix[l], w_in[l], w_out[l], cmp_pe_k[l], cmp_w1_k[l], cmp_w2_k[l],
                       cmp_pe_v[l], cmp_w1_v[l], cmp_w2_v[l])
    xs = _cross_block(xs, mem.reshape(mem.shape[1], d), norm_cross[l], norm_mem[l], w_q_cross[l],
                      w_kv_cross[l], w_o_cross[l])
    out = _moe_block(xs, norm_ffn[l], w_router_group[l], b_router_group[l], w_router_expert[l],
                     b_router_expert[l], w_gate[l], w_up[l], w_down[l], norm_final)
    return out.reshape(b, t, d)
```

```python
import functools

import numpy as np
import jax
import jax.numpy as jnp
from jax import lax
from jax.experimental import pallas as pl
from jax.experimental.pallas import tpu as pltpu

F32 = jnp.float32
BF16 = jnp.bfloat16

HEAD_DIM = 128
DIL_CONFIGS = ((128, 1), (512, 4), (2048, 16))
DIL_HEADS_PER_CFG = 4
DIL_HEADS = DIL_HEADS_PER_CFG * len(DIL_CONFIGS)
N_MIX_HEADS = 32
NSA_HEADS = N_MIX_HEADS - DIL_HEADS
NSA_KV_GROUPS = 4
NSA_REP = NSA_HEADS // NSA_KV_GROUPS
CMP_LEN = 32
CMP_STRIDE = 16
SEL_BLOCK = 64
SEL_TOPK = 16
WIN = 512
CROSS_HEADS = 4
N_GROUPS = 8
EXPERTS_PER_GROUP = 8
N_EXPERTS = N_GROUPS * EXPERTS_PER_GROUP
EXPERT_TOPK = 2
EPS = 1e-6
NEG = -1e30
FORCE = 1e4
SCALE = HEAD_DIM ** -0.5

V7X_LANES = 128
V7X_SUBLANES = 8
V7X_VMEM_BYTES = 64 * 1024 * 1024
VMEM_BIG = 56 * 1024 * 1024
VMEM_MID = 40 * 1024 * 1024

QN_BLK = 0
KC_BLK = NSA_HEADS
VC_BLK = KC_BLK + NSA_KV_GROUPS
KS_BLK = VC_BLK + NSA_KV_GROUPS
VS_BLK = KS_BLK + NSA_KV_GROUPS
KW_BLK = VS_BLK + NSA_KV_GROUPS
VW_BLK = KW_BLK + NSA_KV_GROUPS
QD_BLK = VW_BLK + NSA_KV_GROUPS
KD_BLK = QD_BLK + DIL_HEADS
VD_BLK = KD_BLK + DIL_HEADS
PROJ_BLKS = VD_BLK + DIL_HEADS
NSA_Q_W = NSA_HEADS * HEAD_DIM
NSA_KV_W = NSA_KV_GROUPS * HEAD_DIM
NSA_GATE_W = NSA_HEADS * 3
DIL_CFG_W = DIL_HEADS_PER_CFG * HEAD_DIM

SLAB_ROWS = 32
SLAB_PITCH = 40
MOE_UNIT_ROWS = 320
MOE_SUB_ROWS = 256


def _cparams(sem, vmem=None):
    return pltpu.CompilerParams(dimension_semantics=sem, vmem_limit_bytes=vmem)


def _dot(a, b):
    return jnp.dot(a, b, preferred_element_type=F32)


def _dot_nt(a, b):
    return lax.dot_general(a, b, (((1,), (1,)), ((), ())), preferred_element_type=F32)


def _split3(a):
    hi = a.astype(BF16)
    r1 = a - hi.astype(F32)
    mid = r1.astype(BF16)
    lo = (r1 - mid.astype(F32)).astype(BF16)
    return hi, mid, lo


def _rms_kernel(x_ref, g_ref, o_ref):
    x = x_ref[...]
    ms = jnp.mean(x * x, axis=-1, keepdims=True)
    o_ref[...] = (x * lax.rsqrt(ms + EPS) * g_ref[...]).astype(o_ref.dtype)


def _rmsnorm(x, g, out_dtype=BF16):
    n, d = x.shape
    tr = min(256, n)
    return pl.pallas_call(
        _rms_kernel,
        grid=(n // tr,),
        in_specs=[pl.BlockSpec((tr, d), lambda i: (i, 0)), pl.BlockSpec((1, d), lambda i: (0, 0))],
        out_specs=pl.BlockSpec((tr, d), lambda i: (i, 0)),
        out_shape=jax.ShapeDtypeStruct((n, d), out_dtype),
        compiler_params=_cparams(("parallel",)),
    )(x, g.reshape(1, d))


def _mm_kernel(*refs, widths, has_res):
    n_x = len(widths)
    x_refs = refs[:n_x]
    w_ref = refs[n_x]
    o_ref = refs[-1]
    acc = None
    off = 0
    for x_ref, kw in zip(x_refs, widths):
        part = _dot(x_ref[...], w_ref[off:off + kw, :])
        acc = part if acc is None else acc + part
        off += kw
    if has_res:
        acc = refs[n_x + 1][...] + acc
    o_ref[...] = acc.astype(o_ref.dtype)


def _matmul(xs, w, out_dtype, residual=None):
    if not isinstance(xs, (list, tuple)):
        xs = [xs]
    m = xs[0].shape[0]
    widths = tuple(x.shape[1] for x in xs)
    k, n = w.shape
    assert sum(widths) == k
    tm = min(1024, m)
    tn = min(512, n)
    in_specs = [pl.BlockSpec((tm, kw), lambda i, j: (i, 0)) for kw in widths]
    in_specs.append(pl.BlockSpec((k, tn), lambda i, j: (0, j)))
    args = list(xs) + [w]
    if residual is not None:
        in_specs.append(pl.BlockSpec((tm, tn), lambda i, j: (i, j)))
        args.append(residual)
    return pl.pallas_call(
        functools.partial(_mm_kernel, widths=widths, has_res=residual is not None),
        grid=(m // tm, n // tn),
        in_specs=in_specs,
        out_specs=pl.BlockSpec((tm, tn), lambda i, j: (i, j)),
        out_shape=jax.ShapeDtypeStruct((m, n), out_dtype),
        compiler_params=_cparams(("parallel", "parallel"), VMEM_BIG),
    )(*args)


def _gelu_tanh(x):
    return 0.5 * x * (1.0 + jnp.tanh(np.sqrt(2.0 / np.pi) * (x + 0.044715 * (x * x * x))))


def _cmp_kernel(a_ref, w1_ref, pe_ref, w2_ref, o_ref):
    a = a_ref[0, 0]
    nc = a.shape[0]
    half = CMP_STRIDE * HEAD_DIM
    ha = _dot(a, w1_ref[0, :half, :])
    hb = _dot(a, w1_ref[0, half:, :])
    c = _dot(pe_ref[0], w1_ref[0])[0:1]
    pre = ha + pltpu.roll(hb, nc - 1, 0) + c
    hid = _gelu_tanh(pre)
    o_ref[0, 0] = _dot(hid.astype(BF16), w2_ref[0]).astype(o_ref.dtype)


def _compress(a2, w1, pe, w2):
    _, g, nc, ck = a2.shape
    hid = w1.shape[-1]
    return pl.pallas_call(
        _cmp_kernel,
        grid=(2, g),
        in_specs=[
            pl.BlockSpec((1, 1, nc, ck), lambda s, gg: (s, gg, 0, 0)),
            pl.BlockSpec((1, 2 * ck, hid), lambda s, gg: (s, 0, 0)),
            pl.BlockSpec((1, 8, 2 * ck), lambda s, gg: (s, 0, 0)),
            pl.BlockSpec((1, hid, HEAD_DIM), lambda s, gg: (s, 0, 0)),
        ],
        out_specs=pl.BlockSpec((1, 1, nc, HEAD_DIM), lambda s, gg: (s, gg, 0, 0)),
        out_shape=jax.ShapeDtypeStruct((2, g, nc, HEAD_DIM), BF16),
        compiler_params=_cparams(("parallel", "parallel"), VMEM_MID),
    )(a2, w1, pe, w2)


def _cmpattn_kernel(slopes_ref, q_ref, kc_ref, vc_ref, ov_ref, o_ref, sel_ref,
                    s_sc, p_sc, hi_sc, mid_sc, lo_sc, *, n_sel):
    g = pl.program_id(0)
    i = pl.program_id(1)
    tq = q_ref.shape[0]
    nc = kc_ref.shape[2]
    rc = SOFTMAX_CHUNK
    _flash_scores(q_ref, kc_ref[0, 0], NSA_REP, s_sc)
    slopes = [slopes_ref[g * NSA_REP + r] for r in range(NSA_REP)]

    def body(c, carry):
        r0 = pl.multiple_of(c * rc, rc)
        tpos = i * tq + r0 + lax.broadcasted_iota(jnp.int32, (rc, 1), 0)
        n = lax.broadcasted_iota(jnp.int32, (1, nc), 1)
        dist = (tpos - (CMP_STRIDE * n + CMP_LEN - 1)).astype(F32)
        valid = dist >= 0
        psum = jnp.zeros((rc, nc), F32)
        for r, slope in enumerate(slopes):
            srow = pl.ds(r * tq + r0, rc)
            s = jnp.where(valid, s_sc[srow, :] - slope * dist, NEG)
            m = jnp.max(s, axis=-1, keepdims=True)
            p = jnp.where(valid, jnp.exp(s - m), 0.0)
            l = jnp.sum(p, axis=-1, keepdims=True)
            p = p * (1.0 / jnp.where(l > 0, l, 1.0))
            p_sc[srow, :] = p.astype(BF16)
            psum = psum + p
        hi, mid, lo = _split3(psum)
        rows = pl.ds(r0, rc)
        hi_sc[rows, :] = hi
        mid_sc[rows, :] = mid
        lo_sc[rows, :] = lo
        return carry

    lax.fori_loop(0, tq // rc, body, 0)
    pv = _dot(p_sc[...], vc_ref[0, 0])
    for r in range(NSA_REP):
        o_ref[:, r * HEAD_DIM:(r + 1) * HEAD_DIM] = pv[r * tq:(r + 1) * tq, :].astype(o_ref.dtype)
    ov = ov_ref[...]
    imp = _dot(hi_sc[...], ov) + _dot(mid_sc[...], ov) + _dot(lo_sc[...], ov)
    t = i * tq + lax.broadcasted_iota(jnp.int32, (tq, 1), 0)
    blk = lax.broadcasted_iota(jnp.int32, (1, V7X_LANES), 1)
    cur = t // SEL_BLOCK
    causal = blk * SEL_BLOCK <= t
    forced = (blk == 0) | (blk == cur) | (blk == cur - 1)
    work = jnp.where(causal, jnp.where(forced, FORCE, imp), NEG)
    sel = jnp.zeros((tq, V7X_LANES), F32)
    for _ in range(n_sel):
        mx = jnp.max(work, axis=-1, keepdims=True)
        idx = jnp.min(jnp.where(work == mx, blk, V7X_LANES), axis=-1, keepdims=True)
        pick = blk == idx
        sel = jnp.where(pick, 1.0, sel)
        work = jnp.where(pick, -jnp.inf, work)
    sel_ref[0] = jnp.where(causal, sel, 0.0)


def _cmp_attention(proj, kvc, slopes, overlap, n_sel):
    t = proj.shape[0]
    g = NSA_KV_GROUPS
    nc = kvc.shape[2]
    tq = min(256, t)
    qw = NSA_REP * HEAD_DIM
    return pl.pallas_call(
        functools.partial(_cmpattn_kernel, n_sel=n_sel),
        grid=(g, t // tq),
        in_specs=[
            pl.BlockSpec(memory_space=pltpu.SMEM),
            pl.BlockSpec((tq, qw), lambda gg, i: (i, gg)),
            pl.BlockSpec((1, 1, nc, HEAD_DIM), lambda gg, i: (0, gg, 0, 0)),
            pl.BlockSpec((1, 1, nc, HEAD_DIM), lambda gg, i: (1, gg, 0, 0)),
            pl.BlockSpec((nc, V7X_LANES), lambda gg, i: (0, 0)),
        ],
        out_specs=[
            pl.BlockSpec((tq, qw), lambda gg, i: (i, gg)),
            pl.BlockSpec((1, tq, V7X_LANES), lambda gg, i: (gg, i, 0)),
        ],
        out_shape=[
            jax.ShapeDtypeStruct((t, NSA_Q_W), BF16),
            jax.ShapeDtypeStruct((g, t, V7X_LANES), F32),
        ],
        scratch_shapes=[
            pltpu.VMEM((NSA_REP * tq, nc), F32),
            pltpu.VMEM((NSA_REP * tq, nc), BF16),
            pltpu.VMEM((tq, nc), BF16),
            pltpu.VMEM((tq, nc), BF16),
            pltpu.VMEM((tq, nc), BF16),
        ],
        compiler_params=_cparams(("parallel", "parallel"), VMEM_MID),
    )(slopes, proj, kvc, kvc, overlap)


SOFTMAX_CHUNK = 16


def _flash_init(m_sc, l_sc, acc_sc):
    m_sc[...] = jnp.full(m_sc.shape, NEG, F32)
    l_sc[...] = jnp.zeros(l_sc.shape, F32)
    acc_sc[...] = jnp.zeros(acc_sc.shape, F32)


def _tile_dist_rows(q0, k0, rows, cols):
    return (q0 + lax.broadcasted_iota(jnp.int32, (rows, 1), 0)) - (k0 + lax.broadcasted_iota(jnp.int32, (1, cols), 1))


def _flash_scores(q_ref, k, n_rep, s_sc, row0=0):
    tq = q_ref.shape[0]
    qs = [q_ref[:, r * HEAD_DIM:(r + 1) * HEAD_DIM] for r in range(n_rep)]
    q = qs[0] if n_rep == 1 else jnp.concatenate(qs, axis=0)
    s_sc[row0:row0 + n_rep * tq, :] = _dot_nt(q, k)


def _flash_softmax(s_sc, p_sc, m_sc, l_sc, a_sc, *, tq, slopes, masks, s_row0=0, stat_row0=0):
    rc = SOFTMAX_CHUNK

    def body(c, carry):
        r0 = pl.multiple_of(c * rc, rc)
        distf, madd = masks(r0)
        for r, slope in enumerate(slopes):
            srow = pl.ds(s_row0 + r * tq + r0, rc)
            trow = pl.ds(stat_row0 + r * tq + r0, rc)
            t = (s_sc[srow, :] - slope * distf) + madd
            m_prev = m_sc[trow, :]
            m_new = jnp.maximum(m_prev, jnp.max(t, axis=-1, keepdims=True))
            alpha = jnp.exp(m_prev - m_new)
            p = jnp.exp(t - m_new)
            l_sc[trow, :] = alpha * l_sc[trow, :] + jnp.sum(p, axis=-1, keepdims=True)
            m_sc[trow, :] = m_new
            a_sc[trow, :] = alpha
            p_sc[srow, :] = p.astype(BF16)
        return carry

    lax.fori_loop(0, tq // rc, body, 0)


def _flash_accumulate(p_sc, v, a_sc, acc_sc, rows, s_row0=0, stat_row0=0):
    srows = slice(s_row0, s_row0 + rows)
    trows = slice(stat_row0, stat_row0 + rows)
    acc_sc[trows, :] = a_sc[trows, :] * acc_sc[trows, :] + _dot(p_sc[srows, :], v)


def _flash_scratch(n_rows, tk):
    return [
        pltpu.VMEM((n_rows, tk), F32),
        pltpu.VMEM((n_rows, tk), BF16),
        pltpu.VMEM((n_rows, 1), F32),
        pltpu.VMEM((n_rows, 1), F32),
        pltpu.VMEM((n_rows, 1), F32),
        pltpu.VMEM((n_rows, HEAD_DIM), F32),
    ]


def _flash_store(o_ref, acc_sc, l_sc, n_rep, tq):
    for r in range(n_rep):
        rows = slice(r * tq, (r + 1) * tq)
        o_ref[:, r * HEAD_DIM:(r + 1) * HEAD_DIM] = (acc_sc[rows, :] / l_sc[rows, :]).astype(o_ref.dtype)


def _selattn_kernel(pi_ref, pj_ref, slopes_ref, q_ref, k_ref, v_ref, sel_ref, e_ref, o_ref,
                    madd_sc, s_sc, p_sc, m_sc, l_sc, a_sc, acc_sc):
    g = pl.program_id(0)
    step = pl.program_id(1)
    i = pi_ref[step]
    j = pj_ref[step]
    tq = q_ref.shape[0]
    tk = k_ref.shape[0]

    @pl.when(j == 0)
    def _():
        _flash_init(m_sc, l_sc, acc_sc)

    _flash_scores(q_ref, k_ref[...], NSA_REP, s_sc)
    madd_sc[...] = (_dot(sel_ref[0].astype(BF16), e_ref[0]) - 1.0) * (-NEG)

    def masks(r0):
        dist = _tile_dist_rows(i * tq + r0, j * tk, SOFTMAX_CHUNK, tk)
        madd = jnp.where(dist >= 0, madd_sc[pl.ds(r0, SOFTMAX_CHUNK), :], NEG)
        return dist.astype(F32), madd

    slopes = [slopes_ref[g * NSA_REP + r] for r in range(NSA_REP)]
    _flash_softmax(s_sc, p_sc, m_sc, l_sc, a_sc, tq=tq, slopes=slopes, masks=masks)
    _flash_accumulate(p_sc, v_ref[...], a_sc, acc_sc, NSA_REP * tq)

    @pl.when(j == i)
    def _():
        _flash_store(o_ref, acc_sc, l_sc, NSA_REP, tq)


def _sel_attention(proj, sel, expand, slopes):
    t = proj.shape[0]
    g = NSA_KV_GROUPS
    tq = min(512, t)
    nq = t // tq
    qw = NSA_REP * HEAD_DIM
    pairs = [(i, j) for i in range(nq) for j in range(i + 1)]
    pi = jnp.asarray([p[0] for p in pairs], jnp.int32)
    pj = jnp.asarray([p[1] for p in pairs], jnp.int32)
    return pl.pallas_call(
        _selattn_kernel,
        grid_spec=pltpu.PrefetchScalarGridSpec(
            num_scalar_prefetch=2,
            grid=(g, len(pairs)),
            in_specs=[
                pl.BlockSpec(memory_space=pltpu.SMEM),
                pl.BlockSpec((tq, qw), lambda gg, s, pi, pj: (pi[s], gg)),
                pl.BlockSpec((tq, HEAD_DIM), lambda gg, s, pi, pj: (pj[s], KS_BLK + gg)),
                pl.BlockSpec((tq, HEAD_DIM), lambda gg, s, pi, pj: (pj[s], VS_BLK + gg)),
                pl.BlockSpec((1, tq, V7X_LANES), lambda gg, s, pi, pj: (gg, pi[s], 0)),
                pl.BlockSpec((1, V7X_LANES, tq), lambda gg, s, pi, pj: (pj[s], 0, 0)),
            ],
            out_specs=pl.BlockSpec((tq, qw), lambda gg, s, pi, pj: (pi[s], gg)),
            scratch_shapes=[pltpu.VMEM((tq, tq), F32)] + _flash_scratch(NSA_REP * tq, tq),
        ),
        out_shape=jax.ShapeDtypeStruct((t, NSA_Q_W), BF16),
        compiler_params=_cparams(("parallel", "arbitrary"), VMEM_MID),
    )(pi, pj, slopes, proj, proj, proj, sel, expand)


def _win_kernel(slopes_ref, q_ref, k_ref, v_ref, o_ref, s_sc, p_sc, m_sc, l_sc, a_sc, acc_sc):
    g = pl.program_id(0)
    i = pl.program_id(1)
    j = pl.program_id(2)
    tq = q_ref.shape[0]
    kt = i - 1 + j

    @pl.when(j == 0)
    def _():
        _flash_init(m_sc, l_sc, acc_sc)

    @pl.when(kt >= 0)
    def _():
        _flash_scores(q_ref, k_ref[...], NSA_REP, s_sc)

        def masks(r0):
            dist = _tile_dist_rows(i * tq + r0, kt * tq, SOFTMAX_CHUNK, tq)
            ok = (dist >= 0) & (dist <= WIN - 1)
            return dist.astype(F32), jnp.where(ok, 0.0, NEG)

        slopes = [slopes_ref[g * NSA_REP + r] for r in range(NSA_REP)]
        _flash_softmax(s_sc, p_sc, m_sc, l_sc, a_sc, tq=tq, slopes=slopes, masks=masks)
        _flash_accumulate(p_sc, v_ref[...], a_sc, acc_sc, NSA_REP * tq)

    @pl.when(j == 1)
    def _():
        _flash_store(o_ref, acc_sc, l_sc, NSA_REP, tq)


def _win_attention(proj, slopes):
    t = proj.shape[0]
    tq = min(512, t)
    assert WIN - 1 <= tq
    qw = NSA_REP * HEAD_DIM
    kv_tile = lambda i, j: jnp.maximum(i - 1 + j, 0)
    return pl.pallas_call(
        _win_kernel,
        grid=(NSA_KV_GROUPS, t // tq, 2),
        in_specs=[
            pl.BlockSpec(memory_space=pltpu.SMEM),
            pl.BlockSpec((tq, qw), lambda g, i, j: (i, g)),
            pl.BlockSpec((tq, HEAD_DIM), lambda g, i, j: (kv_tile(i, j), KW_BLK + g)),
            pl.BlockSpec((tq, HEAD_DIM), lambda g, i, j: (kv_tile(i, j), VW_BLK + g)),
        ],
        out_specs=pl.BlockSpec((tq, qw), lambda g, i, j: (i, g)),
        out_shape=jax.ShapeDtypeStruct((t, NSA_Q_W), BF16),
        scratch_shapes=_flash_scratch(NSA_REP * tq, tq),
        compiler_params=_cparams(("parallel", "parallel", "arbitrary"), VMEM_MID),
    )(slopes, proj, proj, proj)


def _dil_tiles(tq):
    return tuple(-(-window // tq) + 1 for window, _ in DIL_CONFIGS)


def _dil_kernel(slopes_ref, *refs):
    n_cfg = len(DIL_CONFIGS)
    q_refs = refs[0:n_cfg]
    k_refs = refs[n_cfg:2 * n_cfg]
    v_refs = refs[2 * n_cfg:3 * n_cfg]
    o_refs = refs[3 * n_cfg:4 * n_cfg]
    s_sc, p_sc, m_sc, l_sc, a_sc, acc_sc = refs[4 * n_cfg:]
    h = pl.program_id(0)
    i = pl.program_id(1)
    j = pl.program_id(2)
    tq = q_refs[0].shape[0]
    tiles = _dil_tiles(tq)
    n_steps = max(tiles)
    kt = i - (n_steps - 1) + j

    @pl.when(j == 0)
    def _():
        _flash_init(m_sc, l_sc, acc_sc)

    for c, (window, dil) in enumerate(DIL_CONFIGS):
        @pl.when((j >= n_steps - tiles[c]) & (kt >= 0))
        def _(c=c, window=window, dil=dil):
            _flash_scores(q_refs[c], k_refs[c][...], 1, s_sc)

            def masks(r0):
                dist = _tile_dist_rows(i * tq + r0, kt * tq, SOFTMAX_CHUNK, tq)
                ok = (dist >= 0) & (dist <= window) & ((dist & (dil - 1)) == 0)
                return dist.astype(F32), jnp.where(ok, 0.0, NEG)

            slope = slopes_ref[NSA_HEADS + c * DIL_HEADS_PER_CFG + h]
            _flash_softmax(s_sc, p_sc, m_sc, l_sc, a_sc, tq=tq, slopes=[slope], masks=masks, stat_row0=c * tq)
            _flash_accumulate(p_sc, v_refs[c][...], a_sc, acc_sc, tq, stat_row0=c * tq)

    @pl.when(j == n_steps - 1)
    def _():
        rows = [slice(c * tq, (c + 1) * tq) for c in range(n_cfg)]
        lses = [m_sc[rw, :] + jnp.log(l_sc[rw, :]) for rw in rows]
        mx = functools.reduce(jnp.maximum, lses)
        es = [jnp.exp(x - mx) for x in lses]
        den = functools.reduce(lambda a, b: a + b, es)
        for c in range(n_cfg):
            o = acc_sc[rows[c], :] / l_sc[rows[c], :] * (es[c] / den)
            o_refs[c][...] = o.astype(o_refs[c].dtype)


def _dil_attention(proj, slopes):
    t = proj.shape[0]
    tq = min(512, t)
    n_cfg = len(DIL_CONFIGS)
    for _, dil in DIL_CONFIGS:
        assert dil & (dil - 1) == 0
    tiles = _dil_tiles(tq)
    n_steps = max(tiles)

    def kv_spec(col0, c):
        def idx(h, i, j):
            step = jnp.maximum(j - (n_steps - tiles[c]), 0)
            return (jnp.maximum(i - (tiles[c] - 1) + step, 0), col0 + c * DIL_HEADS_PER_CFG + h)
        return pl.BlockSpec((tq, HEAD_DIM), idx)

    q_specs = [pl.BlockSpec((tq, HEAD_DIM), lambda h, i, j, c=c: (i, QD_BLK + c * DIL_HEADS_PER_CFG + h))
               for c in range(n_cfg)]
    return pl.pallas_call(
        _dil_kernel,
        grid=(DIL_HEADS_PER_CFG, t // tq, n_steps),
        in_specs=[pl.BlockSpec(memory_space=pltpu.SMEM)] + q_specs
        + [kv_spec(KD_BLK, c) for c in range(n_cfg)] + [kv_spec(VD_BLK, c) for c in range(n_cfg)],
        out_specs=[pl.BlockSpec((tq, HEAD_DIM), lambda h, i, j: (i, h)) for _ in range(n_cfg)],
        out_shape=[jax.ShapeDtypeStruct((t, DIL_CFG_W), BF16) for _ in range(n_cfg)],
        scratch_shapes=_flash_scratch(tq, tq)[:2] + _flash_scratch(n_cfg * tq, tq)[2:],
        compiler_params=_cparams(("parallel", "parallel", "arbitrary"), VMEM_MID),
    )(slopes, *([proj] * (3 * n_cfg)))


def _combine_kernel(gates_ref, ocmp_ref, osel_ref, owin_ref, o_ref):
    sg = 1.0 / (1.0 + jnp.exp(-gates_ref[...]))
    for h in range(NSA_HEADS):
        cs = slice(h * HEAD_DIM, (h + 1) * HEAD_DIM)
        o = (sg[:, 3 * h:3 * h + 1] * ocmp_ref[:, cs].astype(F32)
             + sg[:, 3 * h + 1:3 * h + 2] * osel_ref[:, cs].astype(F32)
             + sg[:, 3 * h + 2:3 * h + 3] * owin_ref[:, cs].astype(F32))
        o_ref[:, cs] = o.astype(o_ref.dtype)


def _combine(gates, o_cmp, o_sel, o_win):
    t = gates.shape[0]
    tr = min(256, t)
    row = lambda w: pl.BlockSpec((tr, w), lambda i: (i, 0))
    return pl.pallas_call(
        _combine_kernel,
        grid=(t // tr,),
        in_specs=[row(V7X_LANES), row(NSA_Q_W), row(NSA_Q_W), row(NSA_Q_W)],
        out_specs=row(NSA_Q_W),
        out_shape=jax.ShapeDtypeStruct((t, NSA_Q_W), BF16),
        compiler_params=_cparams(("parallel",), VMEM_MID),
    )(gates, o_cmp, o_sel, o_win)


def _cross_kernel(q_ref, kv_ref, wo_ref, x_ref, o_ref):
    cw = CROSS_HEADS * HEAD_DIM
    outs = []
    for h in range(CROSS_HEADS):
        q = q_ref[:, h * HEAD_DIM:(h + 1) * HEAD_DIM]
        k = kv_ref[:, h * HEAD_DIM:(h + 1) * HEAD_DIM]
        v = kv_ref[:, cw + h * HEAD_DIM:cw + (h + 1) * HEAD_DIM]
        s = _dot_nt(q, k) * SCALE
        m = jnp.max(s, axis=-1, keepdims=True)
        p = jnp.exp(s - m)
        l = jnp.sum(p, axis=-1, keepdims=True)
        outs.append((_dot(p.astype(BF16), v) / l).astype(BF16))
    o = jnp.concatenate(outs, axis=-1)
    o_ref[...] = x_ref[...] + _dot(o, wo_ref[...])


def _cross_attention(q, kv, wo, x):
    t, d = x.shape
    s_len = kv.shape[0]
    cw = CROSS_HEADS * HEAD_DIM
    tq = min(256, t)
    return pl.pallas_call(
        _cross_kernel,
        grid=(t // tq,),
        in_specs=[
            pl.BlockSpec((tq, cw), lambda i: (i, 0)),
            pl.BlockSpec((s_len, 2 * cw), lambda i: (0, 0)),
            pl.BlockSpec((cw, d), lambda i: (0, 0)),
            pl.BlockSpec((tq, d), lambda i: (i, 0)),
        ],
        out_specs=pl.BlockSpec((tq, d), lambda i: (i, 0)),
        out_shape=jax.ShapeDtypeStruct((t, d), F32),
        compiler_params=_cparams(("parallel",), VMEM_MID),
    )(q, kv, wo, x)


def _router_kernel(x_ref, g_ref, whi_ref, wmid_ref, wlo_ref, b_ref, h_ref, eid_ref, gate_ref):
    x = x_ref[...]
    tr, d = x.shape
    ms = jnp.mean(x * x, axis=-1, keepdims=True)
    h = x * lax.rsqrt(ms + EPS) * g_ref[...]
    h_ref[...] = jnp.zeros(h_ref.shape, h_ref.dtype)
    for s in range(SLAB_ROWS):
        h_ref[pl.ds(s, tr, stride=SLAB_PITCH), :] = h[:, s * V7X_LANES:(s + 1) * V7X_LANES]
    hh, hm, hl = _split3(h)
    whi, wmid, wlo = whi_ref[...], wmid_ref[...], wlo_ref[...]
    logits = (_dot(hh, whi) + (_dot(hh, wmid) + _dot(hm, whi))
              + (_dot(hh, wlo) + _dot(hm, wmid) + _dot(hl, whi))) + b_ref[...]
    lane = lax.broadcasted_iota(jnp.int32, (1, V7X_LANES), 1)
    ninf = -jnp.inf
    is_g = lane < N_GROUPS
    lg = jnp.where(is_g, logits, ninf)
    mg = jnp.max(lg, axis=-1, keepdims=True)
    grp = jnp.min(jnp.where(lg == mg, lane, V7X_LANES), axis=-1, keepdims=True)
    p_grp = 1.0 / jnp.sum(jnp.where(is_g, jnp.exp(lg - mg), 0.0), axis=-1, keepdims=True)
    lo_lane = N_GROUPS + grp * EXPERTS_PER_GROUP
    in_grp = (lane >= lo_lane) & (lane < lo_lane + EXPERTS_PER_GROUP)
    le = jnp.where(in_grp, logits, ninf)
    v1 = jnp.max(le, axis=-1, keepdims=True)
    i1 = jnp.min(jnp.where(le == v1, lane, V7X_LANES), axis=-1, keepdims=True)
    le2 = jnp.where(lane == i1, ninf, le)
    v2 = jnp.max(le2, axis=-1, keepdims=True)
    i2 = jnp.min(jnp.where(le2 == v2, lane, V7X_LANES), axis=-1, keepdims=True)
    e2 = jnp.exp(v2 - v1)
    den = 1.0 + e2
    eid_ref[...] = jnp.where(lane == 0, i1 - N_GROUPS, jnp.where(lane == 1, i2 - N_GROUPS, 0))
    gate_ref[...] = jnp.where(lane == 0, p_grp / den, jnp.where(lane == 1, p_grp * e2 / den, 0.0))


def _router(x, g, w_hi, w_mid, w_lo, b):
    t, d = x.shape
    assert d == SLAB_ROWS * V7X_LANES
    tr = min(256, t)
    full = lambda shp: pl.BlockSpec(shp, lambda i: (0, 0))
    return pl.pallas_call(
        _router_kernel,
        grid=(t // tr,),
        in_specs=[pl.BlockSpec((tr, d), lambda i: (i, 0)), full((1, d)),
                  full((d, V7X_LANES)), full((d, V7X_LANES)), full((d, V7X_LANES)), full((1, V7X_LANES))],
        out_specs=[pl.BlockSpec((tr * SLAB_PITCH, V7X_LANES), lambda i: (i, 0)),
                   pl.BlockSpec((tr, V7X_LANES), lambda i: (i, 0)),
                   pl.BlockSpec((tr, V7X_LANES), lambda i: (i, 0))],
        out_shape=[jax.ShapeDtypeStruct((t * SLAB_PITCH, V7X_LANES), F32),
                   jax.ShapeDtypeStruct((t, V7X_LANES), jnp.int32),
                   jax.ShapeDtypeStruct((t, V7X_LANES), F32)],
        compiler_params=_cparams(("parallel",), VMEM_MID),
    )(x, g.reshape(1, d), w_hi, w_mid, w_lo, b)


def _slab_copy(src_ref, dst_ref, sem, src_slab, dst_slab, src_pitch, dst_pitch, rows):
    src0 = pl.multiple_of(src_slab * src_pitch, V7X_SUBLANES)
    dst0 = pl.multiple_of(dst_slab * dst_pitch, V7X_SUBLANES)
    return pltpu.make_async_copy(src_ref.at[pl.ds(src0, rows)], dst_ref.at[pl.ds(dst0, rows)], sem)


def _unpack_rows(xg, r0, rows):
    parts = [xg[pl.ds(r0 * SLAB_PITCH + s, rows, stride=SLAB_PITCH), :].astype(BF16) for s in range(SLAB_ROWS)]
    return jnp.concatenate(parts, axis=1)


def _moe_kernel(ue_ref, un_ref, tab_ref, hp_ref, wg_ref, wu_ref, wd_ref, rw_ref, yt_ref,
                xg, acc, yst, gsem, ssem, cnt):
    u = pl.program_id(0)
    f = pl.program_id(1)
    n_units = pl.num_programs(0)
    n_f = pl.num_programs(1)
    n = un_ref[u]
    unit_rows = acc.shape[0]

    def x_copy(k, tok):
        return _slab_copy(hp_ref, xg, gsem, tok, k, SLAB_PITCH, SLAB_PITCH, SLAB_ROWS)

    def y_copy(k, slot):
        return _slab_copy(yst, yt_ref, ssem, k, slot, SLAB_PITCH, SLAB_PITCH, SLAB_PITCH)

    def drain_scatter():
        def body(k, c):
            y_copy(0, 0).wait()
            return c
        lax.fori_loop(0, cnt[0], body, 0)
        cnt[0] = 0

    @pl.when((u == 0) & (f == 0))
    def _():
        xg[...] = jnp.zeros(xg.shape, xg.dtype)
        yst[...] = jnp.zeros(yst.shape, yst.dtype)
        cnt[0] = 0

    @pl.when((f == 0) & (n > 0))
    def _():
        def issue(k, c):
            x_copy(k, tab_ref[0, 0, k]).start()
            return c
        lax.fori_loop(0, n, issue, 0)

        def drain(k, c):
            x_copy(0, 0).wait()
            return c
        lax.fori_loop(0, n, drain, 0)
        acc[...] = jnp.zeros(acc.shape, acc.dtype)

    @pl.when(n > 0)
    def _():
        wg = wg_ref[0].astype(BF16)
        wu = wu_ref[0].astype(BF16)
        wd = wd_ref[0].astype(BF16)
        d = acc.shape[1]
        col_chunk = min(1024, d)

        def sub_block(r0, rows):
            x = _unpack_rows(xg, r0, rows)
            gt = _dot(x, wg)
            up = _dot(x, wu)
            a = (gt * (1.0 / (1.0 + jnp.exp(-gt))) * up).astype(BF16)
            for c0 in range(0, d, col_chunk):
                acc[r0:r0 + rows, c0:c0 + col_chunk] += _dot(a, wd[:, c0:c0 + col_chunk])

        sub_block(0, MOE_SUB_ROWS)

        @pl.when(n > MOE_SUB_ROWS)
        def _():
            sub_block(MOE_SUB_ROWS, unit_rows - MOE_SUB_ROWS)

    @pl.when((f == n_f - 1) & (n > 0))
    def _():
        drain_scatter()
        rw = rw_ref[0]
        for s in range(SLAB_ROWS):
            yst[pl.ds(s, unit_rows, stride=SLAB_PITCH), :] = acc[:, s * V7X_LANES:(s + 1) * V7X_LANES] * rw

        def scatter(k, c):
            y_copy(k, tab_ref[0, 0, unit_rows + k]).start()
            return c
        lax.fori_loop(0, n, scatter, 0)
        cnt[0] = n

    @pl.when((u == n_units - 1) & (f == n_f - 1))
    def _():
        drain_scatter()


def _moe_experts(hp, tab, row_w, unit_e, unit_n, w_gate, w_up, w_down, t):
    n_units = unit_e.shape[0]
    r = MOE_UNIT_ROWS
    _, d, d_exp = w_gate.shape
    assert d == SLAB_ROWS * V7X_LANES
    tf = min(256, d_exp)
    nf = d_exp // tf

    def wchunk(u, f, ue, un):
        return jnp.where(un[u] > 0, f, nf - 1)

    return pl.pallas_call(
        _moe_kernel,
        grid_spec=pltpu.PrefetchScalarGridSpec(
            num_scalar_prefetch=2,
            grid=(n_units, nf),
            in_specs=[
                pl.BlockSpec((1, 1, 2 * r), lambda u, f, ue, un: (u, 0, 0), memory_space=pltpu.SMEM),
                pl.BlockSpec(memory_space=pl.ANY),
                pl.BlockSpec((1, d, tf), lambda u, f, ue, un: (ue[u], 0, wchunk(u, f, ue, un))),
                pl.BlockSpec((1, d, tf), lambda u, f, ue, un: (ue[u], 0, wchunk(u, f, ue, un))),
                pl.BlockSpec((1, tf, d), lambda u, f, ue, un: (ue[u], wchunk(u, f, ue, un), 0)),
                pl.BlockSpec((1, r, 1), lambda u, f, ue, un: (u, 0, 0)),
            ],
            out_specs=pl.BlockSpec(memory_space=pl.ANY),
            scratch_shapes=[
                pltpu.VMEM((r * SLAB_PITCH, V7X_LANES), F32),
                pltpu.VMEM((r, d), F32),
                pltpu.VMEM((r * SLAB_PITCH, V7X_LANES), F32),
                pltpu.SemaphoreType.DMA(()),
                pltpu.SemaphoreType.DMA(()),
                pltpu.SMEM((1,), jnp.int32),
            ],
        ),
        out_shape=jax.ShapeDtypeStruct((EXPERT_TOPK * t * SLAB_PITCH, V7X_LANES), F32),
        compiler_params=_cparams(("arbitrary", "arbitrary"), VMEM_BIG),
    )(unit_e, unit_n, tab, hp, w_gate, w_up, w_down, row_w)


def _final_kernel(x_ref, y0_ref, y1_ref, g_ref, o_ref):
    tq = x_ref.shape[0]
    parts = []
    for s in range(SLAB_ROWS):
        parts.append(y0_ref[pl.ds(s, tq, stride=SLAB_PITCH), :] + y1_ref[pl.ds(s, tq, stride=SLAB_PITCH), :])
    x = x_ref[...] + jnp.concatenate(parts, axis=1)
    ms = jnp.mean(x * x, axis=-1, keepdims=True)
    o_ref[...] = x * lax.rsqrt(ms + EPS) * g_ref[...]


def _final(x2, yt, g_final):
    t, d = x2.shape
    tq = min(128, t)
    nt = t // tq
    return pl.pallas_call(
        _final_kernel,
        grid=(nt,),
        in_specs=[
            pl.BlockSpec((tq, d), lambda i: (i, 0)),
            pl.BlockSpec((tq * SLAB_PITCH, V7X_LANES), lambda i: (i, 0)),
            pl.BlockSpec((tq * SLAB_PITCH, V7X_LANES), lambda i: (nt + i, 0)),
            pl.BlockSpec((1, d), lambda i: (0, 0)),
        ],
        out_specs=pl.BlockSpec((tq, d), lambda i: (i, 0)),
        out_shape=jax.ShapeDtypeStruct((t, d), F32),
        compiler_params=_cparams(("parallel",), VMEM_MID),
    )(x2, yt, yt, g_final.reshape(1, d))


def _alibi_slopes(n):
    return jnp.exp2(-8.0 * jnp.arange(1, n + 1, dtype=F32) / n)


def _hybrid_mixer(x, norm_g, w_in, w_out, pe_k, w1_k, w2_k, pe_v, w1_v, w2_v):
    t, d = x.shape
    assert t % (CMP_STRIDE * 8) == 0 and t // SEL_BLOCK <= V7X_LANES
    h = _rmsnorm(x, norm_g)

    g0 = NSA_Q_W + 6 * NSA_KV_W
    col_scale = np.ones((PROJ_BLKS * HEAD_DIM,), np.float32)
    col_scale[QN_BLK * HEAD_DIM:KC_BLK * HEAD_DIM] = SCALE
    col_scale[QD_BLK * HEAD_DIM:KD_BLK * HEAD_DIM] = SCALE
    w_main = jnp.concatenate([w_in[:, :g0], w_in[:, g0 + NSA_GATE_W:]], axis=1)
    w_main = (w_main * col_scale).astype(BF16)
    w_gates = jnp.pad(w_in[:, g0:g0 + NSA_GATE_W], ((0, 0), (0, V7X_LANES - NSA_GATE_W))).astype(BF16)
    proj = _matmul(h, w_main, BF16)
    gates = _matmul(h, w_gates, F32)
    slopes = _alibi_slopes(N_MIX_HEADS)

    nc = t // CMP_STRIDE
    kv_raw = proj[:, KC_BLK * HEAD_DIM:KS_BLK * HEAD_DIM]
    a2 = kv_raw.reshape(nc, CMP_STRIDE, 2, NSA_KV_GROUPS, HEAD_DIM).transpose(2, 3, 0, 1, 4)
    a2 = a2.reshape(2, NSA_KV_GROUPS, nc, CMP_STRIDE * HEAD_DIM)
    w1 = jnp.stack([w1_k, w1_v]).astype(BF16)
    w2 = jnp.stack([w2_k, w2_v]).astype(BF16)
    pe = jnp.stack([pe_k, pe_v]).reshape(2, 1, CMP_LEN * HEAD_DIM)
    pe = jnp.broadcast_to(pe, (2, 8, CMP_LEN * HEAD_DIM)).astype(BF16)
    kvc = _compress(a2, w1, pe, w2)

    n_blk = t // SEL_BLOCK
    n_sel = min(SEL_TOPK, n_blk)
    ci = CMP_STRIDE * np.arange(nc)[:, None]
    sj = SEL_BLOCK * np.arange(V7X_LANES)[None, :]
    overlap = jnp.asarray(((ci < sj + SEL_BLOCK) & (ci + CMP_LEN > sj)).astype(np.float32), dtype=BF16)
    o_cmp, sel = _cmp_attention(proj, kvc, slopes, overlap, n_sel)

    tq = min(512, t)
    kblk = (np.arange(t) // SEL_BLOCK).reshape(t // tq, 1, tq)
    expand = jnp.asarray((kblk == np.arange(V7X_LANES)[None, :, None]).astype(np.float32), dtype=BF16)
    o_sel = _sel_attention(proj, sel, expand, slopes)

    o_win = _win_attention(proj, slopes)
    o_dil = _dil_attention(proj, slopes)
    o_nsa = _combine(gates, o_cmp, o_sel, o_win)
    return _matmul([o_nsa] + list(o_dil), w_out.astype(BF16), F32, residual=x)


def _cross_block(x, mem, norm_cross, norm_mem, wq, wkv, wo):
    h = _rmsnorm(x, norm_cross)
    m = _rmsnorm(mem, norm_mem)
    q = _matmul(h, wq.astype(BF16), BF16)
    kv = _matmul(m, wkv.astype(BF16), BF16)
    return _cross_attention(q, kv, wo.astype(BF16), x)


def _moe_tables(eid, gate, t):
    r = MOE_UNIT_ROWS
    n_assign = t * EXPERT_TOPK
    e_flat = eid[:, :EXPERT_TOPK].reshape(n_assign)
    w_flat = gate[:, :EXPERT_TOPK].reshape(n_assign)
    order = jnp.argsort(e_flat).astype(jnp.int32)
    e_s = e_flat[order]
    counts = jnp.zeros((N_EXPERTS,), jnp.int32).at[e_flat].add(1)
    starts = jnp.cumsum(counts) - counts
    pcounts = (counts + r - 1) // r * r
    pends = jnp.cumsum(pcounts)
    pstarts = pends - pcounts
    row = (pstarts[e_s] + (jnp.arange(n_assign, dtype=jnp.int32) - starts[e_s])).astype(jnp.int32)
    n_units = n_assign // r + N_EXPERTS
    n_rows = n_units * r
    tok_s = order // EXPERT_TOPK
    slot_s = (order % EXPERT_TOPK) * t + tok_s
    row_tok = jnp.zeros((n_rows,), jnp.int32).at[row].set(tok_s)
    row_slot = jnp.zeros((n_rows,), jnp.int32).at[row].set(slot_s)
    row_w = jnp.zeros((n_rows,), F32).at[row].set(w_flat[order])
    unit_start = jnp.arange(n_units, dtype=jnp.int32) * r
    active = unit_start < pends[-1]
    e_raw = jnp.minimum(jnp.sum(pends[None, :] <= unit_start[:, None], axis=1), N_EXPERTS - 1).astype(jnp.int32)
    e_last = e_raw[pends[-1] // r - 1]
    unit_e = jnp.where(active, e_raw, e_last)
    unit_n = jnp.where(active, jnp.clip(counts[unit_e] - (unit_start - pstarts[unit_e]), 0, r), 0).astype(jnp.int32)
    tab = jnp.concatenate([row_tok.reshape(n_units, 1, r), row_slot.reshape(n_units, 1, r)], axis=2)
    return tab, row_w.reshape(n_units, r, 1), unit_e, unit_n


def _moe_block(x, norm_ffn, w_rg, b_rg, w_re, b_re, w_gate, w_up, w_down, norm_final):
    t, d = x.shape
    pad = V7X_LANES - N_GROUPS - N_EXPERTS
    w_r = jnp.pad(jnp.concatenate([w_rg, w_re], axis=1), ((0, 0), (0, pad)))
    b_r = jnp.pad(jnp.concatenate([b_rg, b_re]), (0, pad)).reshape(1, V7X_LANES)
    w_hi, w_mid, w_lo = _split3(w_r)
    hp, eid, gate = _router(x, norm_ffn, w_hi, w_mid, w_lo, b_r)
    tab, row_w, unit_e, unit_n = _moe_tables(eid, gate, t)
    yt = _moe_experts(hp, tab, row_w, unit_e, unit_n, w_gate, w_up, w_down, t)
    return _final(x, yt, norm_final)


def kernel(x, mem, norm_mix, w_in, w_out, cmp_pe_k, cmp_w1_k, cmp_w2_k, cmp_pe_v, cmp_w1_v, cmp_w2_v, norm_cross, norm_mem, w_q_cross, w_kv_cross, w_o_cross, norm_ffn, w_router_group, b_router_group, w_router_expert, b_router_expert, w_gate, w_up, w_down, norm_final):
    b, t, d = x.shape
    depth = norm_mix.shape[0]
    assert b == 1 and depth == 1
    xs = x.reshape(t, d)
    l = 0
    xs = _hybrid_mixer(xs, norm_mix[l], w_in[l], w_out[l], cmp_pe_k[l], cmp_w1_k[l], cmp_w2_k[l],
                       cmp_pe_v[l], cmp_w1_v[l], cmp_w2_v[l])
    xs = _cross_block(xs, mem.reshape(mem.shape[1], d), norm_cross[l], norm_mem[l], w_q_cross[l],
                      w_kv_cross[l], w_o_cross[l])
    out = _moe_block(xs, norm_ffn[l], w_router_group[l], b_router_group[l], w_router_expert[l],
                     b_router_expert[l], w_gate[l], w_up[l], w_down[l], norm_final)
    return out.reshape(b, t, d)
```

```python
import functools

import numpy as np
import jax
import jax.numpy as jnp
from jax import lax
from jax.experimental import pallas as pl
from jax.experimental.pallas import tpu as pltpu

F32 = jnp.float32
BF16 = jnp.bfloat16

HEAD_DIM = 128
DIL_CONFIGS = ((128, 1), (512, 4), (2048, 16))
DIL_HEADS_PER_CFG = 4
DIL_HEADS = DIL_HEADS_PER_CFG * len(DIL_CONFIGS)
N_MIX_HEADS = 32
NSA_HEADS = N_MIX_HEADS - DIL_HEADS
NSA_KV_GROUPS = 4
NSA_REP = NSA_HEADS // NSA_KV_GROUPS
CMP_LEN = 32
CMP_STRIDE = 16
SEL_BLOCK = 64
SEL_TOPK = 16
WIN = 512
CROSS_HEADS = 4
N_GROUPS = 8
EXPERTS_PER_GROUP = 8
N_EXPERTS = N_GROUPS * EXPERTS_PER_GROUP
EXPERT_TOPK = 2
EPS = 1e-6
NEG = -1e30
FORCE = 1e4
SCALE = HEAD_DIM ** -0.5
LOG2E = float(np.log2(np.e))

V7X_LANES = 128
V7X_SUBLANES = 8
V7X_VMEM_BYTES = 64 * 1024 * 1024
VMEM_BIG = 56 * 1024 * 1024
VMEM_MID = 40 * 1024 * 1024

QN_BLK = 0
KC_BLK = NSA_HEADS
VC_BLK = KC_BLK + NSA_KV_GROUPS
KS_BLK = VC_BLK + NSA_KV_GROUPS
VS_BLK = KS_BLK + NSA_KV_GROUPS
KW_BLK = VS_BLK + NSA_KV_GROUPS
VW_BLK = KW_BLK + NSA_KV_GROUPS
QD_BLK = VW_BLK + NSA_KV_GROUPS
KD_BLK = QD_BLK + DIL_HEADS
VD_BLK = KD_BLK + DIL_HEADS
PROJ_BLKS = VD_BLK + DIL_HEADS
NSA_Q_W = NSA_HEADS * HEAD_DIM
NSA_KV_W = NSA_KV_GROUPS * HEAD_DIM
NSA_GATE_W = NSA_HEADS * 3
DIL_CFG_W = DIL_HEADS_PER_CFG * HEAD_DIM

SLAB_ROWS = 32
SLAB_PITCH = 40
MOE_UNIT_ROWS = 320
MOE_SUB_ROWS = 256


def _cparams(sem, vmem=None):
    return pltpu.CompilerParams(dimension_semantics=sem, vmem_limit_bytes=vmem)


def _dot(a, b):
    return jnp.dot(a, b, preferred_element_type=F32)


def _dot_nt(a, b):
    return lax.dot_general(a, b, (((1,), (1,)), ((), ())), preferred_element_type=F32)


def _split3(a):
    hi = a.astype(BF16)
    r1 = a - hi.astype(F32)
    mid = r1.astype(BF16)
    lo = (r1 - mid.astype(F32)).astype(BF16)
    return hi, mid, lo


def _rms_kernel(x_ref, g_ref, o_ref):
    x = x_ref[...]
    ms = jnp.mean(x * x, axis=-1, keepdims=True)
    o_ref[...] = (x * lax.rsqrt(ms + EPS) * g_ref[...]).astype(o_ref.dtype)


def _rmsnorm(x, g, out_dtype=BF16):
    n, d = x.shape
    tr = min(256, n)
    return pl.pallas_call(
        _rms_kernel,
        grid=(n // tr,),
        in_specs=[pl.BlockSpec((tr, d), lambda i: (i, 0)), pl.BlockSpec((1, d), lambda i: (0, 0))],
        out_specs=pl.BlockSpec((tr, d), lambda i: (i, 0)),
        out_shape=jax.ShapeDtypeStruct((n, d), out_dtype),
        compiler_params=_cparams(("parallel",)),
    )(x, g.reshape(1, d))


def _mm_kernel(*refs, widths, has_res):
    n_x = len(widths)
    x_refs = refs[:n_x]
    w_ref = refs[n_x]
    o_ref = refs[-1]
    acc = None
    off = 0
    for x_ref, kw in zip(x_refs, widths):
        part = _dot(x_ref[...], w_ref[off:off + kw, :])
        acc = part if acc is None else acc + part
        off += kw
    if has_res:
        acc = refs[n_x + 1][...] + acc
    o_ref[...] = acc.astype(o_ref.dtype)


def _matmul(xs, w, out_dtype, residual=None):
    if not isinstance(xs, (list, tuple)):
        xs = [xs]
    m = xs[0].shape[0]
    widths = tuple(x.shape[1] for x in xs)
    k, n = w.shape
    assert sum(widths) == k
    tm = min(1024, m)
    tn = min(512, n)
    in_specs = [pl.BlockSpec((tm, kw), lambda i, j: (i, 0)) for kw in widths]
    in_specs.append(pl.BlockSpec((k, tn), lambda i, j: (0, j)))
    args = list(xs) + [w]
    if residual is not None:
        in_specs.append(pl.BlockSpec((tm, tn), lambda i, j: (i, j)))
        args.append(residual)
    return pl.pallas_call(
        functools.partial(_mm_kernel, widths=widths, has_res=residual is not None),
        grid=(m // tm, n // tn),
        in_specs=in_specs,
        out_specs=pl.BlockSpec((tm, tn), lambda i, j: (i, j)),
        out_shape=jax.ShapeDtypeStruct((m, n), out_dtype),
        compiler_params=_cparams(("parallel", "parallel"), VMEM_BIG),
    )(*args)


def _gelu_tanh(x):
    return 0.5 * x * (1.0 + jnp.tanh(np.sqrt(2.0 / np.pi) * (x + 0.044715 * (x * x * x))))


def _cmp_kernel(a_ref, w1_ref, pe_ref, w2_ref, o_ref):
    a = a_ref[0, 0]
    nc = a.shape[0]
    half = CMP_STRIDE * HEAD_DIM
    ha = _dot(a, w1_ref[0, :half, :])
    hb = _dot(a, w1_ref[0, half:, :])
    c = _dot(pe_ref[0], w1_ref[0])[0:1]
    pre = ha + pltpu.roll(hb, nc - 1, 0) + c
    hid = _gelu_tanh(pre)
    o_ref[0, 0] = _dot(hid.astype(BF16), w2_ref[0]).astype(o_ref.dtype)


def _compress(a2, w1, pe, w2):
    _, g, nc, ck = a2.shape
    hid = w1.shape[-1]
    return pl.pallas_call(
        _cmp_kernel,
        grid=(2, g),
        in_specs=[
            pl.BlockSpec((1, 1, nc, ck), lambda s, gg: (s, gg, 0, 0)),
            pl.BlockSpec((1, 2 * ck, hid), lambda s, gg: (s, 0, 0)),
            pl.BlockSpec((1, 8, 2 * ck), lambda s, gg: (s, 0, 0)),
            pl.BlockSpec((1, hid, HEAD_DIM), lambda s, gg: (s, 0, 0)),
        ],
        out_specs=pl.BlockSpec((1, 1, nc, HEAD_DIM), lambda s, gg: (s, gg, 0, 0)),
        out_shape=jax.ShapeDtypeStruct((2, g, nc, HEAD_DIM), BF16),
        compiler_params=_cparams(("parallel", "parallel"), VMEM_MID),
    )(a2, w1, pe, w2)


def _cmpattn_kernel(slopes_ref, q_ref, kc_ref, vc_ref, ov_ref, o_ref, sel_ref, *, n_sel, row0):
    g = pl.program_id(0)
    i = pl.program_id(1)
    tq = q_ref.shape[0]
    nc = kc_ref.shape[2]
    t = row0 + i * tq + lax.broadcasted_iota(jnp.int32, (tq, 1), 0)
    n = lax.broadcasted_iota(jnp.int32, (1, nc), 1)
    dist = (t - (CMP_STRIDE * n + CMP_LEN - 1)).astype(F32)
    madd = jnp.where(dist >= 0, 0.0, NEG)
    has_block = t >= CMP_LEN - 1
    kc = kc_ref[0, 0]
    vc = vc_ref[0, 0]
    psum = jnp.zeros((tq, nc), F32)
    for r in range(NSA_REP):
        slope = slopes_ref[g * NSA_REP + r]
        s = (_dot_nt(q_ref[:, r * HEAD_DIM:(r + 1) * HEAD_DIM], kc) - slope * dist) + madd
        m = jnp.max(s, axis=-1, keepdims=True)
        e = jnp.exp2(s - m)
        l = jnp.sum(e, axis=-1, keepdims=True)
        p = e * jnp.where(has_block, 1.0 / l, 0.0)
        o_ref[:, r * HEAD_DIM:(r + 1) * HEAD_DIM] = _dot(p.astype(BF16), vc).astype(o_ref.dtype)
        psum = psum + p
    ov = ov_ref[...]
    hi, mid, lo = _split3(psum)
    imp = _dot(hi, ov) + _dot(mid, ov) + _dot(lo, ov)
    blk = lax.broadcasted_iota(jnp.int32, (1, V7X_LANES), 1)
    cur = t // SEL_BLOCK
    causal = blk * SEL_BLOCK <= t
    forced = (blk == 0) | (blk == cur) | (blk == cur - 1)
    work = jnp.where(causal, jnp.where(forced, FORCE, imp), NEG)
    sel = jnp.zeros((tq, V7X_LANES), F32)
    for _ in range(n_sel):
        mx = jnp.max(work, axis=-1, keepdims=True)
        idx = jnp.min(jnp.where(work == mx, blk, V7X_LANES), axis=-1, keepdims=True)
        pick = blk == idx
        sel = jnp.where(pick, 1.0, sel)
        work = jnp.where(pick, -jnp.inf, work)
    sel_ref[0] = jnp.where(causal, sel, 0.0)


CMP_SEGMENT = 2048


def _cmp_attention(proj, kvc, slopes, overlap, n_sel):
    t = proj.shape[0]
    g = NSA_KV_GROUPS
    tq = min(512, t)
    qw = NSA_REP * HEAD_DIM
    seg = min(CMP_SEGMENT, t)
    assert t % seg == 0 and seg % tq == 0 and seg % (CMP_STRIDE * V7X_SUBLANES) == 0
    outs, sels = [], []
    for q in range(t // seg):
        nc = (q + 1) * seg // CMP_STRIDE
        tile0 = q * seg // tq
        o_q, sel_q = pl.pallas_call(
            functools.partial(_cmpattn_kernel, n_sel=n_sel, row0=q * seg),
            grid=(g, seg // tq),
            in_specs=[
                pl.BlockSpec(memory_space=pltpu.SMEM),
                pl.BlockSpec((tq, qw), lambda gg, i, tile0=tile0: (tile0 + i, gg)),
                pl.BlockSpec((1, 1, nc, HEAD_DIM), lambda gg, i: (0, gg, 0, 0)),
                pl.BlockSpec((1, 1, nc, HEAD_DIM), lambda gg, i: (1, gg, 0, 0)),
                pl.BlockSpec((nc, V7X_LANES), lambda gg, i: (0, 0)),
            ],
            out_specs=[
                pl.BlockSpec((tq, qw), lambda gg, i: (i, gg)),
                pl.BlockSpec((1, tq, V7X_LANES), lambda gg, i: (gg, i, 0)),
            ],
            out_shape=[
                jax.ShapeDtypeStruct((seg, NSA_Q_W), BF16),
                jax.ShapeDtypeStruct((g, seg, V7X_LANES), F32),
            ],
            compiler_params=_cparams(("parallel", "parallel"), VMEM_MID),
        )(slopes, proj, kvc, kvc, overlap)
        outs.append(o_q)
        sels.append(sel_q)
    return jnp.concatenate(outs, axis=0), jnp.concatenate(sels, axis=1)


def _flash_scratch(n_rows):
    return [pltpu.VMEM((n_rows, V7X_LANES), F32), pltpu.VMEM((n_rows, 2 * HEAD_DIM), F32)]


def _flash_init(m_sc, acc_sc):
    m_sc[...] = jnp.full(m_sc.shape, NEG, F32)
    acc_sc[...] = jnp.zeros(acc_sc.shape, F32)


def _with_ones(v):
    return jnp.concatenate([v, jnp.ones(v.shape, v.dtype)], axis=1)


def _tile_dist(q0, k0, rows, cols):
    return (q0 + lax.broadcasted_iota(jnp.int32, (rows, 1), 0)) - (k0 + lax.broadcasted_iota(jnp.int32, (1, cols), 1))


def _flash_update(rows, s, vx, m_sc, acc_sc):
    reps = s.shape[1] // V7X_LANES
    m_prev = m_sc[rows, :]
    m_new = jnp.maximum(m_prev, jnp.max(s, axis=-1, keepdims=True))
    alpha = jnp.exp2(m_prev - m_new)
    p = jnp.exp2(s - jnp.concatenate([m_new] * reps, axis=1))
    acc_sc[rows, :] = jnp.concatenate([alpha, alpha], axis=1) * acc_sc[rows, :] + _dot(p.astype(BF16), vx)
    m_sc[rows, :] = m_new


def _flash_out(rows, acc_sc):
    return acc_sc[rows, :HEAD_DIM] / acc_sc[rows, HEAD_DIM:]


def _flash_store(o_ref, acc_sc, n_rep, tq):
    for r in range(n_rep):
        o = _flash_out(slice(r * tq, (r + 1) * tq), acc_sc)
        o_ref[:, r * HEAD_DIM:(r + 1) * HEAD_DIM] = o.astype(o_ref.dtype)


def _selattn_kernel(pi_ref, pj_ref, slopes_ref, q_ref, k_ref, v_ref, sel_ref, e_ref, o_ref, m_sc, acc_sc):
    g = pl.program_id(0)
    step = pl.program_id(1)
    i = pi_ref[step]
    j = pj_ref[step]
    tq = q_ref.shape[0]
    tk = k_ref.shape[0]

    @pl.when(j == 0)
    def _():
        _flash_init(m_sc, acc_sc)

    k = k_ref[...]
    vx = _with_ones(v_ref[...])
    madd = (_dot(sel_ref[0].astype(BF16), e_ref[0]) - 1.0) * (-NEG)
    dist = _tile_dist(i * tq, j * tk, tq, tk)
    madd = jnp.where(dist >= 0, madd, NEG)
    distf = dist.astype(F32)
    for r in range(NSA_REP):
        slope = slopes_ref[g * NSA_REP + r]
        s = (_dot_nt(q_ref[:, r * HEAD_DIM:(r + 1) * HEAD_DIM], k) - slope * distf) + madd
        _flash_update(slice(r * tq, (r + 1) * tq), s, vx, m_sc, acc_sc)

    @pl.when(j == i)
    def _():
        _flash_store(o_ref, acc_sc, NSA_REP, tq)


def _sel_attention(proj, sel, expand, slopes):
    t = proj.shape[0]
    g = NSA_KV_GROUPS
    tq = min(512, t)
    nq = t // tq
    qw = NSA_REP * HEAD_DIM
    pairs = [(i, j) for i in range(nq) for j in range(i + 1)]
    pi = jnp.asarray([p[0] for p in pairs], jnp.int32)
    pj = jnp.asarray([p[1] for p in pairs], jnp.int32)
    return pl.pallas_call(
        _selattn_kernel,
        grid_spec=pltpu.PrefetchScalarGridSpec(
            num_scalar_prefetch=2,
            grid=(g, len(pairs)),
            in_specs=[
                pl.BlockSpec(memory_space=pltpu.SMEM),
                pl.BlockSpec((tq, qw), lambda gg, s, pi, pj: (pi[s], gg)),
                pl.BlockSpec((tq, HEAD_DIM), lambda gg, s, pi, pj: (pj[s], KS_BLK + gg)),
                pl.BlockSpec((tq, HEAD_DIM), lambda gg, s, pi, pj: (pj[s], VS_BLK + gg)),
                pl.BlockSpec((1, tq, V7X_LANES), lambda gg, s, pi, pj: (gg, pi[s], 0)),
                pl.BlockSpec((1, V7X_LANES, tq), lambda gg, s, pi, pj: (pj[s], 0, 0)),
            ],
            out_specs=pl.BlockSpec((tq, qw), lambda gg, s, pi, pj: (pi[s], gg)),
            scratch_shapes=_flash_scratch(NSA_REP * tq),
        ),
        out_shape=jax.ShapeDtypeStruct((t, NSA_Q_W), BF16),
        compiler_params=_cparams(("parallel", "arbitrary"), VMEM_MID),
    )(pi, pj, slopes, proj, proj, proj, sel, expand)


def _win_kernel(slopes_ref, q_ref, k_ref, v_ref, o_ref, m_sc, acc_sc):
    g = pl.program_id(0)
    i = pl.program_id(1)
    j = pl.program_id(2)
    tq = q_ref.shape[0]
    kt = i - 1 + j

    @pl.when(j == 0)
    def _():
        _flash_init(m_sc, acc_sc)

    @pl.when(kt >= 0)
    def _():
        k = k_ref[...]
        vx = _with_ones(v_ref[...])
        dist = _tile_dist(i * tq, kt * tq, tq, tq)
        madd = jnp.where((dist >= 0) & (dist <= WIN - 1), 0.0, NEG)
        distf = dist.astype(F32)
        for r in range(NSA_REP):
            slope = slopes_ref[g * NSA_REP + r]
            s = (_dot_nt(q_ref[:, r * HEAD_DIM:(r + 1) * HEAD_DIM], k) - slope * distf) + madd
            _flash_update(slice(r * tq, (r + 1) * tq), s, vx, m_sc, acc_sc)

    @pl.when(j == 1)
    def _():
        _flash_store(o_ref, acc_sc, NSA_REP, tq)


def _win_attention(proj, slopes):
    t = proj.shape[0]
    tq = min(512, t)
    assert WIN - 1 <= tq
    qw = NSA_REP * HEAD_DIM
    kv_tile = lambda i, j: jnp.maximum(i - 1 + j, 0)
    return pl.pallas_call(
        _win_kernel,
        grid=(NSA_KV_GROUPS, t // tq, 2),
        in_specs=[
            pl.BlockSpec(memory_space=pltpu.SMEM),
            pl.BlockSpec((tq, qw), lambda g, i, j: (i, g)),
            pl.BlockSpec((tq, HEAD_DIM), lambda g, i, j: (kv_tile(i, j), KW_BLK + g)),
            pl.BlockSpec((tq, HEAD_DIM), lambda g, i, j: (kv_tile(i, j), VW_BLK + g)),
        ],
        out_specs=pl.BlockSpec((tq, qw), lambda g, i, j: (i, g)),
        out_shape=jax.ShapeDtypeStruct((t, NSA_Q_W), BF16),
        scratch_shapes=_flash_scratch(NSA_REP * tq),
        compiler_params=_cparams(("parallel", "parallel", "arbitrary"), VMEM_MID),
    )(slopes, proj, proj, proj)


def _dil_tiles(tq):
    return tuple(-(-window // tq) + 1 for window, _ in DIL_CONFIGS)


def _dil_kernel(slopes_ref, *refs):
    n_cfg = len(DIL_CONFIGS)
    q_refs = refs[0:n_cfg]
    k_refs = refs[n_cfg:2 * n_cfg]
    v_refs = refs[2 * n_cfg:3 * n_cfg]
    o_refs = refs[3 * n_cfg:4 * n_cfg]
    m_sc, acc_sc = refs[4 * n_cfg:]
    h = pl.program_id(0)
    i = pl.program_id(1)
    j = pl.program_id(2)
    tq = q_refs[0].shape[0]
    tiles = _dil_tiles(tq)
    n_steps = max(tiles)
    kt = i - (n_steps - 1) + j

    @pl.when(j == 0)
    def _():
        _flash_init(m_sc, acc_sc)

    for c, (window, dil) in enumerate(DIL_CONFIGS):
        @pl.when((j >= n_steps - tiles[c]) & (kt >= 0))
        def _(c=c, window=window, dil=dil):
            dist = _tile_dist(i * tq, kt * tq, tq, tq)
            ok = (dist >= 0) & (dist <= window) & ((dist & (dil - 1)) == 0)
            slope = slopes_ref[NSA_HEADS + c * DIL_HEADS_PER_CFG + h]
            s = _dot_nt(q_refs[c][...], k_refs[c][...]) - slope * dist.astype(F32)
            s = jnp.where(ok, s, NEG)
            _flash_update(slice(c * tq, (c + 1) * tq), s, _with_ones(v_refs[c][...]), m_sc, acc_sc)

    @pl.when(j == n_steps - 1)
    def _():
        rows = [slice(c * tq, (c + 1) * tq) for c in range(n_cfg)]
        lses = [m_sc[rw, :] + jnp.log2(acc_sc[rw, HEAD_DIM:]) for rw in rows]
        mx = functools.reduce(jnp.maximum, lses)
        es = [jnp.exp2(x - mx) for x in lses]
        den = functools.reduce(lambda a, b: a + b, es)
        for c in range(n_cfg):
            o_refs[c][...] = (_flash_out(rows[c], acc_sc) * (es[c] / den)).astype(o_refs[c].dtype)


def _dil_attention(proj, slopes):
    t = proj.shape[0]
    tq = min(512, t)
    n_cfg = len(DIL_CONFIGS)
    for _, dil in DIL_CONFIGS:
        assert dil & (dil - 1) == 0
    tiles = _dil_tiles(tq)
    n_steps = max(tiles)

    def kv_spec(col0, c):
        def idx(h, i, j):
            step = jnp.maximum(j - (n_steps - tiles[c]), 0)
            return (jnp.maximum(i - (tiles[c] - 1) + step, 0), col0 + c * DIL_HEADS_PER_CFG + h)
        return pl.BlockSpec((tq, HEAD_DIM), idx)

    q_specs = [pl.BlockSpec((tq, HEAD_DIM), lambda h, i, j, c=c: (i, QD_BLK + c * DIL_HEADS_PER_CFG + h))
               for c in range(n_cfg)]
    return pl.pallas_call(
        _dil_kernel,
        grid=(DIL_HEADS_PER_CFG, t // tq, n_steps),
        in_specs=[pl.BlockSpec(memory_space=pltpu.SMEM)] + q_specs
        + [kv_spec(KD_BLK, c) for c in range(n_cfg)] + [kv_spec(VD_BLK, c) for c in range(n_cfg)],
        out_specs=[pl.BlockSpec((tq, HEAD_DIM), lambda h, i, j: (i, h)) for _ in range(n_cfg)],
        out_shape=[jax.ShapeDtypeStruct((t, DIL_CFG_W), BF16) for _ in range(n_cfg)],
        scratch_shapes=_flash_scratch(n_cfg * tq),
        compiler_params=_cparams(("parallel", "parallel", "arbitrary"), VMEM_MID),
    )(slopes, *([proj] * (3 * n_cfg)))


def _combine_kernel(gates_ref, ocmp_ref, osel_ref, owin_ref, o_ref):
    sg = 1.0 / (1.0 + jnp.exp(-gates_ref[...]))
    for h in range(NSA_HEADS):
        cs = slice(h * HEAD_DIM, (h + 1) * HEAD_DIM)
        o = (sg[:, 3 * h:3 * h + 1] * ocmp_ref[:, cs].astype(F32)
             + sg[:, 3 * h + 1:3 * h + 2] * osel_ref[:, cs].astype(F32)
             + sg[:, 3 * h + 2:3 * h + 3] * owin_ref[:, cs].astype(F32))
        o_ref[:, cs] = o.astype(o_ref.dtype)


def _combine(gates, o_cmp, o_sel, o_win):
    t = gates.shape[0]
    tr = min(256, t)
    row = lambda w: pl.BlockSpec((tr, w), lambda i: (i, 0))
    return pl.pallas_call(
        _combine_kernel,
        grid=(t // tr,),
        in_specs=[row(V7X_LANES), row(NSA_Q_W), row(NSA_Q_W), row(NSA_Q_W)],
        out_specs=row(NSA_Q_W),
        out_shape=jax.ShapeDtypeStruct((t, NSA_Q_W), BF16),
        compiler_params=_cparams(("parallel",), VMEM_MID),
    )(gates, o_cmp, o_sel, o_win)


def _cross_kernel(q_ref, kv_ref, wo_ref, x_ref, o_ref):
    cw = CROSS_HEADS * HEAD_DIM
    outs = []
    for h in range(CROSS_HEADS):
        q = q_ref[:, h * HEAD_DIM:(h + 1) * HEAD_DIM]
        k = kv_ref[:, h * HEAD_DIM:(h + 1) * HEAD_DIM]
        v = kv_ref[:, cw + h * HEAD_DIM:cw + (h + 1) * HEAD_DIM]
        s = _dot_nt(q, k) * SCALE
        m = jnp.max(s, axis=-1, keepdims=True)
        p = jnp.exp(s - m)
        l = jnp.sum(p, axis=-1, keepdims=True)
        outs.append((_dot(p.astype(BF16), v) / l).astype(BF16))
    o = jnp.concatenate(outs, axis=-1)
    o_ref[...] = x_ref[...] + _dot(o, wo_ref[...])


def _cross_attention(q, kv, wo, x):
    t, d = x.shape
    s_len = kv.shape[0]
    cw = CROSS_HEADS * HEAD_DIM
    tq = min(256, t)
    return pl.pallas_call(
        _cross_kernel,
        grid=(t // tq,),
        in_specs=[
            pl.BlockSpec((tq, cw), lambda i: (i, 0)),
            pl.BlockSpec((s_len, 2 * cw), lambda i: (0, 0)),
            pl.BlockSpec((cw, d), lambda i: (0, 0)),
            pl.BlockSpec((tq, d), lambda i: (i, 0)),
        ],
        out_specs=pl.BlockSpec((tq, d), lambda i: (i, 0)),
        out_shape=jax.ShapeDtypeStruct((t, d), F32),
        compiler_params=_cparams(("parallel",), VMEM_MID),
    )(q, kv, wo, x)


def _router_kernel(x_ref, g_ref, whi_ref, wmid_ref, wlo_ref, b_ref, h_ref, eid_ref, gate_ref):
    x = x_ref[...]
    tr, d = x.shape
    ms = jnp.mean(x * x, axis=-1, keepdims=True)
    h = x * lax.rsqrt(ms + EPS) * g_ref[...]
    h_ref[...] = jnp.zeros(h_ref.shape, h_ref.dtype)
    for s in range(SLAB_ROWS):
        h_ref[pl.ds(s, tr, stride=SLAB_PITCH), :] = h[:, s * V7X_LANES:(s + 1) * V7X_LANES]
    hh, hm, hl = _split3(h)
    whi, wmid, wlo = whi_ref[...], wmid_ref[...], wlo_ref[...]
    logits = (_dot(hh, whi) + (_dot(hh, wmid) + _dot(hm, whi))
              + (_dot(hh, wlo) + _dot(hm, wmid) + _dot(hl, whi))) + b_ref[...]
    lane = lax.broadcasted_iota(jnp.int32, (1, V7X_LANES), 1)
    ninf = -jnp.inf
    is_g = lane < N_GROUPS
    lg = jnp.where(is_g, logits, ninf)
    mg = jnp.max(lg, axis=-1, keepdims=True)
    grp = jnp.min(jnp.where(lg == mg, lane, V7X_LANES), axis=-1, keepdims=True)
    p_grp = 1.0 / jnp.sum(jnp.where(is_g, jnp.exp(lg - mg), 0.0), axis=-1, keepdims=True)
    lo_lane = N_GROUPS + grp * EXPERTS_PER_GROUP
    in_grp = (lane >= lo_lane) & (lane < lo_lane + EXPERTS_PER_GROUP)
    le = jnp.where(in_grp, logits, ninf)
    v1 = jnp.max(le, axis=-1, keepdims=True)
    i1 = jnp.min(jnp.where(le == v1, lane, V7X_LANES), axis=-1, keepdims=True)
    le2 = jnp.where(lane == i1, ninf, le)
    v2 = jnp.max(le2, axis=-1, keepdims=True)
    i2 = jnp.min(jnp.where(le2 == v2, lane, V7X_LANES), axis=-1, keepdims=True)
    e2 = jnp.exp(v2 - v1)
    den = 1.0 + e2
    eid_ref[...] = jnp.where(lane == 0, i1 - N_GROUPS, jnp.where(lane == 1, i2 - N_GROUPS, 0))
    gate_ref[...] = jnp.where(lane == 0, p_grp / den, jnp.where(lane == 1, p_grp * e2 / den, 0.0))


def _router(x, g, w_hi, w_mid, w_lo, b):
    t, d = x.shape
    assert d == SLAB_ROWS * V7X_LANES
    tr = min(256, t)
    full = lambda shp: pl.BlockSpec(shp, lambda i: (0, 0))
    return pl.pallas_call(
        _router_kernel,
        grid=(t // tr,),
        in_specs=[pl.BlockSpec((tr, d), lambda i: (i, 0)), full((1, d)),
                  full((d, V7X_LANES)), full((d, V7X_LANES)), full((d, V7X_LANES)), full((1, V7X_LANES))],
        out_specs=[pl.BlockSpec((tr * SLAB_PITCH, V7X_LANES), lambda i: (i, 0)),
                   pl.BlockSpec((tr, V7X_LANES), lambda i: (i, 0)),
                   pl.BlockSpec((tr, V7X_LANES), lambda i: (i, 0))],
        out_shape=[jax.ShapeDtypeStruct((t * SLAB_PITCH, V7X_LANES), F32),
                   jax.ShapeDtypeStruct((t, V7X_LANES), jnp.int32),
                   jax.ShapeDtypeStruct((t, V7X_LANES), F32)],
        compiler_params=_cparams(("parallel",), VMEM_MID),
    )(x, g.reshape(1, d), w_hi, w_mid, w_lo, b)


def _slab_copy(src_ref, dst_ref, sem, src_slab, dst_slab, src_pitch, dst_pitch, rows):
    src0 = pl.multiple_of(src_slab * src_pitch, V7X_SUBLANES)
    dst0 = pl.multiple_of(dst_slab * dst_pitch, V7X_SUBLANES)
    return pltpu.make_async_copy(src_ref.at[pl.ds(src0, rows)], dst_ref.at[pl.ds(dst0, rows)], sem)


def _unpack_rows(xg, r0, rows):
    parts = [xg[pl.ds(r0 * SLAB_PITCH + s, rows, stride=SLAB_PITCH), :].astype(BF16) for s in range(SLAB_ROWS)]
    return jnp.concatenate(parts, axis=1)


def _moe_kernel(ue_ref, un_ref, tab_ref, hp_ref, wg_ref, wu_ref, wd_ref, rw_ref, yt_ref,
                xg, acc, yst, gsem, ssem, cnt):
    u = pl.program_id(0)
    f = pl.program_id(1)
    n_units = pl.num_programs(0)
    n_f = pl.num_programs(1)
    n = un_ref[u]
    unit_rows = acc.shape[0]

    def x_copy(k, tok):
        return _slab_copy(hp_ref, xg, gsem, tok, k, SLAB_PITCH, SLAB_PITCH, SLAB_ROWS)

    def y_copy(k, slot):
        return _slab_copy(yst, yt_ref, ssem, k, slot, SLAB_PITCH, SLAB_PITCH, SLAB_PITCH)

    def drain_scatter():
        def body(k, c):
            y_copy(0, 0).wait()
            return c
        lax.fori_loop(0, cnt[0], body, 0)
        cnt[0] = 0

    @pl.when((u == 0) & (f == 0))
    def _():
        xg[...] = jnp.zeros(xg.shape, xg.dtype)
        yst[...] = jnp.zeros(yst.shape, yst.dtype)
        cnt[0] = 0

    @pl.when((f == 0) & (n > 0))
    def _():
        def issue(k, c):
            x_copy(k, tab_ref[0, 0, k]).start()
            return c
        lax.fori_loop(0, n, issue, 0)

        def drain(k, c):
            x_copy(0, 0).wait()
            return c
        lax.fori_loop(0, n, drain, 0)
        acc[...] = jnp.zeros(acc.shape, acc.dtype)

    @pl.when(n > 0)
    def _():
        wg = wg_ref[0].astype(BF16)
        wu = wu_ref[0].astype(BF16)
        wd = wd_ref[0].astype(BF16)
        d = acc.shape[1]
        col_chunk = min(1024, d)

        def sub_block(r0, rows):
            x = _unpack_rows(xg, r0, rows)
            gt = _dot(x, wg)
            up = _dot(x, wu)
            a = (gt * (1.0 / (1.0 + jnp.exp(-gt))) * up).astype(BF16)
            for c0 in range(0, d, col_chunk):
                acc[r0:r0 + rows, c0:c0 + col_chunk] += _dot(a, wd[:, c0:c0 + col_chunk])

        sub_block(0, MOE_SUB_ROWS)

        @pl.when(n > MOE_SUB_ROWS)
        def _():
            sub_block(MOE_SUB_ROWS, unit_rows - MOE_SUB_ROWS)

    @pl.when((f == n_f - 1) & (n > 0))
    def _():
        drain_scatter()
        rw = rw_ref[0]
        for s in range(SLAB_ROWS):
            yst[pl.ds(s, unit_rows, stride=SLAB_PITCH), :] = acc[:, s * V7X_LANES:(s + 1) * V7X_LANES] * rw

        def scatter(k, c):
            y_copy(k, tab_ref[0, 0, unit_rows + k]).start()
            return c
        lax.fori_loop(0, n, scatter, 0)
        cnt[0] = n

    @pl.when((u == n_units - 1) & (f == n_f - 1))
    def _():
        drain_scatter()


def _moe_experts(hp, tab, row_w, unit_e, unit_n, w_gate, w_up, w_down, t):
    n_units = unit_e.shape[0]
    r = MOE_UNIT_ROWS
    _, d, d_exp = w_gate.shape
    assert d == SLAB_ROWS * V7X_LANES
    tf = min(256, d_exp)
    nf = d_exp // tf

    def wchunk(u, f, ue, un):
        return jnp.where(un[u] > 0, f, nf - 1)

    return pl.pallas_call(
        _moe_kernel,
        grid_spec=pltpu.PrefetchScalarGridSpec(
            num_scalar_prefetch=2,
            grid=(n_units, nf),
            in_specs=[
                pl.BlockSpec((1, 1, 2 * r), lambda u, f, ue, un: (u, 0, 0), memory_space=pltpu.SMEM),
                pl.BlockSpec(memory_space=pl.ANY),
                pl.BlockSpec((1, d, tf), lambda u, f, ue, un: (ue[u], 0, wchunk(u, f, ue, un))),
                pl.BlockSpec((1, d, tf), lambda u, f, ue, un: (ue[u], 0, wchunk(u, f, ue, un))),
                pl.BlockSpec((1, tf, d), lambda u, f, ue, un: (ue[u], wchunk(u, f, ue, un), 0)),
                pl.BlockSpec((1, r, 1), lambda u, f, ue, un: (u, 0, 0)),
            ],
            out_specs=pl.BlockSpec(memory_space=pl.ANY),
            scratch_shapes=[
                pltpu.VMEM((r * SLAB_PITCH, V7X_LANES), F32),
                pltpu.VMEM((r, d), F32),
                pltpu.VMEM((r * SLAB_PITCH, V7X_LANES), F32),
                pltpu.SemaphoreType.DMA(()),
                pltpu.SemaphoreType.DMA(()),
                pltpu.SMEM((1,), jnp.int32),
            ],
        ),
        out_shape=jax.ShapeDtypeStruct((EXPERT_TOPK * t * SLAB_PITCH, V7X_LANES), F32),
        compiler_params=_cparams(("arbitrary", "arbitrary"), VMEM_BIG),
    )(unit_e, unit_n, tab, hp, w_gate, w_up, w_down, row_w)


def _final_kernel(x_ref, y0_ref, y1_ref, g_ref, o_ref):
    tq = x_ref.shape[0]
    parts = []
    for s in range(SLAB_ROWS):
        parts.append(y0_ref[pl.ds(s, tq, stride=SLAB_PITCH), :] + y1_ref[pl.ds(s, tq, stride=SLAB_PITCH), :])
    x = x_ref[...] + jnp.concatenate(parts, axis=1)
    ms = jnp.mean(x * x, axis=-1, keepdims=True)
    o_ref[...] = x * lax.rsqrt(ms + EPS) * g_ref[...]


def _final(x2, yt, g_final):
    t, d = x2.shape
    tq = min(128, t)
    nt = t // tq
    return pl.pallas_call(
        _final_kernel,
        grid=(nt,),
        in_specs=[
            pl.BlockSpec((tq, d), lambda i: (i, 0)),
            pl.BlockSpec((tq * SLAB_PITCH, V7X_LANES), lambda i: (i, 0)),
            pl.BlockSpec((tq * SLAB_PITCH, V7X_LANES), lambda i: (nt + i, 0)),
            pl.BlockSpec((1, d), lambda i: (0, 0)),
        ],
        out_specs=pl.BlockSpec((tq, d), lambda i: (i, 0)),
        out_shape=jax.ShapeDtypeStruct((t, d), F32),
        compiler_params=_cparams(("parallel",), VMEM_MID),
    )(x2, yt, yt, g_final.reshape(1, d))


def _alibi_slopes(n):
    return jnp.exp2(-8.0 * jnp.arange(1, n + 1, dtype=F32) / n)


def _hybrid_mixer(x, norm_g, w_in, w_out, pe_k, w1_k, w2_k, pe_v, w1_v, w2_v):
    t, d = x.shape
    assert t % (CMP_STRIDE * 8) == 0 and t // SEL_BLOCK <= V7X_LANES
    h = _rmsnorm(x, norm_g)

    g0 = NSA_Q_W + 6 * NSA_KV_W
    col_scale = np.ones((PROJ_BLKS * HEAD_DIM,), np.float32)
    col_scale[QN_BLK * HEAD_DIM:KC_BLK * HEAD_DIM] = SCALE * LOG2E
    col_scale[QD_BLK * HEAD_DIM:KD_BLK * HEAD_DIM] = SCALE * LOG2E
    w_main = jnp.concatenate([w_in[:, :g0], w_in[:, g0 + NSA_GATE_W:]], axis=1)
    w_main = (w_main * col_scale).astype(BF16)
    w_gates = jnp.pad(w_in[:, g0:g0 + NSA_GATE_W], ((0, 0), (0, V7X_LANES - NSA_GATE_W))).astype(BF16)
    proj = _matmul(h, w_main, BF16)
    gates = _matmul(h, w_gates, F32)
    slopes = _alibi_slopes(N_MIX_HEADS) * LOG2E

    nc = t // CMP_STRIDE
    kv_raw = proj[:, KC_BLK * HEAD_DIM:KS_BLK * HEAD_DIM]
    a2 = kv_raw.reshape(nc, CMP_STRIDE, 2, NSA_KV_GROUPS, HEAD_DIM).transpose(2, 3, 0, 1, 4)
    a2 = a2.reshape(2, NSA_KV_GROUPS, nc, CMP_STRIDE * HEAD_DIM)
    w1 = jnp.stack([w1_k, w1_v]).astype(BF16)
    w2 = jnp.stack([w2_k, w2_v]).astype(BF16)
    pe = jnp.stack([pe_k, pe_v]).reshape(2, 1, CMP_LEN * HEAD_DIM)
    pe = jnp.broadcast_to(pe, (2, 8, CMP_LEN * HEAD_DIM)).astype(BF16)
    kvc = _compress(a2, w1, pe, w2)

    n_blk = t // SEL_BLOCK
    n_sel = min(SEL_TOPK, n_blk)
    ci = CMP_STRIDE * np.arange(nc)[:, None]
    sj = SEL_BLOCK * np.arange(V7X_LANES)[None, :]
    overlap = jnp.asarray(((ci < sj + SEL_BLOCK) & (ci + CMP_LEN > sj)).astype(np.float32), dtype=BF16)
    o_cmp, sel = _cmp_attention(proj, kvc, slopes, overlap, n_sel)

    tq = min(512, t)
    kblk = (np.arange(t) // SEL_BLOCK).reshape(t // tq, 1, tq)
    expand = jnp.asarray((kblk == np.arange(V7X_LANES)[None, :, None]).astype(np.float32), dtype=BF16)
    o_sel = _sel_attention(proj, sel, expand, slopes)

    o_win = _win_attention(proj, slopes)
    o_dil = _dil_attention(proj, slopes)
    o_nsa = _combine(gates, o_cmp, o_sel, o_win)
    return _matmul([o_nsa] + list(o_dil), w_out.astype(BF16), F32, residual=x)


def _cross_block(x, mem, norm_cross, norm_mem, wq, wkv, wo):
    h = _rmsnorm(x, norm_cross)
    m = _rmsnorm(mem, norm_mem)
    q = _matmul(h, wq.astype(BF16), BF16)
    kv = _matmul(m, wkv.astype(BF16), BF16)
    return _cross_attention(q, kv, wo.astype(BF16), x)


def _moe_tables(eid, gate, t):
    r = MOE_UNIT_ROWS
    n_assign = t * EXPERT_TOPK
    e_flat = eid[:, :EXPERT_TOPK].reshape(n_assign)
    w_flat = gate[:, :EXPERT_TOPK].reshape(n_assign)
    order = jnp.argsort(e_flat).astype(jnp.int32)
    e_s = e_flat[order]
    counts = jnp.zeros((N_EXPERTS,), jnp.int32).at[e_flat].add(1)
    starts = jnp.cumsum(counts) - counts
    pcounts = (counts + r - 1) // r * r
    pends = jnp.cumsum(pcounts)
    pstarts = pends - pcounts
    row = (pstarts[e_s] + (jnp.arange(n_assign, dtype=jnp.int32) - starts[e_s])).astype(jnp.int32)
    n_units = n_assign // r + N_EXPERTS
    n_rows = n_units * r
    tok_s = order // EXPERT_TOPK
    slot_s = (order % EXPERT_TOPK) * t + tok_s
    row_tok = jnp.zeros((n_rows,), jnp.int32).at[row].set(tok_s)
    row_slot = jnp.zeros((n_rows,), jnp.int32).at[row].set(slot_s)
    row_w = jnp.zeros((n_rows,), F32).at[row].set(w_flat[order])
    unit_start = jnp.arange(n_units, dtype=jnp.int32) * r
    active = unit_start < pends[-1]
    e_raw = jnp.minimum(jnp.sum(pends[None, :] <= unit_start[:, None], axis=1), N_EXPERTS - 1).astype(jnp.int32)
    e_last = e_raw[pends[-1] // r - 1]
    unit_e = jnp.where(active, e_raw, e_last)
    unit_n = jnp.where(active, jnp.clip(counts[unit_e] - (unit_start - pstarts[unit_e]), 0, r), 0).astype(jnp.int32)
    tab = jnp.concatenate([row_tok.reshape(n_units, 1, r), row_slot.reshape(n_units, 1, r)], axis=2)
    return tab, row_w.reshape(n_units, r, 1), unit_e, unit_n


def _moe_block(x, norm_ffn, w_rg, b_rg, w_re, b_re, w_gate, w_up, w_down, norm_final):
    t, d = x.shape
    pad = V7X_LANES - N_GROUPS - N_EXPERTS
    w_r = jnp.pad(jnp.concatenate([w_rg, w_re], axis=1), ((0, 0), (0, pad)))
    b_r = jnp.pad(jnp.concatenate([b_rg, b_re]), (0, pad)).reshape(1, V7X_LANES)
    w_hi, w_mid, w_lo = _split3(w_r)
    hp, eid, gate = _router(x, norm_ffn, w_hi, w_mid, w_lo, b_r)
    tab, row_w, unit_e, unit_n = _moe_tables(eid, gate, t)
    yt = _moe_experts(hp, tab, row_w, unit_e, unit_n, w_gate, w_up, w_down, t)
    return _final(x, yt, norm_final)


def kernel(x, mem, norm_mix, w_in, w_out, cmp_pe_k, cmp_w1_k, cmp_w2_k, cmp_pe_v, cmp_w1_v, cmp_w2_v, norm_cross, norm_mem, w_q_cross, w_kv_cross, w_o_cross, norm_ffn, w_router_group, b_router_group, w_router_expert, b_router_expert, w_gate, w_up, w_down, norm_final):
    b, t, d = x.shape
    depth = norm_mix.shape[0]
    assert b == 1 and depth == 1
    xs = x.reshape(t, d)
    l = 0
    xs = _hybrid_mixer(xs, norm_mix[l], w_in[l], w_out[l], cmp_pe_k[l], cmp_w1_k[l], cmp_w2_k[l],
                       cmp_pe_v[l], cmp_w1_v[l], cmp_w2_v[l])
    xs = _cross_block(xs, mem.reshape(mem.shape[1], d), norm_cross[l], norm_mem[l], w_q_cross[l],
                      w_kv_cross[l], w_o_cross[l])
    out = _moe_block(xs, norm_ffn[l], w_router_group[l], b_router_group[l], w_router_expert[l],
                     b_router_expert[l], w_gate[l], w_up[l], w_down[l], norm_final)
    return out.reshape(b, t, d)
```

```python
import functools

import numpy as np
import jax
import jax.numpy as jnp
from jax import lax
from jax.experimental import pallas as pl
from jax.experimental.pallas import tpu as pltpu

F32 = jnp.float32
BF16 = jnp.bfloat16

HEAD_DIM = 128
DIL_CONFIGS = ((128, 1), (512, 4), (2048, 16))
DIL_HEADS_PER_CFG = 4
DIL_HEADS = DIL_HEADS_PER_CFG * len(DIL_CONFIGS)
N_MIX_HEADS = 32
NSA_HEADS = N_MIX_HEADS - DIL_HEADS
NSA_KV_GROUPS = 4
NSA_REP = NSA_HEADS // NSA_KV_GROUPS
CMP_LEN = 32
CMP_STRIDE = 16
SEL_BLOCK = 64
SEL_TOPK = 16
WIN = 512
CROSS_HEADS = 4
N_GROUPS = 8
EXPERTS_PER_GROUP = 8
N_EXPERTS = N_GROUPS * EXPERTS_PER_GROUP
EXPERT_TOPK = 2
EPS = 1e-6
NEG = -1e30
FORCE = 1e4
SCALE = HEAD_DIM ** -0.5
LOG2E = float(np.log2(np.e))

V7X_LANES = 128
V7X_SUBLANES = 8
V7X_VMEM_BYTES = 64 * 1024 * 1024
VMEM_BIG = 56 * 1024 * 1024
VMEM_MID = 40 * 1024 * 1024

QN_BLK = 0
KC_BLK = NSA_HEADS
VC_BLK = KC_BLK + NSA_KV_GROUPS
KS_BLK = VC_BLK + NSA_KV_GROUPS
VS_BLK = KS_BLK + NSA_KV_GROUPS
KW_BLK = VS_BLK + NSA_KV_GROUPS
VW_BLK = KW_BLK + NSA_KV_GROUPS
NSA_BLKS = VW_BLK + NSA_KV_GROUPS
QD_BLK = 0
KD_BLK = QD_BLK + DIL_HEADS
VD_BLK = KD_BLK + DIL_HEADS
NSA_Q_W = NSA_HEADS * HEAD_DIM
NSA_KV_W = NSA_KV_GROUPS * HEAD_DIM
NSA_GATE_W = NSA_HEADS * 3
DIL_CFG_W = DIL_HEADS_PER_CFG * HEAD_DIM

SLAB_ROWS = 32
SLAB_PITCH = 40
MOE_UNIT_ROWS = 320
MOE_SUB_ROWS = 256


def _cparams(sem, vmem=None):
    return pltpu.CompilerParams(dimension_semantics=sem, vmem_limit_bytes=vmem)


def _dot(a, b):
    return jnp.dot(a, b, preferred_element_type=F32)


def _dot_nt(a, b):
    return lax.dot_general(a, b, (((1,), (1,)), ((), ())), preferred_element_type=F32)


def _split3(a):
    hi = a.astype(BF16)
    r1 = a - hi.astype(F32)
    mid = r1.astype(BF16)
    lo = (r1 - mid.astype(F32)).astype(BF16)
    return hi, mid, lo


def _rms_kernel(x_ref, g_ref, o_ref):
    x = x_ref[...]
    ms = jnp.mean(x * x, axis=-1, keepdims=True)
    o_ref[...] = (x * lax.rsqrt(ms + EPS) * g_ref[...]).astype(o_ref.dtype)


def _rmsnorm(x, g, out_dtype=BF16):
    n, d = x.shape
    tr = min(256, n)
    return pl.pallas_call(
        _rms_kernel,
        grid=(n // tr,),
        in_specs=[pl.BlockSpec((tr, d), lambda i: (i, 0)), pl.BlockSpec((1, d), lambda i: (0, 0))],
        out_specs=pl.BlockSpec((tr, d), lambda i: (i, 0)),
        out_shape=jax.ShapeDtypeStruct((n, d), out_dtype),
        compiler_params=_cparams(("parallel",)),
    )(x, g.reshape(1, d))


def _mm_kernel(*refs, widths, has_res):
    n_x = len(widths)
    x_refs = refs[:n_x]
    w_ref = refs[n_x]
    o_ref = refs[-1]
    acc = None
    off = 0
    for x_ref, kw in zip(x_refs, widths):
        part = _dot(x_ref[...], w_ref[off:off + kw, :])
        acc = part if acc is None else acc + part
        off += kw
    if has_res:
        acc = refs[n_x + 1][...] + acc
    o_ref[...] = acc.astype(o_ref.dtype)


def _matmul(xs, w, out_dtype, residual=None):
    if not isinstance(xs, (list, tuple)):
        xs = [xs]
    m = xs[0].shape[0]
    widths = tuple(x.shape[1] for x in xs)
    k, n = w.shape
    assert sum(widths) == k
    tm = min(1024, m)
    tn = min(512, n)
    in_specs = [pl.BlockSpec((tm, kw), lambda i, j: (i, 0)) for kw in widths]
    in_specs.append(pl.BlockSpec((k, tn), lambda i, j: (0, j)))
    args = list(xs) + [w]
    if residual is not None:
        in_specs.append(pl.BlockSpec((tm, tn), lambda i, j: (i, j)))
        args.append(residual)
    return pl.pallas_call(
        functools.partial(_mm_kernel, widths=widths, has_res=residual is not None),
        grid=(m // tm, n // tn),
        in_specs=in_specs,
        out_specs=pl.BlockSpec((tm, tn), lambda i, j: (i, j)),
        out_shape=jax.ShapeDtypeStruct((m, n), out_dtype),
        compiler_params=_cparams(("parallel", "parallel"), VMEM_BIG),
    )(*args)


def _mm_wcast_kernel(x_ref, w_ref, s_ref, o_ref, wb_sc):
    @pl.when(pl.program_id(1) == 0)
    def _():
        wb_sc[...] = (w_ref[...] * s_ref[...]).astype(BF16)

    o_ref[...] = _dot(x_ref[...], wb_sc[...]).astype(o_ref.dtype)


def _matmul_wcast(x, w, col_scale, col_blk0, n_col_blks, tn, out_dtype):
    m, k = x.shape
    tm = min(1024, m)
    return pl.pallas_call(
        _mm_wcast_kernel,
        grid=(n_col_blks, m // tm),
        in_specs=[
            pl.BlockSpec((tm, k), lambda j, i: (i, 0)),
            pl.BlockSpec((k, tn), lambda j, i: (0, col_blk0 + j)),
            pl.BlockSpec((1, tn), lambda j, i: (0, j)),
        ],
        out_specs=pl.BlockSpec((tm, tn), lambda j, i: (i, j)),
        out_shape=jax.ShapeDtypeStruct((m, n_col_blks * tn), out_dtype),
        scratch_shapes=[pltpu.VMEM((k, tn), BF16)],
        compiler_params=_cparams(("parallel", "arbitrary"), VMEM_BIG),
    )(x, w, col_scale.reshape(1, n_col_blks * tn))


def _gelu_tanh(x):
    return 0.5 * x * (1.0 + jnp.tanh(np.sqrt(2.0 / np.pi) * (x + 0.044715 * (x * x * x))))


def _cmp_kernel(a_ref, w1_ref, pe_ref, w2_ref, o_ref):
    a = a_ref[0, 0]
    nc = a.shape[0]
    half = CMP_STRIDE * HEAD_DIM
    ha = _dot(a, w1_ref[0, :half, :])
    hb = _dot(a, w1_ref[0, half:, :])
    c = _dot(pe_ref[0], w1_ref[0])[0:1]
    pre = ha + pltpu.roll(hb, nc - 1, 0) + c
    hid = _gelu_tanh(pre)
    o_ref[0, 0] = _dot(hid.astype(BF16), w2_ref[0]).astype(o_ref.dtype)


def _compress(a2, w1, pe, w2):
    _, g, nc, ck = a2.shape
    hid = w1.shape[-1]
    return pl.pallas_call(
        _cmp_kernel,
        grid=(2, g),
        in_specs=[
            pl.BlockSpec((1, 1, nc, ck), lambda s, gg: (s, gg, 0, 0)),
            pl.BlockSpec((1, 2 * ck, hid), lambda s, gg: (s, 0, 0)),
            pl.BlockSpec((1, 8, 2 * ck), lambda s, gg: (s, 0, 0)),
            pl.BlockSpec((1, hid, HEAD_DIM), lambda s, gg: (s, 0, 0)),
        ],
        out_specs=pl.BlockSpec((1, 1, nc, HEAD_DIM), lambda s, gg: (s, gg, 0, 0)),
        out_shape=jax.ShapeDtypeStruct((2, g, nc, HEAD_DIM), BF16),
        compiler_params=_cparams(("parallel", "parallel"), VMEM_MID),
    )(a2, w1, pe, w2)


def _cmpattn_kernel(slopes_ref, q_ref, kc_ref, vc_ref, ov_ref, o_ref, sel_ref, *, n_sel, row0):
    g = pl.program_id(0)
    i = pl.program_id(1)
    tq = q_ref.shape[0]
    nc = kc_ref.shape[2]
    t = row0 + i * tq + lax.broadcasted_iota(jnp.int32, (tq, 1), 0)
    n = lax.broadcasted_iota(jnp.int32, (1, nc), 1)
    dist = (t - (CMP_STRIDE * n + CMP_LEN - 1)).astype(F32)
    madd = jnp.where(dist >= 0, 0.0, NEG)
    has_block = t >= CMP_LEN - 1
    kc = kc_ref[0, 0]
    vc = vc_ref[0, 0]
    psum = jnp.zeros((tq, nc), F32)
    for r in range(NSA_REP):
        slope = slopes_ref[g * NSA_REP + r]
        s = (_dot_nt(q_ref[:, r * HEAD_DIM:(r + 1) * HEAD_DIM], kc) - slope * dist) + madd
        m = jnp.max(s, axis=-1, keepdims=True)
        e = jnp.exp2(s - m)
        l = jnp.sum(e, axis=-1, keepdims=True)
        p = e * jnp.where(has_block, 1.0 / l, 0.0)
        o_ref[:, r * HEAD_DIM:(r + 1) * HEAD_DIM] = _dot(p.astype(BF16), vc).astype(o_ref.dtype)
        psum = psum + p
    ov = ov_ref[...]
    hi, mid, lo = _split3(psum)
    imp = _dot(hi, ov) + _dot(mid, ov) + _dot(lo, ov)
    blk = lax.broadcasted_iota(jnp.int32, (1, V7X_LANES), 1)
    cur = t // SEL_BLOCK
    causal = blk * SEL_BLOCK <= t
    forced = (blk == 0) | (blk == cur) | (blk == cur - 1)
    work = jnp.where(causal, jnp.where(forced, FORCE, imp), NEG)
    sel = jnp.zeros((tq, V7X_LANES), F32)
    for _ in range(n_sel):
        mx = jnp.max(work, axis=-1, keepdims=True)
        idx = jnp.min(jnp.where(work == mx, blk, V7X_LANES), axis=-1, keepdims=True)
        pick = blk == idx
        sel = jnp.where(pick, 1.0, sel)
        work = jnp.where(pick, -jnp.inf, work)
    sel_ref[0] = jnp.where(causal, sel, 0.0)


CMP_SEGMENT = 2048


def _cmp_attention(proj, kvc, slopes, overlap, n_sel):
    t = proj.shape[0]
    g = NSA_KV_GROUPS
    tq = min(512, t)
    qw = NSA_REP * HEAD_DIM
    seg = min(CMP_SEGMENT, t)
    assert t % seg == 0 and seg % tq == 0 and seg % (CMP_STRIDE * V7X_SUBLANES) == 0
    outs, sels = [], []
    for q in range(t // seg):
        nc = (q + 1) * seg // CMP_STRIDE
        tile0 = q * seg // tq
        o_q, sel_q = pl.pallas_call(
            functools.partial(_cmpattn_kernel, n_sel=n_sel, row0=q * seg),
            grid=(g, seg // tq),
            in_specs=[
                pl.BlockSpec(memory_space=pltpu.SMEM),
                pl.BlockSpec((tq, qw), lambda gg, i, tile0=tile0: (tile0 + i, gg)),
                pl.BlockSpec((1, 1, nc, HEAD_DIM), lambda gg, i: (0, gg, 0, 0)),
                pl.BlockSpec((1, 1, nc, HEAD_DIM), lambda gg, i: (1, gg, 0, 0)),
                pl.BlockSpec((nc, V7X_LANES), lambda gg, i: (0, 0)),
            ],
            out_specs=[
                pl.BlockSpec((tq, qw), lambda gg, i: (i, gg)),
                pl.BlockSpec((1, tq, V7X_LANES), lambda gg, i: (gg, i, 0)),
            ],
            out_shape=[
                jax.ShapeDtypeStruct((seg, NSA_Q_W), BF16),
                jax.ShapeDtypeStruct((g, seg, V7X_LANES), F32),
            ],
            compiler_params=_cparams(("parallel", "parallel"), VMEM_MID),
        )(slopes, proj, kvc, kvc, overlap)
        outs.append(o_q)
        sels.append(sel_q)
    return jnp.concatenate(outs, axis=0), jnp.concatenate(sels, axis=1)


def _flash_scratch(n_rows):
    return [pltpu.VMEM((n_rows, V7X_LANES), F32), pltpu.VMEM((n_rows, 2 * HEAD_DIM), F32)]


def _flash_init(m_sc, acc_sc):
    m_sc[...] = jnp.full(m_sc.shape, NEG, F32)
    acc_sc[...] = jnp.zeros(acc_sc.shape, F32)


def _with_ones(v):
    return jnp.concatenate([v, jnp.ones(v.shape, v.dtype)], axis=1)


def _tile_dist(q0, k0, rows, cols):
    return (q0 + lax.broadcasted_iota(jnp.int32, (rows, 1), 0)) - (k0 + lax.broadcasted_iota(jnp.int32, (1, cols), 1))


def _flash_update(rows, s, vx, m_sc, acc_sc):
    reps = s.shape[1] // V7X_LANES
    m_prev = m_sc[rows, :]
    m_new = jnp.maximum(m_prev, jnp.max(s, axis=-1, keepdims=True))
    alpha = jnp.exp2(m_prev - m_new)
    p = jnp.exp2(s - jnp.concatenate([m_new] * reps, axis=1))
    acc_sc[rows, :] = jnp.concatenate([alpha, alpha], axis=1) * acc_sc[rows, :] + _dot(p.astype(BF16), vx)
    m_sc[rows, :] = m_new


def _flash_out(rows, acc_sc):
    return acc_sc[rows, :HEAD_DIM] / acc_sc[rows, HEAD_DIM:]


def _flash_store(o_ref, acc_sc, n_rep, tq):
    for r in range(n_rep):
        o = _flash_out(slice(r * tq, (r + 1) * tq), acc_sc)
        o_ref[:, r * HEAD_DIM:(r + 1) * HEAD_DIM] = o.astype(o_ref.dtype)


def _selattn_kernel(pi_ref, pj_ref, slopes_ref, q_ref, k_ref, v_ref, sel_ref, e_ref, o_ref, m_sc, acc_sc):
    g = pl.program_id(0)
    step = pl.program_id(1)
    i = pi_ref[step]
    j = pj_ref[step]
    tq = q_ref.shape[0]
    tk = k_ref.shape[0]

    @pl.when(j == 0)
    def _():
        _flash_init(m_sc, acc_sc)

    k = k_ref[...]
    vx = _with_ones(v_ref[...])
    madd = (_dot(sel_ref[0].astype(BF16), e_ref[0]) - 1.0) * (-NEG)
    dist = _tile_dist(i * tq, j * tk, tq, tk)
    madd = jnp.where(dist >= 0, madd, NEG)
    distf = dist.astype(F32)
    for r in range(NSA_REP):
        slope = slopes_ref[g * NSA_REP + r]
        s = (_dot_nt(q_ref[:, r * HEAD_DIM:(r + 1) * HEAD_DIM], k) - slope * distf) + madd
        _flash_update(slice(r * tq, (r + 1) * tq), s, vx, m_sc, acc_sc)

    @pl.when(j == i)
    def _():
        _flash_store(o_ref, acc_sc, NSA_REP, tq)


def _sel_attention(proj, sel, expand, slopes):
    t = proj.shape[0]
    g = NSA_KV_GROUPS
    tq = min(512, t)
    nq = t // tq
    qw = NSA_REP * HEAD_DIM
    pairs = [(i, j) for i in range(nq) for j in range(i + 1)]
    pi = jnp.asarray([p[0] for p in pairs], jnp.int32)
    pj = jnp.asarray([p[1] for p in pairs], jnp.int32)
    return pl.pallas_call(
        _selattn_kernel,
        grid_spec=pltpu.PrefetchScalarGridSpec(
            num_scalar_prefetch=2,
            grid=(g, len(pairs)),
            in_specs=[
                pl.BlockSpec(memory_space=pltpu.SMEM),
                pl.BlockSpec((tq, qw), lambda gg, s, pi, pj: (pi[s], gg)),
                pl.BlockSpec((tq, HEAD_DIM), lambda gg, s, pi, pj: (pj[s], KS_BLK + gg)),
                pl.BlockSpec((tq, HEAD_DIM), lambda gg, s, pi, pj: (pj[s], VS_BLK + gg)),
                pl.BlockSpec((1, tq, V7X_LANES), lambda gg, s, pi, pj: (gg, pi[s], 0)),
                pl.BlockSpec((1, V7X_LANES, tq), lambda gg, s, pi, pj: (pj[s], 0, 0)),
            ],
            out_specs=pl.BlockSpec((tq, qw), lambda gg, s, pi, pj: (pi[s], gg)),
            scratch_shapes=_flash_scratch(NSA_REP * tq),
        ),
        out_shape=jax.ShapeDtypeStruct((t, NSA_Q_W), BF16),
        compiler_params=_cparams(("parallel", "arbitrary"), VMEM_MID),
    )(pi, pj, slopes, proj, proj, proj, sel, expand)


def _win_kernel(slopes_ref, q_ref, k_ref, v_ref, o_ref, m_sc, acc_sc):
    g = pl.program_id(0)
    i = pl.program_id(1)
    j = pl.program_id(2)
    tq = q_ref.shape[0]
    kt = i - 1 + j

    @pl.when(j == 0)
    def _():
        _flash_init(m_sc, acc_sc)

    @pl.when(kt >= 0)
    def _():
        k = k_ref[...]
        vx = _with_ones(v_ref[...])
        dist = _tile_dist(i * tq, kt * tq, tq, tq)
        madd = jnp.where((dist >= 0) & (dist <= WIN - 1), 0.0, NEG)
        distf = dist.astype(F32)
        for r in range(NSA_REP):
            slope = slopes_ref[g * NSA_REP + r]
            s = (_dot_nt(q_ref[:, r * HEAD_DIM:(r + 1) * HEAD_DIM], k) - slope * distf) + madd
            _flash_update(slice(r * tq, (r + 1) * tq), s, vx, m_sc, acc_sc)

    @pl.when(j == 1)
    def _():
        _flash_store(o_ref, acc_sc, NSA_REP, tq)


def _win_attention(proj, slopes):
    t = proj.shape[0]
    tq = min(512, t)
    assert WIN - 1 <= tq
    qw = NSA_REP * HEAD_DIM
    kv_tile = lambda i, j: jnp.maximum(i - 1 + j, 0)
    return pl.pallas_call(
        _win_kernel,
        grid=(NSA_KV_GROUPS, t // tq, 2),
        in_specs=[
            pl.BlockSpec(memory_space=pltpu.SMEM),
            pl.BlockSpec((tq, qw), lambda g, i, j: (i, g)),
            pl.BlockSpec((tq, HEAD_DIM), lambda g, i, j: (kv_tile(i, j), KW_BLK + g)),
            pl.BlockSpec((tq, HEAD_DIM), lambda g, i, j: (kv_tile(i, j), VW_BLK + g)),
        ],
        out_specs=pl.BlockSpec((tq, qw), lambda g, i, j: (i, g)),
        out_shape=jax.ShapeDtypeStruct((t, NSA_Q_W), BF16),
        scratch_shapes=_flash_scratch(NSA_REP * tq),
        compiler_params=_cparams(("parallel", "parallel", "arbitrary"), VMEM_MID),
    )(slopes, proj, proj, proj)


def _dil_tiles(tq):
    return tuple(-(-window // tq) + 1 for window, _ in DIL_CONFIGS)


def _dil_kernel(slopes_ref, *refs):
    n_cfg = len(DIL_CONFIGS)
    q_refs = refs[0:n_cfg]
    k_refs = refs[n_cfg:2 * n_cfg]
    v_refs = refs[2 * n_cfg:3 * n_cfg]
    o_refs = refs[3 * n_cfg:4 * n_cfg]
    m_sc, acc_sc = refs[4 * n_cfg:]
    h = pl.program_id(0)
    i = pl.program_id(1)
    j = pl.program_id(2)
    tq = q_refs[0].shape[0]
    tiles = _dil_tiles(tq)
    n_steps = max(tiles)
    kt = i - (n_steps - 1) + j

    @pl.when(j == 0)
    def _():
        _flash_init(m_sc, acc_sc)

    for c, (window, dil) in enumerate(DIL_CONFIGS):
        @pl.when((j >= n_steps - tiles[c]) & (kt >= 0))
        def _(c=c, window=window, dil=dil):
            dist = _tile_dist(i * tq, kt * tq, tq, tq)
            ok = (dist >= 0) & (dist <= window) & ((dist & (dil - 1)) == 0)
            slope = slopes_ref[NSA_HEADS + c * DIL_HEADS_PER_CFG + h]
            s = _dot_nt(q_refs[c][...], k_refs[c][...]) - slope * dist.astype(F32)
            s = jnp.where(ok, s, NEG)
            _flash_update(slice(c * tq, (c + 1) * tq), s, _with_ones(v_refs[c][...]), m_sc, acc_sc)

    @pl.when(j == n_steps - 1)
    def _():
        rows = [slice(c * tq, (c + 1) * tq) for c in range(n_cfg)]
        lses = [m_sc[rw, :] + jnp.log2(acc_sc[rw, HEAD_DIM:]) for rw in rows]
        mx = functools.reduce(jnp.maximum, lses)
        es = [jnp.exp2(x - mx) for x in lses]
        den = functools.reduce(lambda a, b: a + b, es)
        for c in range(n_cfg):
            o_refs[c][...] = (_flash_out(rows[c], acc_sc) * (es[c] / den)).astype(o_refs[c].dtype)


def _dil_attention(proj, slopes):
    t = proj.shape[0]
    tq = min(512, t)
    n_cfg = len(DIL_CONFIGS)
    for _, dil in DIL_CONFIGS:
        assert dil & (dil - 1) == 0
    tiles = _dil_tiles(tq)
    n_steps = max(tiles)

    def kv_spec(col0, c):
        def idx(h, i, j):
            step = jnp.maximum(j - (n_steps - tiles[c]), 0)
            return (jnp.maximum(i - (tiles[c] - 1) + step, 0), col0 + c * DIL_HEADS_PER_CFG + h)
        return pl.BlockSpec((tq, HEAD_DIM), idx)

    q_specs = [pl.BlockSpec((tq, HEAD_DIM), lambda h, i, j, c=c: (i, QD_BLK + c * DIL_HEADS_PER_CFG + h))
               for c in range(n_cfg)]
    return pl.pallas_call(
        _dil_kernel,
        grid=(DIL_HEADS_PER_CFG, t // tq, n_steps),
        in_specs=[pl.BlockSpec(memory_space=pltpu.SMEM)] + q_specs
        + [kv_spec(KD_BLK, c) for c in range(n_cfg)] + [kv_spec(VD_BLK, c) for c in range(n_cfg)],
        out_specs=[pl.BlockSpec((tq, HEAD_DIM), lambda h, i, j: (i, h)) for _ in range(n_cfg)],
        out_shape=[jax.ShapeDtypeStruct((t, DIL_CFG_W), BF16) for _ in range(n_cfg)],
        scratch_shapes=_flash_scratch(n_cfg * tq),
        compiler_params=_cparams(("parallel", "parallel", "arbitrary"), VMEM_MID),
    )(slopes, *([proj] * (3 * n_cfg)))


def _combine_kernel(gates_ref, ocmp_ref, osel_ref, owin_ref, o_ref):
    sg = 1.0 / (1.0 + jnp.exp(-gates_ref[...]))
    for h in range(NSA_HEADS):
        cs = slice(h * HEAD_DIM, (h + 1) * HEAD_DIM)
        o = (sg[:, 3 * h:3 * h + 1] * ocmp_ref[:, cs].astype(F32)
             + sg[:, 3 * h + 1:3 * h + 2] * osel_ref[:, cs].astype(F32)
             + sg[:, 3 * h + 2:3 * h + 3] * owin_ref[:, cs].astype(F32))
        o_ref[:, cs] = o.astype(o_ref.dtype)


def _combine(gates, o_cmp, o_sel, o_win):
    t = gates.shape[0]
    tr = min(256, t)
    row = lambda w: pl.BlockSpec((tr, w), lambda i: (i, 0))
    return pl.pallas_call(
        _combine_kernel,
        grid=(t // tr,),
        in_specs=[row(V7X_LANES), row(NSA_Q_W), row(NSA_Q_W), row(NSA_Q_W)],
        out_specs=row(NSA_Q_W),
        out_shape=jax.ShapeDtypeStruct((t, NSA_Q_W), BF16),
        compiler_params=_cparams(("parallel",), VMEM_MID),
    )(gates, o_cmp, o_sel, o_win)


def _cross_kernel(q_ref, kv_ref, wo_ref, x_ref, o_ref):
    cw = CROSS_HEADS * HEAD_DIM
    outs = []
    for h in range(CROSS_HEADS):
        q = q_ref[:, h * HEAD_DIM:(h + 1) * HEAD_DIM]
        k = kv_ref[:, h * HEAD_DIM:(h + 1) * HEAD_DIM]
        v = kv_ref[:, cw + h * HEAD_DIM:cw + (h + 1) * HEAD_DIM]
        s = _dot_nt(q, k) * SCALE
        m = jnp.max(s, axis=-1, keepdims=True)
        p = jnp.exp(s - m)
        l = jnp.sum(p, axis=-1, keepdims=True)
        outs.append((_dot(p.astype(BF16), v) / l).astype(BF16))
    o = jnp.concatenate(outs, axis=-1)
    o_ref[...] = x_ref[...] + _dot(o, wo_ref[...])


def _cross_attention(q, kv, wo, x):
    t, d = x.shape
    s_len = kv.shape[0]
    cw = CROSS_HEADS * HEAD_DIM
    tq = min(256, t)
    return pl.pallas_call(
        _cross_kernel,
        grid=(t // tq,),
        in_specs=[
            pl.BlockSpec((tq, cw), lambda i: (i, 0)),
            pl.BlockSpec((s_len, 2 * cw), lambda i: (0, 0)),
            pl.BlockSpec((cw, d), lambda i: (0, 0)),
            pl.BlockSpec((tq, d), lambda i: (i, 0)),
        ],
        out_specs=pl.BlockSpec((tq, d), lambda i: (i, 0)),
        out_shape=jax.ShapeDtypeStruct((t, d), F32),
        compiler_params=_cparams(("parallel",), VMEM_MID),
    )(q, kv, wo, x)


def _router_kernel(x_ref, g_ref, whi_ref, wmid_ref, wlo_ref, b_ref, h_ref, eid_ref, gate_ref):
    x = x_ref[...]
    tr, d = x.shape
    ms = jnp.mean(x * x, axis=-1, keepdims=True)
    h = x * lax.rsqrt(ms + EPS) * g_ref[...]
    h_ref[...] = jnp.zeros(h_ref.shape, h_ref.dtype)
    for s in range(SLAB_ROWS):
        h_ref[pl.ds(s, tr, stride=SLAB_PITCH), :] = h[:, s * V7X_LANES:(s + 1) * V7X_LANES]
    hh, hm, hl = _split3(h)
    whi, wmid, wlo = whi_ref[...], wmid_ref[...], wlo_ref[...]
    logits = (_dot(hh, whi) + (_dot(hh, wmid) + _dot(hm, whi))
              + (_dot(hh, wlo) + _dot(hm, wmid) + _dot(hl, whi))) + b_ref[...]
    lane = lax.broadcasted_iota(jnp.int32, (1, V7X_LANES), 1)
    ninf = -jnp.inf
    is_g = lane < N_GROUPS
    lg = jnp.where(is_g, logits, ninf)
    mg = jnp.max(lg, axis=-1, keepdims=True)
    grp = jnp.min(jnp.where(lg == mg, lane, V7X_LANES), axis=-1, keepdims=True)
    p_grp = 1.0 / jnp.sum(jnp.where(is_g, jnp.exp(lg - mg), 0.0), axis=-1, keepdims=True)
    lo_lane = N_GROUPS + grp * EXPERTS_PER_GROUP
    in_grp = (lane >= lo_lane) & (lane < lo_lane + EXPERTS_PER_GROUP)
    le = jnp.where(in_grp, logits, ninf)
    v1 = jnp.max(le, axis=-1, keepdims=True)
    i1 = jnp.min(jnp.where(le == v1, lane, V7X_LANES), axis=-1, keepdims=True)
    le2 = jnp.where(lane == i1, ninf, le)
    v2 = jnp.max(le2, axis=-1, keepdims=True)
    i2 = jnp.min(jnp.where(le2 == v2, lane, V7X_LANES), axis=-1, keepdims=True)
    e2 = jnp.exp(v2 - v1)
    den = 1.0 + e2
    eid_ref[...] = jnp.where(lane == 0, i1 - N_GROUPS, jnp.where(lane == 1, i2 - N_GROUPS, 0))
    gate_ref[...] = jnp.where(lane == 0, p_grp / den, jnp.where(lane == 1, p_grp * e2 / den, 0.0))


def _router(x, g, w_hi, w_mid, w_lo, b):
    t, d = x.shape
    assert d == SLAB_ROWS * V7X_LANES
    tr = min(256, t)
    full = lambda shp: pl.BlockSpec(shp, lambda i: (0, 0))
    return pl.pallas_call(
        _router_kernel,
        grid=(t // tr,),
        in_specs=[pl.BlockSpec((tr, d), lambda i: (i, 0)), full((1, d)),
                  full((d, V7X_LANES)), full((d, V7X_LANES)), full((d, V7X_LANES)), full((1, V7X_LANES))],
        out_specs=[pl.BlockSpec((tr * SLAB_PITCH, V7X_LANES), lambda i: (i, 0)),
                   pl.BlockSpec((tr, V7X_LANES), lambda i: (i, 0)),
                   pl.BlockSpec((tr, V7X_LANES), lambda i: (i, 0))],
        out_shape=[jax.ShapeDtypeStruct((t * SLAB_PITCH, V7X_LANES), F32),
                   jax.ShapeDtypeStruct((t, V7X_LANES), jnp.int32),
                   jax.ShapeDtypeStruct((t, V7X_LANES), F32)],
        compiler_params=_cparams(("parallel",), VMEM_MID),
    )(x, g.reshape(1, d), w_hi, w_mid, w_lo, b)


def _slab_copy(src_ref, dst_ref, sem, src_slab, dst_slab, src_pitch, dst_pitch, rows):
    src0 = pl.multiple_of(src_slab * src_pitch, V7X_SUBLANES)
    dst0 = pl.multiple_of(dst_slab * dst_pitch, V7X_SUBLANES)
    return pltpu.make_async_copy(src_ref.at[pl.ds(src0, rows)], dst_ref.at[pl.ds(dst0, rows)], sem)


def _unpack_rows(xg, r0, rows):
    parts = [xg[pl.ds(r0 * SLAB_PITCH + s, rows, stride=SLAB_PITCH), :].astype(BF16) for s in range(SLAB_ROWS)]
    return jnp.concatenate(parts, axis=1)


def _moe_kernel(ue_ref, un_ref, tab_ref, tabn_ref, hp_ref, wg_ref, wu_ref, wd_ref, rw_ref, yt_ref,
                xg, xb, acc, yst, gsem, ssem, cnt):
    u = pl.program_id(0)
    f = pl.program_id(1)
    n_units = pl.num_programs(0)
    n_f = pl.num_programs(1)
    n = un_ref[u]
    n_next = jnp.where(u + 1 < n_units, un_ref[jnp.minimum(u + 1, n_units - 1)], 0)
    unit_rows = acc.shape[0]

    def x_copy(k, tok):
        return _slab_copy(hp_ref, xg, gsem, tok, k, SLAB_PITCH, SLAB_PITCH, SLAB_ROWS)

    def y_copy(k, slot):
        return _slab_copy(yst, yt_ref, ssem, k, slot, SLAB_PITCH, SLAB_PITCH, SLAB_PITCH)

    def issue_gather(table_ref, count):
        def body(k, c):
            x_copy(k, table_ref[0, 0, k]).start()
            return c
        lax.fori_loop(0, count, body, 0)

    def drain_scatter():
        def body(k, c):
            y_copy(0, 0).wait()
            return c
        lax.fori_loop(0, cnt[0], body, 0)
        cnt[0] = 0

    @pl.when((u == 0) & (f == 0))
    def _():
        xg[...] = jnp.zeros(xg.shape, xg.dtype)
        yst[...] = jnp.zeros(yst.shape, yst.dtype)
        cnt[0] = 0
        issue_gather(tab_ref, n)

    @pl.when((f == 0) & (n > 0))
    def _():
        def drain(k, c):
            x_copy(0, 0).wait()
            return c
        lax.fori_loop(0, n, drain, 0)
        xb[...] = _unpack_rows(xg, 0, unit_rows)
        acc[...] = jnp.zeros(acc.shape, acc.dtype)

    @pl.when((f == 1) & (n_next > 0))
    def _():
        issue_gather(tabn_ref, n_next)

    @pl.when(n > 0)
    def _():
        wg = wg_ref[0].astype(BF16)
        wu = wu_ref[0].astype(BF16)
        wd = wd_ref[0].astype(BF16)
        d = acc.shape[1]
        col_chunk = min(1024, d)

        def sub_block(r0, rows):
            x = xb[r0:r0 + rows, :]
            gt = _dot(x, wg)
            up = _dot(x, wu)
            a = (gt * (1.0 / (1.0 + jnp.exp(-gt))) * up).astype(BF16)
            for c0 in range(0, d, col_chunk):
                acc[r0:r0 + rows, c0:c0 + col_chunk] += _dot(a, wd[:, c0:c0 + col_chunk])

        sub_block(0, MOE_SUB_ROWS)

        @pl.when(n > MOE_SUB_ROWS)
        def _():
            sub_block(MOE_SUB_ROWS, unit_rows - MOE_SUB_ROWS)

    @pl.when((f == n_f - 1) & (n > 0))
    def _():
        drain_scatter()
        rw = rw_ref[0]
        for s in range(SLAB_ROWS):
            yst[pl.ds(s, unit_rows, stride=SLAB_PITCH), :] = acc[:, s * V7X_LANES:(s + 1) * V7X_LANES] * rw

        def scatter(k, c):
            y_copy(k, tab_ref[0, 0, unit_rows + k]).start()
            return c
        lax.fori_loop(0, n, scatter, 0)
        cnt[0] = n

    @pl.when((u == n_units - 1) & (f == n_f - 1))
    def _():
        drain_scatter()


def _moe_experts(hp, tab, row_w, unit_e, unit_n, w_gate, w_up, w_down, t):
    n_units = unit_e.shape[0]
    r = MOE_UNIT_ROWS
    _, d, d_exp = w_gate.shape
    assert d == SLAB_ROWS * V7X_LANES
    tf = min(256, d_exp)
    nf = d_exp // tf
    assert nf >= 2

    def wchunk(u, f, ue, un):
        return jnp.where(un[u] > 0, f, nf - 1)

    return pl.pallas_call(
        _moe_kernel,
        grid_spec=pltpu.PrefetchScalarGridSpec(
            num_scalar_prefetch=2,
            grid=(n_units, nf),
            in_specs=[
                pl.BlockSpec((1, 1, 2 * r), lambda u, f, ue, un: (u, 0, 0), memory_space=pltpu.SMEM),
                pl.BlockSpec((1, 1, 2 * r), lambda u, f, ue, un: (jnp.minimum(u + 1, n_units - 1), 0, 0),
                             memory_space=pltpu.SMEM),
                pl.BlockSpec(memory_space=pl.ANY),
                pl.BlockSpec((1, d, tf), lambda u, f, ue, un: (ue[u], 0, wchunk(u, f, ue, un))),
                pl.BlockSpec((1, d, tf), lambda u, f, ue, un: (ue[u], 0, wchunk(u, f, ue, un))),
                pl.BlockSpec((1, tf, d), lambda u, f, ue, un: (ue[u], wchunk(u, f, ue, un), 0)),
                pl.BlockSpec((1, r, 1), lambda u, f, ue, un: (u, 0, 0)),
            ],
            out_specs=pl.BlockSpec(memory_space=pl.ANY),
            scratch_shapes=[
                pltpu.VMEM((r * SLAB_PITCH, V7X_LANES), F32),
                pltpu.VMEM((r, d), BF16),
                pltpu.VMEM((r, d), F32),
                pltpu.VMEM((r * SLAB_PITCH, V7X_LANES), F32),
                pltpu.SemaphoreType.DMA(()),
                pltpu.SemaphoreType.DMA(()),
                pltpu.SMEM((1,), jnp.int32),
            ],
        ),
        out_shape=jax.ShapeDtypeStruct((EXPERT_TOPK * t * SLAB_PITCH, V7X_LANES), F32),
        compiler_params=_cparams(("arbitrary", "arbitrary"), VMEM_BIG),
    )(unit_e, unit_n, tab, tab, hp, w_gate, w_up, w_down, row_w)


def _final_kernel(x_ref, y0_ref, y1_ref, g_ref, o_ref):
    tq = x_ref.shape[0]
    parts = []
    for s in range(SLAB_ROWS):
        parts.append(y0_ref[pl.ds(s, tq, stride=SLAB_PITCH), :] + y1_ref[pl.ds(s, tq, stride=SLAB_PITCH), :])
    x = x_ref[...] + jnp.concatenate(parts, axis=1)
    ms = jnp.mean(x * x, axis=-1, keepdims=True)
    o_ref[...] = x * lax.rsqrt(ms + EPS) * g_ref[...]


def _final(x2, yt, g_final):
    t, d = x2.shape
    tq = min(128, t)
    nt = t // tq
    return pl.pallas_call(
        _final_kernel,
        grid=(nt,),
        in_specs=[
            pl.BlockSpec((tq, d), lambda i: (i, 0)),
            pl.BlockSpec((tq * SLAB_PITCH, V7X_LANES), lambda i: (i, 0)),
            pl.BlockSpec((tq * SLAB_PITCH, V7X_LANES), lambda i: (nt + i, 0)),
            pl.BlockSpec((1, d), lambda i: (0, 0)),
        ],
        out_specs=pl.BlockSpec((tq, d), lambda i: (i, 0)),
        out_shape=jax.ShapeDtypeStruct((t, d), F32),
        compiler_params=_cparams(("parallel",), VMEM_MID),
    )(x2, yt, yt, g_final.reshape(1, d))


def _alibi_slopes(n):
    return jnp.exp2(-8.0 * jnp.arange(1, n + 1, dtype=F32) / n)


def _hybrid_mixer(x, norm_g, w_in, w_out, pe_k, w1_k, w2_k, pe_v, w1_v, w2_v):
    t, d = x.shape
    assert t % (CMP_STRIDE * 8) == 0 and t // SEL_BLOCK <= V7X_LANES
    h = _rmsnorm(x, norm_g)

    g0 = NSA_Q_W + 6 * NSA_KV_W
    assert g0 == NSA_BLKS * HEAD_DIM
    nsa_scale = np.ones((g0,), np.float32)
    nsa_scale[QN_BLK * HEAD_DIM:KC_BLK * HEAD_DIM] = SCALE * LOG2E
    tn = 512
    proj = _matmul_wcast(h, w_in, jnp.asarray(nsa_scale), 0, g0 // tn, tn, BF16)
    gates = _matmul_wcast(h, w_in, jnp.ones((V7X_LANES,), F32), g0 // V7X_LANES, 1, V7X_LANES, F32)
    dil_scale = np.ones((3 * DIL_HEADS * HEAD_DIM,), np.float32)
    dil_scale[:DIL_HEADS * HEAD_DIM] = SCALE * LOG2E
    w_dil = (w_in[:, g0 + NSA_GATE_W:] * dil_scale).astype(BF16)
    proj_dil = _matmul(h, w_dil, BF16)
    slopes = _alibi_slopes(N_MIX_HEADS) * LOG2E

    nc = t // CMP_STRIDE
    kv_raw = proj[:, KC_BLK * HEAD_DIM:KS_BLK * HEAD_DIM]
    a2 = kv_raw.reshape(nc, CMP_STRIDE, 2, NSA_KV_GROUPS, HEAD_DIM).transpose(2, 3, 0, 1, 4)
    a2 = a2.reshape(2, NSA_KV_GROUPS, nc, CMP_STRIDE * HEAD_DIM)
    w1 = jnp.stack([w1_k, w1_v]).astype(BF16)
    w2 = jnp.stack([w2_k, w2_v]).astype(BF16)
    pe = jnp.stack([pe_k, pe_v]).reshape(2, 1, CMP_LEN * HEAD_DIM)
    pe = jnp.broadcast_to(pe, (2, 8, CMP_LEN * HEAD_DIM)).astype(BF16)
    kvc = _compress(a2, w1, pe, w2)

    n_blk = t // SEL_BLOCK
    n_sel = min(SEL_TOPK, n_blk)
    ci = CMP_STRIDE * np.arange(nc)[:, None]
    sj = SEL_BLOCK * np.arange(V7X_LANES)[None, :]
    overlap = jnp.asarray(((ci < sj + SEL_BLOCK) & (ci + CMP_LEN > sj)).astype(np.float32), dtype=BF16)
    o_cmp, sel = _cmp_attention(proj, kvc, slopes, overlap, n_sel)

    tq = min(512, t)
    kblk = (np.arange(t) // SEL_BLOCK).reshape(t // tq, 1, tq)
    expand = jnp.asarray((kblk == np.arange(V7X_LANES)[None, :, None]).astype(np.float32), dtype=BF16)
    o_sel = _sel_attention(proj, sel, expand, slopes)

    o_win = _win_attention(proj, slopes)
    o_dil = _dil_attention(proj_dil, slopes)
    o_nsa = _combine(gates, o_cmp, o_sel, o_win)
    return _matmul([o_nsa] + list(o_dil), w_out.astype(BF16), F32, residual=x)


def _cross_block(x, mem, norm_cross, norm_mem, wq, wkv, wo):
    h = _rmsnorm(x, norm_cross)
    m = _rmsnorm(mem, norm_mem)
    q = _matmul(h, wq.astype(BF16), BF16)
    kv = _matmul(m, wkv.astype(BF16), BF16)
    return _cross_attention(q, kv, wo.astype(BF16), x)


def _moe_tables(eid, gate, t):
    r = MOE_UNIT_ROWS
    n_assign = t * EXPERT_TOPK
    e_flat = eid[:, :EXPERT_TOPK].reshape(n_assign)
    w_flat = gate[:, :EXPERT_TOPK].reshape(n_assign)
    order = jnp.argsort(e_flat).astype(jnp.int32)
    e_s = e_flat[order]
    experts = jnp.arange(N_EXPERTS, dtype=jnp.int32)
    counts = jnp.sum((e_flat[None, :] == experts[:, None]).astype(jnp.int32), axis=1)
    starts = jnp.cumsum(counts) - counts
    pcounts = (counts + r - 1) // r * r
    pends = jnp.cumsum(pcounts)
    pstarts = pends - pcounts
    row = (pstarts[e_s] + (jnp.arange(n_assign, dtype=jnp.int32) - starts[e_s])).astype(jnp.int32)
    n_units = n_assign // r + N_EXPERTS
    n_rows = n_units * r
    tok_s = order // EXPERT_TOPK
    slot_s = (order % EXPERT_TOPK) * t + tok_s
    w_bits = lax.bitcast_convert_type(w_flat[order], jnp.int32)
    fields = jnp.zeros((n_rows, 3), jnp.int32).at[row].set(jnp.stack([tok_s, slot_s, w_bits], axis=1))
    row_tok, row_slot = fields[:, 0], fields[:, 1]
    row_w = lax.bitcast_convert_type(fields[:, 2], F32)
    unit_start = jnp.arange(n_units, dtype=jnp.int32) * r
    active = unit_start < pends[-1]
    e_raw = jnp.minimum(jnp.sum(pends[None, :] <= unit_start[:, None], axis=1), N_EXPERTS - 1).astype(jnp.int32)
    e_last = e_raw[pends[-1] // r - 1]
    unit_e = jnp.where(active, e_raw, e_last)
    unit_n = jnp.where(active, jnp.clip(counts[unit_e] - (unit_start - pstarts[unit_e]), 0, r), 0).astype(jnp.int32)
    tab = jnp.concatenate([row_tok.reshape(n_units, 1, r), row_slot.reshape(n_units, 1, r)], axis=2)
    return tab, row_w.reshape(n_units, r, 1), unit_e, unit_n


def _moe_block(x, norm_ffn, w_rg, b_rg, w_re, b_re, w_gate, w_up, w_down, norm_final):
    t, d = x.shape
    pad = V7X_LANES - N_GROUPS - N_EXPERTS
    w_r = jnp.pad(jnp.concatenate([w_rg, w_re], axis=1), ((0, 0), (0, pad)))
    b_r = jnp.pad(jnp.concatenate([b_rg, b_re]), (0, pad)).reshape(1, V7X_LANES)
    w_hi, w_mid, w_lo = _split3(w_r)
    hp, eid, gate = _router(x, norm_ffn, w_hi, w_mid, w_lo, b_r)
    tab, row_w, unit_e, unit_n = _moe_tables(eid, gate, t)
    yt = _moe_experts(hp, tab, row_w, unit_e, unit_n, w_gate, w_up, w_down, t)
    return _final(x, yt, norm_final)


def kernel(x, mem, norm_mix, w_in, w_out, cmp_pe_k, cmp_w1_k, cmp_w2_k, cmp_pe_v, cmp_w1_v, cmp_w2_v, norm_cross, norm_mem, w_q_cross, w_kv_cross, w_o_cross, norm_ffn, w_router_group, b_router_group, w_router_expert, b_router_expert, w_gate, w_up, w_down, norm_final):
    b, t, d = x.shape
    depth = norm_mix.shape[0]
    assert b == 1 and depth == 1
    xs = x.reshape(t, d)
    l = 0
    xs = _hybrid_mixer(xs, norm_mix[l], w_in[l], w_out[l], cmp_pe_k[l], cmp_w1_k[l], cmp_w2_k[l],
                       cmp_pe_v[l], cmp_w1_v[l], cmp_w2_v[l])
    xs = _cross_block(xs, mem.reshape(mem.shape[1], d), norm_cross[l], norm_mem[l], w_q_cross[l],
                      w_kv_cross[l], w_o_cross[l])
    out = _moe_block(xs, norm_ffn[l], w_router_group[l], b_router_group[l], w_router_expert[l],
                     b_router_expert[l], w_gate[l], w_up[l], w_down[l], norm_final)
    return out.reshape(b, t, d)
```

```python
import functools

import numpy as np
import jax
import jax.numpy as jnp
from jax import lax
from jax.experimental import pallas as pl
from jax.experimental.pallas import tpu as pltpu

F32 = jnp.float32
BF16 = jnp.bfloat16

HEAD_DIM = 128
DIL_CONFIGS = ((128, 1), (512, 4), (2048, 16))
DIL_HEADS_PER_CFG = 4
DIL_HEADS = DIL_HEADS_PER_CFG * len(DIL_CONFIGS)
N_MIX_HEADS = 32
NSA_HEADS = N_MIX_HEADS - DIL_HEADS
NSA_KV_GROUPS = 4
NSA_REP = NSA_HEADS // NSA_KV_GROUPS
CMP_LEN = 32
CMP_STRIDE = 16
SEL_BLOCK = 64
SEL_TOPK = 16
WIN = 512
CROSS_HEADS = 4
N_GROUPS = 8
EXPERTS_PER_GROUP = 8
N_EXPERTS = N_GROUPS * EXPERTS_PER_GROUP
EXPERT_TOPK = 2
EPS = 1e-6
NEG = -1e30
FORCE = 1e4
SCALE = HEAD_DIM ** -0.5
LOG2E = float(np.log2(np.e))

V7X_LANES = 128
V7X_SUBLANES = 8
V7X_VMEM_BYTES = 64 * 1024 * 1024
VMEM_BIG = 56 * 1024 * 1024
VMEM_MID = 40 * 1024 * 1024

QN_BLK = 0
KC_BLK = NSA_HEADS
VC_BLK = KC_BLK + NSA_KV_GROUPS
KS_BLK = VC_BLK + NSA_KV_GROUPS
VS_BLK = KS_BLK + NSA_KV_GROUPS
KW_BLK = VS_BLK + NSA_KV_GROUPS
VW_BLK = KW_BLK + NSA_KV_GROUPS
NSA_BLKS = VW_BLK + NSA_KV_GROUPS
QD_BLK = 0
KD_BLK = QD_BLK + DIL_HEADS
VD_BLK = KD_BLK + DIL_HEADS
NSA_Q_W = NSA_HEADS * HEAD_DIM
NSA_KV_W = NSA_KV_GROUPS * HEAD_DIM
NSA_GATE_W = NSA_HEADS * 3
DIL_CFG_W = DIL_HEADS_PER_CFG * HEAD_DIM

SLAB_ROWS = 32
SLAB_PITCH = 40
MOE_UNIT_ROWS = 320
MOE_SUB_ROWS = 256


def _cparams(sem, vmem=None):
    return pltpu.CompilerParams(dimension_semantics=sem, vmem_limit_bytes=vmem)


def _dot(a, b):
    return jnp.dot(a, b, preferred_element_type=F32)


def _dot_nt(a, b):
    return lax.dot_general(a, b, (((1,), (1,)), ((), ())), preferred_element_type=F32)


def _split3(a):
    hi = a.astype(BF16)
    r1 = a - hi.astype(F32)
    mid = r1.astype(BF16)
    lo = (r1 - mid.astype(F32)).astype(BF16)
    return hi, mid, lo


def _rms_kernel(x_ref, g_ref, o_ref):
    x = x_ref[...]
    ms = jnp.mean(x * x, axis=-1, keepdims=True)
    o_ref[...] = (x * lax.rsqrt(ms + EPS) * g_ref[...]).astype(o_ref.dtype)


def _rmsnorm(x, g, out_dtype=BF16):
    n, d = x.shape
    tr = min(256, n)
    return pl.pallas_call(
        _rms_kernel,
        grid=(n // tr,),
        in_specs=[pl.BlockSpec((tr, d), lambda i: (i, 0)), pl.BlockSpec((1, d), lambda i: (0, 0))],
        out_specs=pl.BlockSpec((tr, d), lambda i: (i, 0)),
        out_shape=jax.ShapeDtypeStruct((n, d), out_dtype),
        compiler_params=_cparams(("parallel",)),
    )(x, g.reshape(1, d))


def _mm_kernel(*refs, widths, has_res):
    n_x = len(widths)
    x_refs = refs[:n_x]
    w_ref = refs[n_x]
    o_ref = refs[-1]
    acc = None
    off = 0
    for x_ref, kw in zip(x_refs, widths):
        part = _dot(x_ref[...], w_ref[off:off + kw, :])
        acc = part if acc is None else acc + part
        off += kw
    if has_res:
        acc = refs[n_x + 1][...] + acc
    o_ref[...] = acc.astype(o_ref.dtype)


def _matmul(xs, w, out_dtype, residual=None):
    if not isinstance(xs, (list, tuple)):
        xs = [xs]
    m = xs[0].shape[0]
    widths = tuple(x.shape[1] for x in xs)
    k, n = w.shape
    assert sum(widths) == k
    tm = min(1024, m)
    tn = min(512, n)
    in_specs = [pl.BlockSpec((tm, kw), lambda i, j: (i, 0)) for kw in widths]
    in_specs.append(pl.BlockSpec((k, tn), lambda i, j: (0, j)))
    args = list(xs) + [w]
    if residual is not None:
        in_specs.append(pl.BlockSpec((tm, tn), lambda i, j: (i, j)))
        args.append(residual)
    return pl.pallas_call(
        functools.partial(_mm_kernel, widths=widths, has_res=residual is not None),
        grid=(m // tm, n // tn),
        in_specs=in_specs,
        out_specs=pl.BlockSpec((tm, tn), lambda i, j: (i, j)),
        out_shape=jax.ShapeDtypeStruct((m, n), out_dtype),
        compiler_params=_cparams(("parallel", "parallel"), VMEM_BIG),
    )(*args)


def _mm_wcast_kernel(x_ref, w_ref, *rest, lane_off):
    if lane_off:
        wt_ref, s_ref, o_ref, wb_sc = rest
    else:
        s_ref, o_ref, wb_sc = rest
    tn = o_ref.shape[1]

    @pl.when(pl.program_id(1) == 0)
    def _():
        w = w_ref[...]
        if lane_off:
            w = jnp.concatenate([w, wt_ref[...]], axis=1)[:, lane_off:lane_off + tn]
        wb_sc[...] = (w * s_ref[...]).astype(BF16)

    o_ref[...] = _dot(x_ref[...], wb_sc[...]).astype(o_ref.dtype)


def _matmul_wcast(x, w, col_scale, col0, n_col_blks, tn, out_dtype):
    m, k = x.shape
    tm = min(1024, m)
    lane_off = col0 % tn
    assert lane_off < V7X_LANES and tn % V7X_LANES == 0
    blk0 = col0 // tn
    in_specs = [pl.BlockSpec((tm, k), lambda j, i: (i, 0)),
                pl.BlockSpec((k, tn), lambda j, i: (0, blk0 + j))]
    args = [x, w]
    if lane_off:
        lanes_per_blk = tn // V7X_LANES
        in_specs.append(pl.BlockSpec((k, V7X_LANES), lambda j, i: (0, (blk0 + j + 1) * lanes_per_blk)))
        args.append(w)
    in_specs.append(pl.BlockSpec((1, tn), lambda j, i: (0, j)))
    args.append(col_scale.reshape(1, n_col_blks * tn))
    return pl.pallas_call(
        functools.partial(_mm_wcast_kernel, lane_off=lane_off),
        grid=(n_col_blks, m // tm),
        in_specs=in_specs,
        out_specs=pl.BlockSpec((tm, tn), lambda j, i: (i, j)),
        out_shape=jax.ShapeDtypeStruct((m, n_col_blks * tn), out_dtype),
        scratch_shapes=[pltpu.VMEM((k, tn), BF16)],
        compiler_params=_cparams(("parallel", "arbitrary"), VMEM_BIG),
    )(*args)


def _gelu_tanh(x):
    return 0.5 * x * (1.0 + jnp.tanh(np.sqrt(2.0 / np.pi) * (x + 0.044715 * (x * x * x))))


def _cmp_kernel(a_ref, w1_ref, pe_ref, w2_ref, o_ref):
    a = a_ref[0, 0]
    nc = a.shape[0]
    half = CMP_STRIDE * HEAD_DIM
    ha = _dot(a, w1_ref[0, :half, :])
    hb = _dot(a, w1_ref[0, half:, :])
    c = _dot(pe_ref[0], w1_ref[0])[0:1]
    pre = ha + pltpu.roll(hb, nc - 1, 0) + c
    hid = _gelu_tanh(pre)
    o_ref[0, 0] = _dot(hid.astype(BF16), w2_ref[0]).astype(o_ref.dtype)


def _compress(a2, w1, pe, w2):
    _, g, nc, ck = a2.shape
    hid = w1.shape[-1]
    return pl.pallas_call(
        _cmp_kernel,
        grid=(2, g),
        in_specs=[
            pl.BlockSpec((1, 1, nc, ck), lambda s, gg: (s, gg, 0, 0)),
            pl.BlockSpec((1, 2 * ck, hid), lambda s, gg: (s, 0, 0)),
            pl.BlockSpec((1, 8, 2 * ck), lambda s, gg: (s, 0, 0)),
            pl.BlockSpec((1, hid, HEAD_DIM), lambda s, gg: (s, 0, 0)),
        ],
        out_specs=pl.BlockSpec((1, 1, nc, HEAD_DIM), lambda s, gg: (s, gg, 0, 0)),
        out_shape=jax.ShapeDtypeStruct((2, g, nc, HEAD_DIM), BF16),
        compiler_params=_cparams(("parallel", "parallel"), VMEM_MID),
    )(a2, w1, pe, w2)


def _cmpattn_kernel(slopes_ref, q_ref, kc_ref, vc_ref, ov_ref, o_ref, sel_ref, *, n_sel, row0):
    g = pl.program_id(0)
    i = pl.program_id(1)
    tq = q_ref.shape[0]
    nc = kc_ref.shape[2]
    t = row0 + i * tq + lax.broadcasted_iota(jnp.int32, (tq, 1), 0)
    n = lax.broadcasted_iota(jnp.int32, (1, nc), 1)
    dist = (t - (CMP_STRIDE * n + CMP_LEN - 1)).astype(F32)
    madd = jnp.where(dist >= 0, 0.0, NEG)
    has_block = t >= CMP_LEN - 1
    kc = kc_ref[0, 0]
    vc = vc_ref[0, 0]
    psum = jnp.zeros((tq, nc), F32)
    for r in range(NSA_REP):
        slope = slopes_ref[g * NSA_REP + r]
        s = (_dot_nt(q_ref[:, r * HEAD_DIM:(r + 1) * HEAD_DIM], kc) - slope * dist) + madd
        m = jnp.max(s, axis=-1, keepdims=True)
        e = jnp.exp2(s - m)
        l = jnp.sum(e, axis=-1, keepdims=True)
        p = e * jnp.where(has_block, 1.0 / l, 0.0)
        o_ref[:, r * HEAD_DIM:(r + 1) * HEAD_DIM] = _dot(p.astype(BF16), vc).astype(o_ref.dtype)
        psum = psum + p
    ov = ov_ref[...]
    hi, mid, lo = _split3(psum)
    imp = _dot(hi, ov) + _dot(mid, ov) + _dot(lo, ov)
    blk = lax.broadcasted_iota(jnp.int32, (1, V7X_LANES), 1)
    cur = t // SEL_BLOCK
    causal = blk * SEL_BLOCK <= t
    forced = (blk == 0) | (blk == cur) | (blk == cur - 1)
    work = jnp.where(causal, jnp.where(forced, FORCE, imp), NEG)
    sel = jnp.zeros((tq, V7X_LANES), F32)
    for _ in range(n_sel):
        mx = jnp.max(work, axis=-1, keepdims=True)
        idx = jnp.min(jnp.where(work == mx, blk, V7X_LANES), axis=-1, keepdims=True)
        pick = blk == idx
        sel = jnp.where(pick, 1.0, sel)
        work = jnp.where(pick, -jnp.inf, work)
    sel_ref[0] = jnp.where(causal, sel, 0.0)


CMP_SEGMENT = 2048


def _cmp_attention(proj, kvc, slopes, overlap, n_sel):
    t = proj.shape[0]
    g = NSA_KV_GROUPS
    tq = min(512, t)
    qw = NSA_REP * HEAD_DIM
    seg = min(CMP_SEGMENT, t)
    assert t % seg == 0 and seg % tq == 0 and seg % (CMP_STRIDE * V7X_SUBLANES) == 0
    outs, sels = [], []
    for q in range(t // seg):
        nc = (q + 1) * seg // CMP_STRIDE
        tile0 = q * seg // tq
        o_q, sel_q = pl.pallas_call(
            functools.partial(_cmpattn_kernel, n_sel=n_sel, row0=q * seg),
            grid=(g, seg // tq),
            in_specs=[
                pl.BlockSpec(memory_space=pltpu.SMEM),
                pl.BlockSpec((tq, qw), lambda gg, i, tile0=tile0: (tile0 + i, gg)),
                pl.BlockSpec((1, 1, nc, HEAD_DIM), lambda gg, i: (0, gg, 0, 0)),
                pl.BlockSpec((1, 1, nc, HEAD_DIM), lambda gg, i: (1, gg, 0, 0)),
                pl.BlockSpec((nc, V7X_LANES), lambda gg, i: (0, 0)),
            ],
            out_specs=[
                pl.BlockSpec((tq, qw), lambda gg, i: (i, gg)),
                pl.BlockSpec((1, tq, V7X_LANES), lambda gg, i: (gg, i, 0)),
            ],
            out_shape=[
                jax.ShapeDtypeStruct((seg, NSA_Q_W), BF16),
                jax.ShapeDtypeStruct((g, seg, V7X_LANES), F32),
            ],
            compiler_params=_cparams(("parallel", "parallel"), VMEM_MID),
        )(slopes, proj, kvc, kvc, overlap)
        outs.append(o_q)
        sels.append(sel_q)
    return jnp.concatenate(outs, axis=0), jnp.concatenate(sels, axis=1)


def _flash_scratch(n_rows):
    return [pltpu.VMEM((n_rows, V7X_LANES), F32), pltpu.VMEM((n_rows, 2 * HEAD_DIM), F32)]


def _flash_init(m_sc, acc_sc):
    m_sc[...] = jnp.full(m_sc.shape, NEG, F32)
    acc_sc[...] = jnp.zeros(acc_sc.shape, F32)


def _with_ones(v):
    return jnp.concatenate([v, jnp.ones(v.shape, v.dtype)], axis=1)


def _tile_dist(q0, k0, rows, cols):
    return (q0 + lax.broadcasted_iota(jnp.int32, (rows, 1), 0)) - (k0 + lax.broadcasted_iota(jnp.int32, (1, cols), 1))


def _flash_update(rows, s, vx, m_sc, acc_sc):
    reps = s.shape[1] // V7X_LANES
    m_prev = m_sc[rows, :]
    m_new = jnp.maximum(m_prev, jnp.max(s, axis=-1, keepdims=True))
    alpha = jnp.exp2(m_prev - m_new)
    p = jnp.exp2(s - jnp.concatenate([m_new] * reps, axis=1))
    acc_sc[rows, :] = jnp.concatenate([alpha, alpha], axis=1) * acc_sc[rows, :] + _dot(p.astype(BF16), vx)
    m_sc[rows, :] = m_new


def _flash_out(rows, acc_sc):
    return acc_sc[rows, :HEAD_DIM] / acc_sc[rows, HEAD_DIM:]


def _flash_store(o_ref, acc_sc, n_rep, tq):
    for r in range(n_rep):
        o = _flash_out(slice(r * tq, (r + 1) * tq), acc_sc)
        o_ref[:, r * HEAD_DIM:(r + 1) * HEAD_DIM] = o.astype(o_ref.dtype)


MASK_BIG = 2.0 ** 60
SEL_TILE_BLOCKS = 8
SLOPE_LANE0 = SEL_TILE_BLOCKS


def _selattn_kernel(pi_ref, pj_ref, q_ref, k_ref, v_ref, sel_ref, shift_ref, kx_ref, sv_ref, o_ref, m_sc, acc_sc):
    step = pl.program_id(1)
    i = pi_ref[step]
    j = pj_ref[step]
    tq = q_ref.shape[0]
    tk = k_ref.shape[0]

    @pl.when(j == 0)
    def _():
        _flash_init(m_sc, acc_sc)

    def tile(diagonal):
        kk = jnp.concatenate([k_ref[...], kx_ref[0]], axis=1)
        vx = _with_ones(v_ref[...])
        bits = _dot(sel_ref[0].astype(BF16), shift_ref[0])
        lane = lax.broadcasted_iota(jnp.int32, (1, V7X_LANES), 1)
        mask_cols = jnp.where(lane < SEL_TILE_BLOCKS, (bits - 1.0) * MASK_BIG, 0.0)
        if diagonal:
            keep = _tile_dist(0, 0, tq, tk) >= 0
        for r in range(NSA_REP):
            extra = (mask_cols + sv_ref[0, r:r + 1, :].astype(F32)).astype(BF16)
            qq = jnp.concatenate([q_ref[:, r * HEAD_DIM:(r + 1) * HEAD_DIM], extra], axis=1)
            s = _dot_nt(qq, kk)
            if diagonal:
                s = jnp.where(keep, s, -MASK_BIG)
            _flash_update(slice(r * tq, (r + 1) * tq), s, vx, m_sc, acc_sc)

    @pl.when(j < i)
    def _():
        tile(False)

    @pl.when(j == i)
    def _():
        tile(True)
        _flash_store(o_ref, acc_sc, NSA_REP, tq)


def _fold_tables(slopes, n_tiles, tq):
    lane = np.arange(V7X_LANES)
    col = np.arange(tq)[:, None]
    kx = np.zeros((n_tiles, tq, V7X_LANES), np.float32)
    for dtile in range(n_tiles):
        rel = col - dtile * tq
        a = 64 * np.floor_divide(rel, 64)
        kx[dtile, :, :SEL_TILE_BLOCKS] = (col // SEL_BLOCK == lane[None, :SEL_TILE_BLOCKS])
        kx[dtile, :, SLOPE_LANE0:SLOPE_LANE0 + 3] = a
        kx[dtile, :, SLOPE_LANE0 + 3:SLOPE_LANE0 + 6] = rel - a
    s3 = jnp.stack(_split3(slopes[:NSA_HEADS]), axis=1).astype(F32)
    sv = jnp.zeros((NSA_HEADS, V7X_LANES), F32)
    sv = sv.at[:, SLOPE_LANE0:SLOPE_LANE0 + 3].set(s3).at[:, SLOPE_LANE0 + 3:SLOPE_LANE0 + 6].set(s3)
    sv = sv.reshape(NSA_KV_GROUPS, NSA_REP, V7X_LANES)
    sv = jnp.pad(sv, ((0, 0), (0, V7X_SUBLANES - NSA_REP), (0, 0))).astype(BF16)
    return jnp.asarray(kx, BF16), sv


def _sel_attention(proj, sel, slopes):
    t = proj.shape[0]
    g = NSA_KV_GROUPS
    tq = min(512, t)
    assert tq == SEL_TILE_BLOCKS * SEL_BLOCK
    nq = t // tq
    qw = NSA_REP * HEAD_DIM
    pairs = [(i, j) for i in range(nq) for j in range(i + 1)]
    pi = jnp.asarray([p[0] for p in pairs], jnp.int32)
    pj = jnp.asarray([p[1] for p in pairs], jnp.int32)
    lane = np.arange(V7X_LANES)[None, :]
    blk = np.arange(V7X_LANES)[:, None]
    shift = np.stack([(blk == SEL_TILE_BLOCKS * j + lane) & (lane < SEL_TILE_BLOCKS)
                      for j in range(nq)]).astype(np.float32)
    kx, sv = _fold_tables(slopes, nq, tq)
    return pl.pallas_call(
        _selattn_kernel,
        grid_spec=pltpu.PrefetchScalarGridSpec(
            num_scalar_prefetch=2,
            grid=(g, len(pairs)),
            in_specs=[
                pl.BlockSpec((tq, qw), lambda gg, s, pi, pj: (pi[s], gg)),
                pl.BlockSpec((tq, HEAD_DIM), lambda gg, s, pi, pj: (pj[s], KS_BLK + gg)),
                pl.BlockSpec((tq, HEAD_DIM), lambda gg, s, pi, pj: (pj[s], VS_BLK + gg)),
                pl.BlockSpec((1, tq, V7X_LANES), lambda gg, s, pi, pj: (gg, pi[s], 0)),
                pl.BlockSpec((1, V7X_LANES, V7X_LANES), lambda gg, s, pi, pj: (pj[s], 0, 0)),
                pl.BlockSpec((1, tq, V7X_LANES), lambda gg, s, pi, pj: (pi[s] - pj[s], 0, 0)),
                pl.BlockSpec((1, V7X_SUBLANES, V7X_LANES), lambda gg, s, pi, pj: (gg, 0, 0)),
            ],
            out_specs=pl.BlockSpec((tq, qw), lambda gg, s, pi, pj: (pi[s], gg)),
            scratch_shapes=_flash_scratch(NSA_REP * tq),
        ),
        out_shape=jax.ShapeDtypeStruct((t, NSA_Q_W), BF16),
        compiler_params=_cparams(("parallel", "arbitrary"), VMEM_MID),
    )(pi, pj, proj, proj, proj, sel, jnp.asarray(shift, BF16), kx, sv)


def _win_kernel(q_ref, k_ref, v_ref, kx_ref, sv_ref, o_ref, m_sc, acc_sc):
    i = pl.program_id(1)
    j = pl.program_id(2)
    tq = q_ref.shape[0]

    @pl.when(j == 0)
    def _():
        _flash_init(m_sc, acc_sc)

    def tile(current):
        kk = jnp.concatenate([k_ref[...], kx_ref[0]], axis=1)
        vx = _with_ones(v_ref[...])
        rc = _tile_dist(0, 0, tq, tq)
        keep = (rc >= 0) if current else (rc <= WIN - 1 - tq)
        for r in range(NSA_REP):
            extra = jnp.broadcast_to(sv_ref[0, r:r + 1, :], (tq, V7X_LANES))
            qq = jnp.concatenate([q_ref[:, r * HEAD_DIM:(r + 1) * HEAD_DIM], extra], axis=1)
            s = jnp.where(keep, _dot_nt(qq, kk), -MASK_BIG)
            _flash_update(slice(r * tq, (r + 1) * tq), s, vx, m_sc, acc_sc)

    @pl.when((j == 0) & (i > 0))
    def _():
        tile(False)

    @pl.when(j == 1)
    def _():
        tile(True)
        _flash_store(o_ref, acc_sc, NSA_REP, tq)


def _win_attention(proj, slopes):
    t = proj.shape[0]
    tq = min(512, t)
    assert WIN - 1 <= tq
    qw = NSA_REP * HEAD_DIM
    kv_tile = lambda i, j: jnp.maximum(i - 1 + j, 0)
    kx, sv = _fold_tables(slopes, 2, tq)
    return pl.pallas_call(
        _win_kernel,
        grid=(NSA_KV_GROUPS, t // tq, 2),
        in_specs=[
            pl.BlockSpec((tq, qw), lambda g, i, j: (i, g)),
            pl.BlockSpec((tq, HEAD_DIM), lambda g, i, j: (kv_tile(i, j), KW_BLK + g)),
            pl.BlockSpec((tq, HEAD_DIM), lambda g, i, j: (kv_tile(i, j), VW_BLK + g)),
            pl.BlockSpec((1, tq, V7X_LANES), lambda g, i, j: (1 - j, 0, 0)),
            pl.BlockSpec((1, V7X_SUBLANES, V7X_LANES), lambda g, i, j: (g, 0, 0)),
        ],
        out_specs=pl.BlockSpec((tq, qw), lambda g, i, j: (i, g)),
        out_shape=jax.ShapeDtypeStruct((t, NSA_Q_W), BF16),
        scratch_shapes=_flash_scratch(NSA_REP * tq),
        compiler_params=_cparams(("parallel", "parallel", "arbitrary"), VMEM_MID),
    )(proj, proj, proj, kx, sv)


def _dil_tiles(tq):
    return tuple(-(-window // tq) + 1 for window, _ in DIL_CONFIGS)


def _dil_kernel(slopes_ref, *refs):
    n_cfg = len(DIL_CONFIGS)
    q_refs = refs[0:n_cfg]
    k_refs = refs[n_cfg:2 * n_cfg]
    v_refs = refs[2 * n_cfg:3 * n_cfg]
    o_refs = refs[3 * n_cfg:4 * n_cfg]
    m_sc, acc_sc = refs[4 * n_cfg:]
    h = pl.program_id(0)
    i = pl.program_id(1)
    j = pl.program_id(2)
    tq = q_refs[0].shape[0]
    tiles = _dil_tiles(tq)
    n_steps = max(tiles)
    kt = i - (n_steps - 1) + j

    @pl.when(j == 0)
    def _():
        _flash_init(m_sc, acc_sc)

    for c, (window, dil) in enumerate(DIL_CONFIGS):
        @pl.when((j >= n_steps - tiles[c]) & (kt >= 0))
        def _(c=c, window=window, dil=dil):
            dist = _tile_dist(i * tq, kt * tq, tq, tq)
            ok = (dist >= 0) & (dist <= window) & ((dist & (dil - 1)) == 0)
            slope = slopes_ref[NSA_HEADS + c * DIL_HEADS_PER_CFG + h]
            s = _dot_nt(q_refs[c][...], k_refs[c][...]) - slope * dist.astype(F32)
            s = jnp.where(ok, s, NEG)
            _flash_update(slice(c * tq, (c + 1) * tq), s, _with_ones(v_refs[c][...]), m_sc, acc_sc)

    @pl.when(j == n_steps - 1)
    def _():
        rows = [slice(c * tq, (c + 1) * tq) for c in range(n_cfg)]
        lses = [m_sc[rw, :] + jnp.log2(acc_sc[rw, HEAD_DIM:]) for rw in rows]
        mx = functools.reduce(jnp.maximum, lses)
        es = [jnp.exp2(x - mx) for x in lses]
        den = functools.reduce(lambda a, b: a + b, es)
        for c in range(n_cfg):
            o_refs[c][...] = (_flash_out(rows[c], acc_sc) * (es[c] / den)).astype(o_refs[c].dtype)


def _dil_attention(proj, slopes):
    t = proj.shape[0]
    tq = min(512, t)
    n_cfg = len(DIL_CONFIGS)
    for _, dil in DIL_CONFIGS:
        assert dil & (dil - 1) == 0
    tiles = _dil_tiles(tq)
    n_steps = max(tiles)

    def kv_spec(col0, c):
        def idx(h, i, j):
            step = jnp.maximum(j - (n_steps - tiles[c]), 0)
            return (jnp.maximum(i - (tiles[c] - 1) + step, 0), col0 + c * DIL_HEADS_PER_CFG + h)
        return pl.BlockSpec((tq, HEAD_DIM), idx)

    q_specs = [pl.BlockSpec((tq, HEAD_DIM), lambda h, i, j, c=c: (i, QD_BLK + c * DIL_HEADS_PER_CFG + h))
               for c in range(n_cfg)]
    return pl.pallas_call(
        _dil_kernel,
        grid=(DIL_HEADS_PER_CFG, t // tq, n_steps),
        in_specs=[pl.BlockSpec(memory_space=pltpu.SMEM)] + q_specs
        + [kv_spec(KD_BLK, c) for c in range(n_cfg)] + [kv_spec(VD_BLK, c) for c in range(n_cfg)],
        out_specs=[pl.BlockSpec((tq, HEAD_DIM), lambda h, i, j: (i, h)) for _ in range(n_cfg)],
        out_shape=[jax.ShapeDtypeStruct((t, DIL_CFG_W), BF16) for _ in range(n_cfg)],
        scratch_shapes=_flash_scratch(n_cfg * tq),
        compiler_params=_cparams(("parallel", "parallel", "arbitrary"), VMEM_MID),
    )(slopes, *([proj] * (3 * n_cfg)))


def _combine_kernel(gates_ref, ocmp_ref, osel_ref, owin_ref, o_ref):
    sg = 1.0 / (1.0 + jnp.exp(-gates_ref[...]))
    for h in range(NSA_HEADS):
        cs = slice(h * HEAD_DIM, (h + 1) * HEAD_DIM)
        o = (sg[:, 3 * h:3 * h + 1] * ocmp_ref[:, cs].astype(F32)
             + sg[:, 3 * h + 1:3 * h + 2] * osel_ref[:, cs].astype(F32)
             + sg[:, 3 * h + 2:3 * h + 3] * owin_ref[:, cs].astype(F32))
        o_ref[:, cs] = o.astype(o_ref.dtype)


def _combine(gates, o_cmp, o_sel, o_win):
    t = gates.shape[0]
    tr = min(256, t)
    row = lambda w: pl.BlockSpec((tr, w), lambda i: (i, 0))
    return pl.pallas_call(
        _combine_kernel,
        grid=(t // tr,),
        in_specs=[row(V7X_LANES), row(NSA_Q_W), row(NSA_Q_W), row(NSA_Q_W)],
        out_specs=row(NSA_Q_W),
        out_shape=jax.ShapeDtypeStruct((t, NSA_Q_W), BF16),
        compiler_params=_cparams(("parallel",), VMEM_MID),
    )(gates, o_cmp, o_sel, o_win)


def _cross_kernel(q_ref, kv_ref, wo_ref, x_ref, o_ref):
    cw = CROSS_HEADS * HEAD_DIM
    outs = []
    for h in range(CROSS_HEADS):
        q = q_ref[:, h * HEAD_DIM:(h + 1) * HEAD_DIM]
        k = kv_ref[:, h * HEAD_DIM:(h + 1) * HEAD_DIM]
        v = kv_ref[:, cw + h * HEAD_DIM:cw + (h + 1) * HEAD_DIM]
        s = _dot_nt(q, k) * SCALE
        m = jnp.max(s, axis=-1, keepdims=True)
        p = jnp.exp(s - m)
        l = jnp.sum(p, axis=-1, keepdims=True)
        outs.append((_dot(p.astype(BF16), v) / l).astype(BF16))
    o = jnp.concatenate(outs, axis=-1)
    o_ref[...] = x_ref[...] + _dot(o, wo_ref[...])


def _cross_attention(q, kv, wo, x):
    t, d = x.shape
    s_len = kv.shape[0]
    cw = CROSS_HEADS * HEAD_DIM
    tq = min(256, t)
    return pl.pallas_call(
        _cross_kernel,
        grid=(t // tq,),
        in_specs=[
            pl.BlockSpec((tq, cw), lambda i: (i, 0)),
            pl.BlockSpec((s_len, 2 * cw), lambda i: (0, 0)),
            pl.BlockSpec((cw, d), lambda i: (0, 0)),
            pl.BlockSpec((tq, d), lambda i: (i, 0)),
        ],
        out_specs=pl.BlockSpec((tq, d), lambda i: (i, 0)),
        out_shape=jax.ShapeDtypeStruct((t, d), F32),
        compiler_params=_cparams(("parallel",), VMEM_MID),
    )(q, kv, wo, x)


def _router_kernel(x_ref, g_ref, whi_ref, wmid_ref, wlo_ref, b_ref, h_ref, eid_ref, gate_ref):
    x = x_ref[...]
    tr, d = x.shape
    ms = jnp.mean(x * x, axis=-1, keepdims=True)
    h = x * lax.rsqrt(ms + EPS) * g_ref[...]
    h_ref[...] = jnp.zeros(h_ref.shape, h_ref.dtype)
    for s in range(SLAB_ROWS):
        h_ref[pl.ds(s, tr, stride=SLAB_PITCH), :] = h[:, s * V7X_LANES:(s + 1) * V7X_LANES]
    hh, hm, hl = _split3(h)
    whi, wmid, wlo = whi_ref[...], wmid_ref[...], wlo_ref[...]
    logits = (_dot(hh, whi) + (_dot(hh, wmid) + _dot(hm, whi))
              + (_dot(hh, wlo) + _dot(hm, wmid) + _dot(hl, whi))) + b_ref[...]
    lane = lax.broadcasted_iota(jnp.int32, (1, V7X_LANES), 1)
    ninf = -jnp.inf
    is_g = lane < N_GROUPS
    lg = jnp.where(is_g, logits, ninf)
    mg = jnp.max(lg, axis=-1, keepdims=True)
    grp = jnp.min(jnp.where(lg == mg, lane, V7X_LANES), axis=-1, keepdims=True)
    p_grp = 1.0 / jnp.sum(jnp.where(is_g, jnp.exp(lg - mg), 0.0), axis=-1, keepdims=True)
    lo_lane = N_GROUPS + grp * EXPERTS_PER_GROUP
    in_grp = (lane >= lo_lane) & (lane < lo_lane + EXPERTS_PER_GROUP)
    le = jnp.where(in_grp, logits, ninf)
    v1 = jnp.max(le, axis=-1, keepdims=True)
    i1 = jnp.min(jnp.where(le == v1, lane, V7X_LANES), axis=-1, keepdims=True)
    le2 = jnp.where(lane == i1, ninf, le)
    v2 = jnp.max(le2, axis=-1, keepdims=True)
    i2 = jnp.min(jnp.where(le2 == v2, lane, V7X_LANES), axis=-1, keepdims=True)
    e2 = jnp.exp(v2 - v1)
    den = 1.0 + e2
    eid_ref[...] = jnp.where(lane == 0, i1 - N_GROUPS, jnp.where(lane == 1, i2 - N_GROUPS, 0))
    gate_ref[...] = jnp.where(lane == 0, p_grp / den, jnp.where(lane == 1, p_grp * e2 / den, 0.0))


def _router(x, g, w_hi, w_mid, w_lo, b):
    t, d = x.shape
    assert d == SLAB_ROWS * V7X_LANES
    tr = min(256, t)
    full = lambda shp: pl.BlockSpec(shp, lambda i: (0, 0))
    return pl.pallas_call(
        _router_kernel,
        grid=(t // tr,),
        in_specs=[pl.BlockSpec((tr, d), lambda i: (i, 0)), full((1, d)),
                  full((d, V7X_LANES)), full((d, V7X_LANES)), full((d, V7X_LANES)), full((1, V7X_LANES))],
        out_specs=[pl.BlockSpec((tr * SLAB_PITCH, V7X_LANES), lambda i: (i, 0)),
                   pl.BlockSpec((tr, V7X_LANES), lambda i: (i, 0)),
                   pl.BlockSpec((tr, V7X_LANES), lambda i: (i, 0))],
        out_shape=[jax.ShapeDtypeStruct((t * SLAB_PITCH, V7X_LANES), F32),
                   jax.ShapeDtypeStruct((t, V7X_LANES), jnp.int32),
                   jax.ShapeDtypeStruct((t, V7X_LANES), F32)],
        compiler_params=_cparams(("parallel",), VMEM_MID),
    )(x, g.reshape(1, d), w_hi, w_mid, w_lo, b)


def _slab_copy(src_ref, dst_ref, sem, src_slab, dst_slab, src_pitch, dst_pitch, rows):
    src0 = pl.multiple_of(src_slab * src_pitch, V7X_SUBLANES)
    dst0 = pl.multiple_of(dst_slab * dst_pitch, V7X_SUBLANES)
    return pltpu.make_async_copy(src_ref.at[pl.ds(src0, rows)], dst_ref.at[pl.ds(dst0, rows)], sem)


def _unpack_rows(xg, r0, rows):
    parts = [xg[pl.ds(r0 * SLAB_PITCH + s, rows, stride=SLAB_PITCH), :] for s in range(SLAB_ROWS)]
    return jnp.concatenate(parts, axis=1)


def _moe_kernel(ue_ref, un_ref, tab_ref, tabn_ref, hp_ref, wg_ref, wu_ref, wd_ref, rw_ref, yt_ref,
                xg, xb, acc, yst, gsem, ssem, cnt):
    u = pl.program_id(0)
    f = pl.program_id(1)
    n_units = pl.num_programs(0)
    n_f = pl.num_programs(1)
    n = un_ref[u]
    n_next = jnp.where(u + 1 < n_units, un_ref[jnp.minimum(u + 1, n_units - 1)], 0)
    unit_rows = acc.shape[0]

    def x_copy(k, tok):
        return _slab_copy(hp_ref, xg, gsem, tok, k, SLAB_PITCH, SLAB_PITCH, SLAB_ROWS)

    def y_copy(k, slot):
        return _slab_copy(yst, yt_ref, ssem, k, slot, SLAB_PITCH, SLAB_PITCH, SLAB_PITCH)

    def issue_gather(table_ref, count):
        def body(k, c):
            x_copy(k, table_ref[0, 0, k]).start()
            return c
        lax.fori_loop(0, count, body, 0)

    def drain_scatter():
        def body(k, c):
            y_copy(0, 0).wait()
            return c
        lax.fori_loop(0, cnt[0], body, 0)
        cnt[0] = 0

    @pl.when((u == 0) & (f == 0))
    def _():
        xg[...] = jnp.zeros(xg.shape, xg.dtype)
        yst[...] = jnp.zeros(yst.shape, yst.dtype)
        cnt[0] = 0
        issue_gather(tab_ref, n)

    @pl.when((f == 0) & (n > 0))
    def _():
        def drain(k, c):
            x_copy(0, 0).wait()
            return c
        lax.fori_loop(0, n, drain, 0)
        xb[...] = _unpack_rows(xg, 0, unit_rows)
        acc[...] = jnp.zeros(acc.shape, acc.dtype)

    @pl.when((f == 1) & (n_next > 0))
    def _():
        issue_gather(tabn_ref, n_next)

    @pl.when(n > 0)
    def _():
        wg = wg_ref[0].astype(BF16)
        wu = wu_ref[0].astype(BF16)
        wd = wd_ref[0].astype(BF16)
        d = acc.shape[1]
        col_chunk = min(1024, d)

        def sub_block(r0, rows):
            x = xb[r0:r0 + rows, :].astype(BF16)
            gt = _dot(x, wg)
            up = _dot(x, wu)
            a = (gt * (1.0 / (1.0 + jnp.exp(-gt))) * up).astype(BF16)
            for c0 in range(0, d, col_chunk):
                acc[r0:r0 + rows, c0:c0 + col_chunk] += _dot(a, wd[:, c0:c0 + col_chunk])

        sub_block(0, MOE_SUB_ROWS)

        @pl.when(n > MOE_SUB_ROWS)
        def _():
            sub_block(MOE_SUB_ROWS, unit_rows - MOE_SUB_ROWS)

    @pl.when((f == n_f - 1) & (n > 0))
    def _():
        drain_scatter()
        rw = rw_ref[0]
        for s in range(SLAB_ROWS):
            yst[pl.ds(s, unit_rows, stride=SLAB_PITCH), :] = acc[:, s * V7X_LANES:(s + 1) * V7X_LANES] * rw

        def scatter(k, c):
            y_copy(k, tab_ref[0, 0, unit_rows + k]).start()
            return c
        lax.fori_loop(0, n, scatter, 0)
        cnt[0] = n

    @pl.when((u == n_units - 1) & (f == n_f - 1))
    def _():
        drain_scatter()


def _moe_experts(hp, tab, row_w, unit_e, unit_n, w_gate, w_up, w_down, t):
    n_units = unit_e.shape[0]
    r = MOE_UNIT_ROWS
    _, d, d_exp = w_gate.shape
    assert d == SLAB_ROWS * V7X_LANES
    tf = min(256, d_exp)
    nf = d_exp // tf
    assert nf >= 2

    def wchunk(u, f, ue, un):
        return jnp.where(un[u] > 0, f, nf - 1)

    return pl.pallas_call(
        _moe_kernel,
        grid_spec=pltpu.PrefetchScalarGridSpec(
            num_scalar_prefetch=2,
            grid=(n_units, nf),
            in_specs=[
                pl.BlockSpec((1, 1, 2 * r), lambda u, f, ue, un: (u, 0, 0), memory_space=pltpu.SMEM),
                pl.BlockSpec((1, 1, 2 * r), lambda u, f, ue, un: (jnp.minimum(u + 1, n_units - 1), 0, 0),
                             memory_space=pltpu.SMEM),
                pl.BlockSpec(memory_space=pl.ANY),
                pl.BlockSpec((1, d, tf), lambda u, f, ue, un: (ue[u], 0, wchunk(u, f, ue, un))),
                pl.BlockSpec((1, d, tf), lambda u, f, ue, un: (ue[u], 0, wchunk(u, f, ue, un))),
                pl.BlockSpec((1, tf, d), lambda u, f, ue, un: (ue[u], wchunk(u, f, ue, un), 0)),
                pl.BlockSpec((1, r, 1), lambda u, f, ue, un: (u, 0, 0)),
            ],
            out_specs=pl.BlockSpec(memory_space=pl.ANY),
            scratch_shapes=[
                pltpu.VMEM((r * SLAB_PITCH, V7X_LANES), F32),
                pltpu.VMEM((r, d), F32),
                pltpu.VMEM((r, d), F32),
                pltpu.VMEM((r * SLAB_PITCH, V7X_LANES), F32),
                pltpu.SemaphoreType.DMA(()),
                pltpu.SemaphoreType.DMA(()),
                pltpu.SMEM((1,), jnp.int32),
            ],
        ),
        out_shape=jax.ShapeDtypeStruct((EXPERT_TOPK * t * SLAB_PITCH, V7X_LANES), F32),
        compiler_params=_cparams(("arbitrary", "arbitrary"), VMEM_BIG),
    )(unit_e, unit_n, tab, tab, hp, w_gate, w_up, w_down, row_w)


def _final_kernel(x_ref, y0_ref, y1_ref, g_ref, o_ref):
    tq = x_ref.shape[0]
    parts = []
    for s in range(SLAB_ROWS):
        parts.append(y0_ref[pl.ds(s, tq, stride=SLAB_PITCH), :] + y1_ref[pl.ds(s, tq, stride=SLAB_PITCH), :])
    x = x_ref[...] + jnp.concatenate(parts, axis=1)
    ms = jnp.mean(x * x, axis=-1, keepdims=True)
    o_ref[...] = x * lax.rsqrt(ms + EPS) * g_ref[...]


def _final(x2, yt, g_final):
    t, d = x2.shape
    tq = min(128, t)
    nt = t // tq
    return pl.pallas_call(
        _final_kernel,
        grid=(nt,),
        in_specs=[
            pl.BlockSpec((tq, d), lambda i: (i, 0)),
            pl.BlockSpec((tq * SLAB_PITCH, V7X_LANES), lambda i: (i, 0)),
            pl.BlockSpec((tq * SLAB_PITCH, V7X_LANES), lambda i: (nt + i, 0)),
            pl.BlockSpec((1, d), lambda i: (0, 0)),
        ],
        out_specs=pl.BlockSpec((tq, d), lambda i: (i, 0)),
        out_shape=jax.ShapeDtypeStruct((t, d), F32),
        compiler_params=_cparams(("parallel",), VMEM_MID),
    )(x2, yt, yt, g_final.reshape(1, d))


def _alibi_slopes(n):
    return jnp.exp2(-8.0 * jnp.arange(1, n + 1, dtype=F32) / n)


def _hybrid_mixer(x, norm_g, w_in, w_out, pe_k, w1_k, w2_k, pe_v, w1_v, w2_v):
    t, d = x.shape
    assert t % (CMP_STRIDE * 8) == 0 and t // SEL_BLOCK <= V7X_LANES
    h = _rmsnorm(x, norm_g)

    g0 = NSA_Q_W + 6 * NSA_KV_W
    assert g0 == NSA_BLKS * HEAD_DIM
    nsa_scale = np.ones((g0,), np.float32)
    nsa_scale[QN_BLK * HEAD_DIM:KC_BLK * HEAD_DIM] = SCALE * LOG2E
    tn = 512
    proj = _matmul_wcast(h, w_in, jnp.asarray(nsa_scale), 0, g0 // tn, tn, BF16)
    gates = _matmul_wcast(h, w_in, jnp.ones((V7X_LANES,), F32), g0, 1, V7X_LANES, F32)
    dil_w = 3 * DIL_HEADS * HEAD_DIM
    dil_scale = np.ones((dil_w,), np.float32)
    dil_scale[:DIL_HEADS * HEAD_DIM] = SCALE * LOG2E
    proj_dil = _matmul_wcast(h, w_in, jnp.asarray(dil_scale), g0 + NSA_GATE_W, dil_w // tn, tn, BF16)
    slopes = _alibi_slopes(N_MIX_HEADS) * LOG2E

    nc = t // CMP_STRIDE
    kv_raw = proj[:, KC_BLK * HEAD_DIM:KS_BLK * HEAD_DIM]
    a2 = kv_raw.reshape(nc, CMP_STRIDE, 2, NSA_KV_GROUPS, HEAD_DIM).transpose(2, 3, 0, 1, 4)
    a2 = a2.reshape(2, NSA_KV_GROUPS, nc, CMP_STRIDE * HEAD_DIM)
    w1 = jnp.stack([w1_k, w1_v]).astype(BF16)
    w2 = jnp.stack([w2_k, w2_v]).astype(BF16)
    pe = jnp.stack([pe_k, pe_v]).reshape(2, 1, CMP_LEN * HEAD_DIM)
    pe = jnp.broadcast_to(pe, (2, 8, CMP_LEN * HEAD_DIM)).astype(BF16)
    kvc = _compress(a2, w1, pe, w2)

    n_blk = t // SEL_BLOCK
    n_sel = min(SEL_TOPK, n_blk)
    ci = CMP_STRIDE * np.arange(nc)[:, None]
    sj = SEL_BLOCK * np.arange(V7X_LANES)[None, :]
    overlap = jnp.asarray(((ci < sj + SEL_BLOCK) & (ci + CMP_LEN > sj)).astype(np.float32), dtype=BF16)
    o_cmp, sel = _cmp_attention(proj, kvc, slopes, overlap, n_sel)

    o_sel = _sel_attention(proj, sel, slopes)

    o_win = _win_attention(proj, slopes)
    o_dil = _dil_attention(proj_dil, slopes)
    o_nsa = _combine(gates, o_cmp, o_sel, o_win)
    return _matmul([o_nsa] + list(o_dil), w_out.astype(BF16), F32, residual=x)


def _cross_block(x, mem, norm_cross, norm_mem, wq, wkv, wo):
    h = _rmsnorm(x, norm_cross)
    m = _rmsnorm(mem, norm_mem)
    q = _matmul(h, wq.astype(BF16), BF16)
    kv = _matmul(m, wkv.astype(BF16), BF16)
    return _cross_attention(q, kv, wo.astype(BF16), x)


def _moe_tables(eid, gate, t):
    r = MOE_UNIT_ROWS
    n_assign = t * EXPERT_TOPK
    e_flat = eid[:, :EXPERT_TOPK].reshape(n_assign)
    w_flat = gate[:, :EXPERT_TOPK].reshape(n_assign)
    order = jnp.argsort(e_flat).astype(jnp.int32)
    e_s = e_flat[order]
    experts = jnp.arange(N_EXPERTS, dtype=jnp.int32)
    counts = jnp.sum((e_flat[None, :] == experts[:, None]).astype(jnp.int32), axis=1)
    starts = jnp.cumsum(counts) - counts
    pcounts = (counts + r - 1) // r * r
    pends = jnp.cumsum(pcounts)
    pstarts = pends - pcounts
    row = (pstarts[e_s] + (jnp.arange(n_assign, dtype=jnp.int32) - starts[e_s])).astype(jnp.int32)
    n_units = n_assign // r + N_EXPERTS
    n_rows = n_units * r
    tok_s = order // EXPERT_TOPK
    slot_s = (order % EXPERT_TOPK) * t + tok_s
    fields = jnp.zeros((n_rows, 2), jnp.int32).at[row].set(jnp.stack([tok_s, slot_s], axis=1))
    row_tok, row_slot = fields[:, 0], fields[:, 1]
    row_w = jnp.zeros((n_rows,), F32).at[row].set(w_flat[order])
    unit_start = jnp.arange(n_units, dtype=jnp.int32) * r
    active = unit_start < pends[-1]
    e_raw = jnp.minimum(jnp.sum(pends[None, :] <= unit_start[:, None], axis=1), N_EXPERTS - 1).astype(jnp.int32)
    e_last = e_raw[pends[-1] // r - 1]
    unit_e = jnp.where(active, e_raw, e_last)
    unit_n = jnp.where(active, jnp.clip(counts[unit_e] - (unit_start - pstarts[unit_e]), 0, r), 0).astype(jnp.int32)
    tab = jnp.concatenate([row_tok.reshape(n_units, 1, r), row_slot.reshape(n_units, 1, r)], axis=2)
    return tab, row_w.reshape(n_units, r, 1), unit_e, unit_n


def _moe_block(x, norm_ffn, w_rg, b_rg, w_re, b_re, w_gate, w_up, w_down, norm_final):
    t, d = x.shape
    pad = V7X_LANES - N_GROUPS - N_EXPERTS
    w_r = jnp.pad(jnp.concatenate([w_rg, w_re], axis=1), ((0, 0), (0, pad)))
    b_r = jnp.pad(jnp.concatenate([b_rg, b_re]), (0, pad)).reshape(1, V7X_LANES)
    w_hi, w_mid, w_lo = _split3(w_r)
    hp, eid, gate = _router(x, norm_ffn, w_hi, w_mid, w_lo, b_r)
    tab, row_w, unit_e, unit_n = _moe_tables(eid, gate, t)
    yt = _moe_experts(hp, tab, row_w, unit_e, unit_n, w_gate, w_up, w_down, t)
    return _final(x, yt, norm_final)


def kernel(x, mem, norm_mix, w_in, w_out, cmp_pe_k, cmp_w1_k, cmp_w2_k, cmp_pe_v, cmp_w1_v, cmp_w2_v, norm_cross, norm_mem, w_q_cross, w_kv_cross, w_o_cross, norm_ffn, w_router_group, b_router_group, w_router_expert, b_router_expert, w_gate, w_up, w_down, norm_final):
    b, t, d = x.shape
    depth = norm_mix.shape[0]
    assert b == 1 and depth == 1
    xs = x.reshape(t, d)
    l = 0
    xs = _hybrid_mixer(xs, norm_mix[l], w_in[l], w_out[l], cmp_pe_k[l], cmp_w1_k[l], cmp_w2_k[l],
                       cmp_pe_v[l], cmp_w1_v[l], cmp_w2_v[l])
    xs = _cross_block(xs, mem.reshape(mem.shape[1], d), norm_cross[l], norm_mem[l], w_q_cross[l],
                      w_kv_cross[l], w_o_cross[l])
    out = _moe_block(xs, norm_ffn[l], w_router_group[l], b_router_group[l], w_router_expert[l],
                     b_router_expert[l], w_gate[l], w_up[l], w_down[l], norm_final)
    return out.reshape(b, t, d)
```

```python
import functools

import numpy as np
import jax
import jax.numpy as jnp
from jax import lax
from jax.experimental import pallas as pl
from jax.experimental.pallas import tpu as pltpu

F32 = jnp.float32
BF16 = jnp.bfloat16

HEAD_DIM = 128
DIL_CONFIGS = ((128, 1), (512, 4), (2048, 16))
DIL_HEADS_PER_CFG = 4
DIL_HEADS = DIL_HEADS_PER_CFG * len(DIL_CONFIGS)
N_MIX_HEADS = 32
NSA_HEADS = N_MIX_HEADS - DIL_HEADS
NSA_KV_GROUPS = 4
NSA_REP = NSA_HEADS // NSA_KV_GROUPS
CMP_LEN = 32
CMP_STRIDE = 16
SEL_BLOCK = 64
SEL_TOPK = 16
WIN = 512
CROSS_HEADS = 4
N_GROUPS = 8
EXPERTS_PER_GROUP = 8
N_EXPERTS = N_GROUPS * EXPERTS_PER_GROUP
EXPERT_TOPK = 2
EPS = 1e-6
NEG = -1e30
FORCE = 1e4
SCALE = HEAD_DIM ** -0.5
LOG2E = float(np.log2(np.e))

V7X_LANES = 128
V7X_SUBLANES = 8
V7X_VMEM_BYTES = 64 * 1024 * 1024
VMEM_BIG = 56 * 1024 * 1024
VMEM_MID = 40 * 1024 * 1024

QN_BLK = 0
KC_BLK = NSA_HEADS
VC_BLK = KC_BLK + NSA_KV_GROUPS
KS_BLK = VC_BLK + NSA_KV_GROUPS
VS_BLK = KS_BLK + NSA_KV_GROUPS
KW_BLK = VS_BLK + NSA_KV_GROUPS
VW_BLK = KW_BLK + NSA_KV_GROUPS
NSA_BLKS = VW_BLK + NSA_KV_GROUPS
QD_BLK = 0
KD_BLK = QD_BLK + DIL_HEADS
VD_BLK = KD_BLK + DIL_HEADS
NSA_Q_W = NSA_HEADS * HEAD_DIM
NSA_KV_W = NSA_KV_GROUPS * HEAD_DIM
NSA_GATE_W = NSA_HEADS * 3
DIL_CFG_W = DIL_HEADS_PER_CFG * HEAD_DIM

SLAB_ROWS = 32
SLAB_PITCH = 40
MOE_UNIT_ROWS = 320
MOE_SUB_ROWS = 256


def _cparams(sem, vmem=None):
    return pltpu.CompilerParams(dimension_semantics=sem, vmem_limit_bytes=vmem)


def _dot(a, b):
    return jnp.dot(a, b, preferred_element_type=F32)


def _dot_nt(a, b):
    return lax.dot_general(a, b, (((1,), (1,)), ((), ())), preferred_element_type=F32)


def _split3(a):
    hi = a.astype(BF16)
    r1 = a - hi.astype(F32)
    mid = r1.astype(BF16)
    lo = (r1 - mid.astype(F32)).astype(BF16)
    return hi, mid, lo


def _rms_kernel(x_ref, g_ref, o_ref):
    x = x_ref[...]
    ms = jnp.mean(x * x, axis=-1, keepdims=True)
    o_ref[...] = (x * lax.rsqrt(ms + EPS) * g_ref[...]).astype(o_ref.dtype)


def _rmsnorm(x, g, out_dtype=BF16):
    n, d = x.shape
    tr = min(256, n)
    return pl.pallas_call(
        _rms_kernel,
        grid=(n // tr,),
        in_specs=[pl.BlockSpec((tr, d), lambda i: (i, 0)), pl.BlockSpec((1, d), lambda i: (0, 0))],
        out_specs=pl.BlockSpec((tr, d), lambda i: (i, 0)),
        out_shape=jax.ShapeDtypeStruct((n, d), out_dtype),
        compiler_params=_cparams(("parallel",)),
    )(x, g.reshape(1, d))


def _mm_kernel(*refs, widths, has_res):
    n_x = len(widths)
    x_refs = refs[:n_x]
    w_ref = refs[n_x]
    o_ref = refs[-1]
    acc = None
    off = 0
    for x_ref, kw in zip(x_refs, widths):
        part = _dot(x_ref[...], w_ref[off:off + kw, :])
        acc = part if acc is None else acc + part
        off += kw
    if has_res:
        acc = refs[n_x + 1][...] + acc
    o_ref[...] = acc.astype(o_ref.dtype)


def _matmul(xs, w, out_dtype, residual=None):
    if not isinstance(xs, (list, tuple)):
        xs = [xs]
    m = xs[0].shape[0]
    widths = tuple(x.shape[1] for x in xs)
    k, n = w.shape
    assert sum(widths) == k
    tm = min(1024, m)
    tn = min(512, n)
    in_specs = [pl.BlockSpec((tm, kw), lambda i, j: (i, 0)) for kw in widths]
    in_specs.append(pl.BlockSpec((k, tn), lambda i, j: (0, j)))
    args = list(xs) + [w]
    if residual is not None:
        in_specs.append(pl.BlockSpec((tm, tn), lambda i, j: (i, j)))
        args.append(residual)
    return pl.pallas_call(
        functools.partial(_mm_kernel, widths=widths, has_res=residual is not None),
        grid=(m // tm, n // tn),
        in_specs=in_specs,
        out_specs=pl.BlockSpec((tm, tn), lambda i, j: (i, j)),
        out_shape=jax.ShapeDtypeStruct((m, n), out_dtype),
        compiler_params=_cparams(("parallel", "parallel"), VMEM_BIG),
    )(*args)


def _mm_wcast_kernel(x_ref, w_ref, *rest, lane_off):
    if lane_off:
        wt_ref, s_ref, o_ref, wb_sc = rest
    else:
        s_ref, o_ref, wb_sc = rest
    tn = o_ref.shape[1]

    @pl.when(pl.program_id(1) == 0)
    def _():
        w = w_ref[...]
        if lane_off:
            w = jnp.concatenate([w, wt_ref[...]], axis=1)[:, lane_off:lane_off + tn]
        wb_sc[...] = (w * s_ref[...]).astype(BF16)

    o_ref[...] = _dot(x_ref[...], wb_sc[...]).astype(o_ref.dtype)


def _matmul_wcast(x, w, col_scale, col0, n_col_blks, tn, out_dtype):
    m, k = x.shape
    tm = min(1024, m)
    lane_off = col0 % tn
    assert lane_off < V7X_LANES and tn % V7X_LANES == 0
    blk0 = col0 // tn
    in_specs = [pl.BlockSpec((tm, k), lambda j, i: (i, 0)),
                pl.BlockSpec((k, tn), lambda j, i: (0, blk0 + j))]
    args = [x, w]
    if lane_off:
        lanes_per_blk = tn // V7X_LANES
        in_specs.append(pl.BlockSpec((k, V7X_LANES), lambda j, i: (0, (blk0 + j + 1) * lanes_per_blk)))
        args.append(w)
    in_specs.append(pl.BlockSpec((1, tn), lambda j, i: (0, j)))
    args.append(col_scale.reshape(1, n_col_blks * tn))
    return pl.pallas_call(
        functools.partial(_mm_wcast_kernel, lane_off=lane_off),
        grid=(n_col_blks, m // tm),
        in_specs=in_specs,
        out_specs=pl.BlockSpec((tm, tn), lambda j, i: (i, j)),
        out_shape=jax.ShapeDtypeStruct((m, n_col_blks * tn), out_dtype),
        scratch_shapes=[pltpu.VMEM((k, tn), BF16)],
        compiler_params=_cparams(("parallel", "arbitrary"), VMEM_BIG),
    )(*args)


def _gelu_tanh(x):
    return 0.5 * x * (1.0 + jnp.tanh(np.sqrt(2.0 / np.pi) * (x + 0.044715 * (x * x * x))))


def _cmp_kernel(a_ref, w1_ref, pe_ref, w2_ref, o_ref):
    a = a_ref[0, 0]
    nc = a.shape[0]
    half = CMP_STRIDE * HEAD_DIM
    ha = _dot(a, w1_ref[0, :half, :])
    hb = _dot(a, w1_ref[0, half:, :])
    c = _dot(pe_ref[0], w1_ref[0])[0:1]
    pre = ha + pltpu.roll(hb, nc - 1, 0) + c
    hid = _gelu_tanh(pre)
    o_ref[0, 0] = _dot(hid.astype(BF16), w2_ref[0]).astype(o_ref.dtype)


def _compress(a2, w1, pe, w2):
    _, g, nc, ck = a2.shape
    hid = w1.shape[-1]
    return pl.pallas_call(
        _cmp_kernel,
        grid=(2, g),
        in_specs=[
            pl.BlockSpec((1, 1, nc, ck), lambda s, gg: (s, gg, 0, 0)),
            pl.BlockSpec((1, 2 * ck, hid), lambda s, gg: (s, 0, 0)),
            pl.BlockSpec((1, 8, 2 * ck), lambda s, gg: (s, 0, 0)),
            pl.BlockSpec((1, hid, HEAD_DIM), lambda s, gg: (s, 0, 0)),
        ],
        out_specs=pl.BlockSpec((1, 1, nc, HEAD_DIM), lambda s, gg: (s, gg, 0, 0)),
        out_shape=jax.ShapeDtypeStruct((2, g, nc, HEAD_DIM), BF16),
        compiler_params=_cparams(("parallel", "parallel"), VMEM_MID),
    )(a2, w1, pe, w2)


def _cmpattn_kernel(slopes_ref, q_ref, kc_ref, vc_ref, ov_ref, o_ref, sel_ref, *, n_sel, row0):
    g = pl.program_id(0)
    i = pl.program_id(1)
    tq = q_ref.shape[0]
    nc = kc_ref.shape[2]
    t = row0 + i * tq + lax.broadcasted_iota(jnp.int32, (tq, 1), 0)
    n = lax.broadcasted_iota(jnp.int32, (1, nc), 1)
    dist = (t - (CMP_STRIDE * n + CMP_LEN - 1)).astype(F32)
    madd = jnp.where(dist >= 0, 0.0, NEG)
    has_block = t >= CMP_LEN - 1
    kc = kc_ref[0, 0]
    vc = vc_ref[0, 0]
    psum = jnp.zeros((tq, nc), F32)
    for r in range(NSA_REP):
        slope = slopes_ref[g * NSA_REP + r]
        s = (_dot_nt(q_ref[:, r * HEAD_DIM:(r + 1) * HEAD_DIM], kc) - slope * dist) + madd
        m = jnp.max(s, axis=-1, keepdims=True)
        e = jnp.exp2(s - m)
        l = jnp.sum(e, axis=-1, keepdims=True)
        p = e * jnp.where(has_block, 1.0 / l, 0.0)
        o_ref[:, r * HEAD_DIM:(r + 1) * HEAD_DIM] = _dot(p.astype(BF16), vc).astype(o_ref.dtype)
        psum = psum + p
    ov = ov_ref[...]
    hi, mid, lo = _split3(psum)
    imp = _dot(hi, ov) + _dot(mid, ov) + _dot(lo, ov)
    blk = lax.broadcasted_iota(jnp.int32, (1, V7X_LANES), 1)
    cur = t // SEL_BLOCK
    causal = blk * SEL_BLOCK <= t
    forced = (blk == 0) | (blk == cur) | (blk == cur - 1)
    work = jnp.where(causal, jnp.where(forced, FORCE, imp), NEG)
    sel = jnp.zeros((tq, V7X_LANES), F32)
    for _ in range(n_sel):
        mx = jnp.max(work, axis=-1, keepdims=True)
        idx = jnp.min(jnp.where(work == mx, blk, V7X_LANES), axis=-1, keepdims=True)
        pick = blk == idx
        sel = jnp.where(pick, 1.0, sel)
        work = jnp.where(pick, -jnp.inf, work)
    sel_ref[0] = jnp.where(causal, sel, 0.0)


CMP_SEGMENT = 2048


def _cmp_attention(proj, kvc, slopes, overlap, n_sel):
    t = proj.shape[0]
    g = NSA_KV_GROUPS
    tq = min(512, t)
    qw = NSA_REP * HEAD_DIM
    seg = min(CMP_SEGMENT, t)
    assert t % seg == 0 and seg % tq == 0 and seg % (CMP_STRIDE * V7X_SUBLANES) == 0
    outs, sels = [], []
    for q in range(t // seg):
        nc = (q + 1) * seg // CMP_STRIDE
        tile0 = q * seg // tq
        o_q, sel_q = pl.pallas_call(
            functools.partial(_cmpattn_kernel, n_sel=n_sel, row0=q * seg),
            grid=(g, seg // tq),
            in_specs=[
                pl.BlockSpec(memory_space=pltpu.SMEM),
                pl.BlockSpec((tq, qw), lambda gg, i, tile0=tile0: (tile0 + i, gg)),
                pl.BlockSpec((1, 1, nc, HEAD_DIM), lambda gg, i: (0, gg, 0, 0)),
                pl.BlockSpec((1, 1, nc, HEAD_DIM), lambda gg, i: (1, gg, 0, 0)),
                pl.BlockSpec((nc, V7X_LANES), lambda gg, i: (0, 0)),
            ],
            out_specs=[
                pl.BlockSpec((tq, qw), lambda gg, i: (i, gg)),
                pl.BlockSpec((1, tq, V7X_LANES), lambda gg, i: (gg, i, 0)),
            ],
            out_shape=[
                jax.ShapeDtypeStruct((seg, NSA_Q_W), BF16),
                jax.ShapeDtypeStruct((g, seg, V7X_LANES), F32),
            ],
            compiler_params=_cparams(("parallel", "parallel"), VMEM_MID),
        )(slopes, proj, kvc, kvc, overlap)
        outs.append(o_q)
        sels.append(sel_q)
    return jnp.concatenate(outs, axis=0), jnp.concatenate(sels, axis=1)


def _flash_scratch(n_rows):
    return [pltpu.VMEM((n_rows, V7X_LANES), F32), pltpu.VMEM((n_rows, 2 * HEAD_DIM), F32)]


def _flash_init(m_sc, acc_sc):
    m_sc[...] = jnp.full(m_sc.shape, NEG, F32)
    acc_sc[...] = jnp.zeros(acc_sc.shape, F32)


def _with_ones(v):
    return jnp.concatenate([v, jnp.ones(v.shape, v.dtype)], axis=1)


def _tile_dist(q0, k0, rows, cols):
    return (q0 + lax.broadcasted_iota(jnp.int32, (rows, 1), 0)) - (k0 + lax.broadcasted_iota(jnp.int32, (1, cols), 1))


def _flash_update(rows, s, vx, m_sc, acc_sc):
    reps = s.shape[1] // V7X_LANES
    m_prev = m_sc[rows, :]
    m_new = jnp.maximum(m_prev, jnp.max(s, axis=-1, keepdims=True))
    alpha = jnp.exp2(m_prev - m_new)
    p = jnp.exp2(s - jnp.concatenate([m_new] * reps, axis=1))
    acc_sc[rows, :] = jnp.concatenate([alpha, alpha], axis=1) * acc_sc[rows, :] + _dot(p.astype(BF16), vx)
    m_sc[rows, :] = m_new


def _flash_out(rows, acc_sc):
    return acc_sc[rows, :HEAD_DIM] / acc_sc[rows, HEAD_DIM:]


def _flash_store(o_ref, acc_sc, n_rep, tq):
    for r in range(n_rep):
        o = _flash_out(slice(r * tq, (r + 1) * tq), acc_sc)
        o_ref[:, r * HEAD_DIM:(r + 1) * HEAD_DIM] = o.astype(o_ref.dtype)


MASK_BIG = 2.0 ** 60
SEL_TILE_BLOCKS = 8
SLOPE_LANE0 = SEL_TILE_BLOCKS


def _selattn_kernel(pi_ref, pj_ref, q_ref, k_ref, v_ref, sel_ref, shift_ref, kx_ref, sv_ref, o_ref, m_sc, acc_sc):
    step = pl.program_id(1)
    i = pi_ref[step]
    j = pj_ref[step]
    tq = q_ref.shape[0]
    tk = k_ref.shape[0]

    @pl.when(j == 0)
    def _():
        _flash_init(m_sc, acc_sc)

    def tile(diagonal):
        kk = jnp.concatenate([k_ref[...], kx_ref[0]], axis=1)
        vx = _with_ones(v_ref[...])
        bits = _dot(sel_ref[0].astype(BF16), shift_ref[0])
        lane = lax.broadcasted_iota(jnp.int32, (1, V7X_LANES), 1)
        mask_cols = jnp.where(lane < SEL_TILE_BLOCKS, (bits - 1.0) * MASK_BIG, 0.0)
        if diagonal:
            keep = _tile_dist(0, 0, tq, tk) >= 0
        for r in range(NSA_REP):
            extra = (mask_cols + sv_ref[0, r:r + 1, :].astype(F32)).astype(BF16)
            qq = jnp.concatenate([q_ref[:, r * HEAD_DIM:(r + 1) * HEAD_DIM], extra], axis=1)
            s = _dot_nt(qq, kk)
            if diagonal:
                s = jnp.where(keep, s, -MASK_BIG)
            _flash_update(slice(r * tq, (r + 1) * tq), s, vx, m_sc, acc_sc)

    @pl.when(j < i)
    def _():
        tile(False)

    @pl.when(j == i)
    def _():
        tile(True)
        _flash_store(o_ref, acc_sc, NSA_REP, tq)


def _fold_tables(slopes, n_tiles, tq):
    lane = np.arange(V7X_LANES)
    col = np.arange(tq)[:, None]
    kx = np.zeros((n_tiles, tq, V7X_LANES), np.float32)
    for dtile in range(n_tiles):
        rel = col - dtile * tq
        a = 64 * np.floor_divide(rel, 64)
        kx[dtile, :, :SEL_TILE_BLOCKS] = (col // SEL_BLOCK == lane[None, :SEL_TILE_BLOCKS])
        kx[dtile, :, SLOPE_LANE0:SLOPE_LANE0 + 3] = a
        kx[dtile, :, SLOPE_LANE0 + 3:SLOPE_LANE0 + 6] = rel - a
    s3 = jnp.stack(_split3(slopes[:NSA_HEADS]), axis=1).astype(F32)
    sv = jnp.zeros((NSA_HEADS, V7X_LANES), F32)
    sv = sv.at[:, SLOPE_LANE0:SLOPE_LANE0 + 3].set(s3).at[:, SLOPE_LANE0 + 3:SLOPE_LANE0 + 6].set(s3)
    sv = sv.reshape(NSA_KV_GROUPS, NSA_REP, V7X_LANES)
    sv = jnp.pad(sv, ((0, 0), (0, V7X_SUBLANES - NSA_REP), (0, 0))).astype(BF16)
    return jnp.asarray(kx, BF16), sv


def _sel_attention(proj, sel, slopes):
    t = proj.shape[0]
    g = NSA_KV_GROUPS
    tq = min(512, t)
    assert tq == SEL_TILE_BLOCKS * SEL_BLOCK
    nq = t // tq
    qw = NSA_REP * HEAD_DIM
    pairs = [(i, j) for i in range(nq) for j in range(i + 1)]
    pi = jnp.asarray([p[0] for p in pairs], jnp.int32)
    pj = jnp.asarray([p[1] for p in pairs], jnp.int32)
    lane = np.arange(V7X_LANES)[None, :]
    blk = np.arange(V7X_LANES)[:, None]
    shift = np.stack([(blk == SEL_TILE_BLOCKS * j + lane) & (lane < SEL_TILE_BLOCKS)
                      for j in range(nq)]).astype(np.float32)
    kx, sv = _fold_tables(slopes, nq, tq)
    return pl.pallas_call(
        _selattn_kernel,
        grid_spec=pltpu.PrefetchScalarGridSpec(
            num_scalar_prefetch=2,
            grid=(g, len(pairs)),
            in_specs=[
                pl.BlockSpec((tq, qw), lambda gg, s, pi, pj: (pi[s], gg)),
                pl.BlockSpec((tq, HEAD_DIM), lambda gg, s, pi, pj: (pj[s], KS_BLK + gg)),
                pl.BlockSpec((tq, HEAD_DIM), lambda gg, s, pi, pj: (pj[s], VS_BLK + gg)),
                pl.BlockSpec((1, tq, V7X_LANES), lambda gg, s, pi, pj: (gg, pi[s], 0)),
                pl.BlockSpec((1, V7X_LANES, V7X_LANES), lambda gg, s, pi, pj: (pj[s], 0, 0)),
                pl.BlockSpec((1, tq, V7X_LANES), lambda gg, s, pi, pj: (pi[s] - pj[s], 0, 0)),
                pl.BlockSpec((1, V7X_SUBLANES, V7X_LANES), lambda gg, s, pi, pj: (gg, 0, 0)),
            ],
            out_specs=pl.BlockSpec((tq, qw), lambda gg, s, pi, pj: (pi[s], gg)),
            scratch_shapes=_flash_scratch(NSA_REP * tq),
        ),
        out_shape=jax.ShapeDtypeStruct((t, NSA_Q_W), BF16),
        compiler_params=_cparams(("parallel", "arbitrary"), VMEM_MID),
    )(pi, pj, proj, proj, proj, sel, jnp.asarray(shift, BF16), kx, sv)


def _win_kernel(q_ref, k_ref, v_ref, kx_ref, sv_ref, o_ref, m_sc, acc_sc):
    i = pl.program_id(1)
    j = pl.program_id(2)
    tq = q_ref.shape[0]

    @pl.when(j == 0)
    def _():
        _flash_init(m_sc, acc_sc)

    def tile(current):
        kk = jnp.concatenate([k_ref[...], kx_ref[0]], axis=1)
        vx = _with_ones(v_ref[...])
        rc = _tile_dist(0, 0, tq, tq)
        keep = (rc >= 0) if current else (rc <= WIN - 1 - tq)
        for r in range(NSA_REP):
            extra = jnp.broadcast_to(sv_ref[0, r:r + 1, :], (tq, V7X_LANES))
            qq = jnp.concatenate([q_ref[:, r * HEAD_DIM:(r + 1) * HEAD_DIM], extra], axis=1)
            s = jnp.where(keep, _dot_nt(qq, kk), -MASK_BIG)
            _flash_update(slice(r * tq, (r + 1) * tq), s, vx, m_sc, acc_sc)

    @pl.when((j == 0) & (i > 0))
    def _():
        tile(False)

    @pl.when(j == 1)
    def _():
        tile(True)
        _flash_store(o_ref, acc_sc, NSA_REP, tq)


def _win_attention(proj, slopes):
    t = proj.shape[0]
    tq = min(512, t)
    assert WIN - 1 <= tq
    qw = NSA_REP * HEAD_DIM
    kv_tile = lambda i, j: jnp.maximum(i - 1 + j, 0)
    kx, sv = _fold_tables(slopes, 2, tq)
    return pl.pallas_call(
        _win_kernel,
        grid=(NSA_KV_GROUPS, t // tq, 2),
        in_specs=[
            pl.BlockSpec((tq, qw), lambda g, i, j: (i, g)),
            pl.BlockSpec((tq, HEAD_DIM), lambda g, i, j: (kv_tile(i, j), KW_BLK + g)),
            pl.BlockSpec((tq, HEAD_DIM), lambda g, i, j: (kv_tile(i, j), VW_BLK + g)),
            pl.BlockSpec((1, tq, V7X_LANES), lambda g, i, j: (1 - j, 0, 0)),
            pl.BlockSpec((1, V7X_SUBLANES, V7X_LANES), lambda g, i, j: (g, 0, 0)),
        ],
        out_specs=pl.BlockSpec((tq, qw), lambda g, i, j: (i, g)),
        out_shape=jax.ShapeDtypeStruct((t, NSA_Q_W), BF16),
        scratch_shapes=_flash_scratch(NSA_REP * tq),
        compiler_params=_cparams(("parallel", "parallel", "arbitrary"), VMEM_MID),
    )(proj, proj, proj, kx, sv)


def _dil_tiles(tq):
    return tuple(-(-window // tq) + 1 for window, _ in DIL_CONFIGS)


def _dil_kernel(slopes_ref, *refs):
    n_cfg = len(DIL_CONFIGS)
    q_refs = refs[0:n_cfg]
    k_refs = refs[n_cfg:2 * n_cfg]
    v_refs = refs[2 * n_cfg:3 * n_cfg]
    o_refs = refs[3 * n_cfg:4 * n_cfg]
    m_sc, acc_sc = refs[4 * n_cfg:]
    h = pl.program_id(0)
    i = pl.program_id(1)
    j = pl.program_id(2)
    tq = q_refs[0].shape[0]
    tiles = _dil_tiles(tq)
    n_steps = max(tiles)
    kt = i - (n_steps - 1) + j

    @pl.when(j == 0)
    def _():
        _flash_init(m_sc, acc_sc)

    for c, (window, dil) in enumerate(DIL_CONFIGS):
        @pl.when((j >= n_steps - tiles[c]) & (kt >= 0))
        def _(c=c, window=window, dil=dil):
            dist = _tile_dist(i * tq, kt * tq, tq, tq)
            ok = (dist >= 0) & (dist <= window) & ((dist & (dil - 1)) == 0)
            slope = slopes_ref[NSA_HEADS + c * DIL_HEADS_PER_CFG + h]
            s = _dot_nt(q_refs[c][...], k_refs[c][...]) - slope * dist.astype(F32)
            s = jnp.where(ok, s, NEG)
            _flash_update(slice(c * tq, (c + 1) * tq), s, _with_ones(v_refs[c][...]), m_sc, acc_sc)

    @pl.when(j == n_steps - 1)
    def _():
        rows = [slice(c * tq, (c + 1) * tq) for c in range(n_cfg)]
        lses = [m_sc[rw, :] + jnp.log2(acc_sc[rw, HEAD_DIM:]) for rw in rows]
        mx = functools.reduce(jnp.maximum, lses)
        es = [jnp.exp2(x - mx) for x in lses]
        den = functools.reduce(lambda a, b: a + b, es)
        for c in range(n_cfg):
            o_refs[c][...] = (_flash_out(rows[c], acc_sc) * (es[c] / den)).astype(o_refs[c].dtype)


def _dil_attention(proj, slopes):
    t = proj.shape[0]
    tq = min(512, t)
    n_cfg = len(DIL_CONFIGS)
    for _, dil in DIL_CONFIGS:
        assert dil & (dil - 1) == 0
    tiles = _dil_tiles(tq)
    n_steps = max(tiles)

    def kv_spec(col0, c):
        def idx(h, i, j):
            step = jnp.maximum(j - (n_steps - tiles[c]), 0)
            return (jnp.maximum(i - (tiles[c] - 1) + step, 0), col0 + c * DIL_HEADS_PER_CFG + h)
        return pl.BlockSpec((tq, HEAD_DIM), idx)

    q_specs = [pl.BlockSpec((tq, HEAD_DIM), lambda h, i, j, c=c: (i, QD_BLK + c * DIL_HEADS_PER_CFG + h))
               for c in range(n_cfg)]
    return pl.pallas_call(
        _dil_kernel,
        grid=(DIL_HEADS_PER_CFG, t // tq, n_steps),
        in_specs=[pl.BlockSpec(memory_space=pltpu.SMEM)] + q_specs
        + [kv_spec(KD_BLK, c) for c in range(n_cfg)] + [kv_spec(VD_BLK, c) for c in range(n_cfg)],
        out_specs=[pl.BlockSpec((tq, HEAD_DIM), lambda h, i, j: (i, h)) for _ in range(n_cfg)],
        out_shape=[jax.ShapeDtypeStruct((t, DIL_CFG_W), BF16) for _ in range(n_cfg)],
        scratch_shapes=_flash_scratch(n_cfg * tq),
        compiler_params=_cparams(("parallel", "parallel", "arbitrary"), VMEM_MID),
    )(slopes, *([proj] * (3 * n_cfg)))


def _combine_kernel(gates_ref, ocmp_ref, osel_ref, owin_ref, o_ref):
    sg = 1.0 / (1.0 + jnp.exp(-gates_ref[...]))
    for h in range(NSA_HEADS):
        cs = slice(h * HEAD_DIM, (h + 1) * HEAD_DIM)
        o = (sg[:, 3 * h:3 * h + 1] * ocmp_ref[:, cs].astype(F32)
             + sg[:, 3 * h + 1:3 * h + 2] * osel_ref[:, cs].astype(F32)
             + sg[:, 3 * h + 2:3 * h + 3] * owin_ref[:, cs].astype(F32))
        o_ref[:, cs] = o.astype(o_ref.dtype)


def _combine(gates, o_cmp, o_sel, o_win):
    t = gates.shape[0]
    tr = min(256, t)
    row = lambda w: pl.BlockSpec((tr, w), lambda i: (i, 0))
    return pl.pallas_call(
        _combine_kernel,
        grid=(t // tr,),
        in_specs=[row(V7X_LANES), row(NSA_Q_W), row(NSA_Q_W), row(NSA_Q_W)],
        out_specs=row(NSA_Q_W),
        out_shape=jax.ShapeDtypeStruct((t, NSA_Q_W), BF16),
        compiler_params=_cparams(("parallel",), VMEM_MID),
    )(gates, o_cmp, o_sel, o_win)


def _cross_kernel(q_ref, kv_ref, wo_ref, x_ref, o_ref):
    cw = CROSS_HEADS * HEAD_DIM
    outs = []
    for h in range(CROSS_HEADS):
        q = q_ref[:, h * HEAD_DIM:(h + 1) * HEAD_DIM]
        k = kv_ref[:, h * HEAD_DIM:(h + 1) * HEAD_DIM]
        v = kv_ref[:, cw + h * HEAD_DIM:cw + (h + 1) * HEAD_DIM]
        s = _dot_nt(q, k) * SCALE
        m = jnp.max(s, axis=-1, keepdims=True)
        p = jnp.exp(s - m)
        l = jnp.sum(p, axis=-1, keepdims=True)
        outs.append((_dot(p.astype(BF16), v) / l).astype(BF16))
    o = jnp.concatenate(outs, axis=-1)
    o_ref[...] = x_ref[...] + _dot(o, wo_ref[...])


def _cross_attention(q, kv, wo, x):
    t, d = x.shape
    s_len = kv.shape[0]
    cw = CROSS_HEADS * HEAD_DIM
    tq = min(256, t)
    return pl.pallas_call(
        _cross_kernel,
        grid=(t // tq,),
        in_specs=[
            pl.BlockSpec((tq, cw), lambda i: (i, 0)),
            pl.BlockSpec((s_len, 2 * cw), lambda i: (0, 0)),
            pl.BlockSpec((cw, d), lambda i: (0, 0)),
            pl.BlockSpec((tq, d), lambda i: (i, 0)),
        ],
        out_specs=pl.BlockSpec((tq, d), lambda i: (i, 0)),
        out_shape=jax.ShapeDtypeStruct((t, d), F32),
        compiler_params=_cparams(("parallel",), VMEM_MID),
    )(q, kv, wo, x)


def _router_kernel(x_ref, g_ref, whi_ref, wmid_ref, wlo_ref, b_ref, h_ref, eid_ref, gate_ref):
    x = x_ref[...]
    tr, d = x.shape
    ms = jnp.mean(x * x, axis=-1, keepdims=True)
    h = x * lax.rsqrt(ms + EPS) * g_ref[...]
    h_ref[...] = jnp.zeros(h_ref.shape, h_ref.dtype)
    for s in range(SLAB_ROWS):
        h_ref[pl.ds(s, tr, stride=SLAB_PITCH), :] = h[:, s * V7X_LANES:(s + 1) * V7X_LANES]
    hh, hm, hl = _split3(h)
    whi, wmid, wlo = whi_ref[...], wmid_ref[...], wlo_ref[...]
    logits = (_dot(hh, whi) + (_dot(hh, wmid) + _dot(hm, whi))
              + (_dot(hh, wlo) + _dot(hm, wmid) + _dot(hl, whi))) + b_ref[...]
    lane = lax.broadcasted_iota(jnp.int32, (1, V7X_LANES), 1)
    ninf = -jnp.inf
    is_g = lane < N_GROUPS
    lg = jnp.where(is_g, logits, ninf)
    mg = jnp.max(lg, axis=-1, keepdims=True)
    grp = jnp.min(jnp.where(lg == mg, lane, V7X_LANES), axis=-1, keepdims=True)
    p_grp = 1.0 / jnp.sum(jnp.where(is_g, jnp.exp(lg - mg), 0.0), axis=-1, keepdims=True)
    lo_lane = N_GROUPS + grp * EXPERTS_PER_GROUP
    in_grp = (lane >= lo_lane) & (lane < lo_lane + EXPERTS_PER_GROUP)
    le = jnp.where(in_grp, logits, ninf)
    v1 = jnp.max(le, axis=-1, keepdims=True)
    i1 = jnp.min(jnp.where(le == v1, lane, V7X_LANES), axis=-1, keepdims=True)
    le2 = jnp.where(lane == i1, ninf, le)
    v2 = jnp.max(le2, axis=-1, keepdims=True)
    i2 = jnp.min(jnp.where(le2 == v2, lane, V7X_LANES), axis=-1, keepdims=True)
    e2 = jnp.exp(v2 - v1)
    den = 1.0 + e2
    eid_ref[...] = jnp.where(lane == 0, i1 - N_GROUPS, jnp.where(lane == 1, i2 - N_GROUPS, 0))
    gate_ref[...] = jnp.where(lane == 0, p_grp / den, jnp.where(lane == 1, p_grp * e2 / den, 0.0))


def _router(x, g, w_hi, w_mid, w_lo, b):
    t, d = x.shape
    assert d == SLAB_ROWS * V7X_LANES
    tr = min(256, t)
    full = lambda shp: pl.BlockSpec(shp, lambda i: (0, 0))
    return pl.pallas_call(
        _router_kernel,
        grid=(t // tr,),
        in_specs=[pl.BlockSpec((tr, d), lambda i: (i, 0)), full((1, d)),
                  full((d, V7X_LANES)), full((d, V7X_LANES)), full((d, V7X_LANES)), full((1, V7X_LANES))],
        out_specs=[pl.BlockSpec((tr * SLAB_PITCH, V7X_LANES), lambda i: (i, 0)),
                   pl.BlockSpec((tr, V7X_LANES), lambda i: (i, 0)),
                   pl.BlockSpec((tr, V7X_LANES), lambda i: (i, 0))],
        out_shape=[jax.ShapeDtypeStruct((t * SLAB_PITCH, V7X_LANES), F32),
                   jax.ShapeDtypeStruct((t, V7X_LANES), jnp.int32),
                   jax.ShapeDtypeStruct((t, V7X_LANES), F32)],
        compiler_params=_cparams(("parallel",), VMEM_MID),
    )(x, g.reshape(1, d), w_hi, w_mid, w_lo, b)


def _slab_copy(src_ref, dst_ref, sem, src_slab, dst_slab, src_pitch, dst_pitch, rows):
    src0 = pl.multiple_of(src_slab * src_pitch, V7X_SUBLANES)
    dst0 = pl.multiple_of(dst_slab * dst_pitch, V7X_SUBLANES)
    return pltpu.make_async_copy(src_ref.at[pl.ds(src0, rows)], dst_ref.at[pl.ds(dst0, rows)], sem)


def _moe_kernel(ue_ref, un_ref, tab_ref, tabn_ref, hp_ref, wg_ref, wu_ref, wd_ref, rw_ref, yt_ref,
                xg, xb, g_acc, u_acc, a_sc, yst, gsem, ssem, cnt):
    u = pl.program_id(0)
    f = pl.program_id(1)
    n_units = pl.num_programs(0)
    n_f = pl.num_programs(1)
    n_k = xb.shape[0]
    n = un_ref[u]
    n_next = jnp.where(u + 1 < n_units, un_ref[jnp.minimum(u + 1, n_units - 1)], 0)
    unit_rows = g_acc.shape[0]
    sub_blocks = ((0, MOE_SUB_ROWS, None), (MOE_SUB_ROWS, unit_rows - MOE_SUB_ROWS, MOE_SUB_ROWS))

    def for_sub_blocks(fn):
        for r0, rows, need in sub_blocks:
            if need is None:
                fn(r0, rows)
            else:
                pl.when(n > need)(functools.partial(fn, r0, rows))

    def x_copy(k, tok):
        return _slab_copy(hp_ref, xg, gsem, tok, k, SLAB_PITCH, SLAB_PITCH, SLAB_ROWS)

    def y_copy(k, slot):
        return _slab_copy(yst, yt_ref, ssem, k, slot, SLAB_PITCH, SLAB_PITCH, SLAB_PITCH)

    def issue_gather(table_ref, count):
        def body(k, c):
            x_copy(k, table_ref[0, 0, k]).start()
            return c
        lax.fori_loop(0, count, body, 0)

    def drain_scatter():
        def body(k, c):
            y_copy(0, 0).wait()
            return c
        lax.fori_loop(0, cnt[0], body, 0)
        cnt[0] = 0

    @pl.when((u == 0) & (f == 0))
    def _():
        xg[...] = jnp.zeros(xg.shape, xg.dtype)
        yst[...] = jnp.zeros(yst.shape, yst.dtype)
        cnt[0] = 0
        issue_gather(tab_ref, n)

    @pl.when((f == 0) & (n > 0))
    def _():
        def drain(k, c):
            x_copy(0, 0).wait()
            return c
        lax.fori_loop(0, n, drain, 0)
        kc = xb.shape[2]
        lanes_per_chunk = kc // V7X_LANES
        for c in range(n_k):
            parts = [xg[pl.ds(c * lanes_per_chunk + s, unit_rows, stride=SLAB_PITCH), :]
                     for s in range(lanes_per_chunk)]
            xb[c] = jnp.concatenate(parts, axis=1)
        g_acc[...] = jnp.zeros(g_acc.shape, g_acc.dtype)
        u_acc[...] = jnp.zeros(u_acc.shape, u_acc.dtype)

    @pl.when((f == 1) & (n_next > 0))
    def _():
        issue_gather(tabn_ref, n_next)

    @pl.when((n > 0) & (f < n_k))
    def _():
        wg = wg_ref[0].astype(BF16)
        wu = wu_ref[0].astype(BF16)

        def gate_up(r0, rows):
            x = xb[f, r0:r0 + rows, :].astype(BF16)
            g_acc[r0:r0 + rows, :] += _dot(x, wg)
            u_acc[r0:r0 + rows, :] += _dot(x, wu)

        for_sub_blocks(gate_up)

    @pl.when((n > 0) & (f == n_k - 1))
    def _():
        gt = g_acc[...]
        a_sc[...] = gt * (1.0 / (1.0 + jnp.exp(-gt))) * u_acc[...]
        drain_scatter()

    @pl.when((n > 0) & (f >= n_k))
    def _():
        wd = wd_ref[0].astype(BF16)
        tn = wd.shape[1]
        row0 = (f - n_k) * (tn // V7X_LANES)

        def down(r0, rows):
            y = _dot(a_sc[r0:r0 + rows, :].astype(BF16), wd) * rw_ref[0, r0:r0 + rows, :]
            for s in range(tn // V7X_LANES):
                yst[pl.ds(r0 * SLAB_PITCH + row0 + s, rows, stride=SLAB_PITCH), :] = \
                    y[:, s * V7X_LANES:(s + 1) * V7X_LANES]

        for_sub_blocks(down)

    @pl.when((f == n_f - 1) & (n > 0))
    def _():
        def scatter(k, c):
            y_copy(k, tab_ref[0, 0, unit_rows + k]).start()
            return c
        lax.fori_loop(0, n, scatter, 0)
        cnt[0] = n

    @pl.when((u == n_units - 1) & (f == n_f - 1))
    def _():
        drain_scatter()


def _moe_experts(hp, tab, row_w, unit_e, unit_n, w_gate, w_up, w_down, t):
    n_units = unit_e.shape[0]
    r = MOE_UNIT_ROWS
    _, d, d_exp = w_gate.shape
    assert d == SLAB_ROWS * V7X_LANES
    n_k = 4
    kc = d // n_k
    assert kc % V7X_LANES == 0

    def up_chunk(u, f, ue, un):
        return jnp.where(un[u] > 0, jnp.minimum(f, n_k - 1), n_k - 1)

    def down_chunk(u, f, ue, un):
        return jnp.where(un[u] > 0, jnp.maximum(f - n_k, 0), n_k - 1)

    return pl.pallas_call(
        _moe_kernel,
        grid_spec=pltpu.PrefetchScalarGridSpec(
            num_scalar_prefetch=2,
            grid=(n_units, 2 * n_k),
            in_specs=[
                pl.BlockSpec((1, 1, 2 * r), lambda u, f, ue, un: (u, 0, 0), memory_space=pltpu.SMEM),
                pl.BlockSpec((1, 1, 2 * r), lambda u, f, ue, un: (jnp.minimum(u + 1, n_units - 1), 0, 0),
                             memory_space=pltpu.SMEM),
                pl.BlockSpec(memory_space=pl.ANY),
                pl.BlockSpec((1, kc, d_exp), lambda u, f, ue, un: (ue[u], up_chunk(u, f, ue, un), 0)),
                pl.BlockSpec((1, kc, d_exp), lambda u, f, ue, un: (ue[u], up_chunk(u, f, ue, un), 0)),
                pl.BlockSpec((1, d_exp, kc), lambda u, f, ue, un: (ue[u], 0, down_chunk(u, f, ue, un))),
                pl.BlockSpec((1, r, 1), lambda u, f, ue, un: (u, 0, 0)),
            ],
            out_specs=pl.BlockSpec(memory_space=pl.ANY),
            scratch_shapes=[
                pltpu.VMEM((r * SLAB_PITCH, V7X_LANES), F32),
                pltpu.VMEM((n_k, r, kc), F32),
                pltpu.VMEM((r, d_exp), F32),
                pltpu.VMEM((r, d_exp), F32),
                pltpu.VMEM((r, d_exp), F32),
                pltpu.VMEM((r * SLAB_PITCH, V7X_LANES), F32),
                pltpu.SemaphoreType.DMA(()),
                pltpu.SemaphoreType.DMA(()),
                pltpu.SMEM((1,), jnp.int32),
            ],
        ),
        out_shape=jax.ShapeDtypeStruct((EXPERT_TOPK * t * SLAB_PITCH, V7X_LANES), F32),
        compiler_params=_cparams(("arbitrary", "arbitrary"), VMEM_BIG),
    )(unit_e, unit_n, tab, tab, hp, w_gate, w_up, w_down, row_w)


def _final_kernel(x_ref, y0_ref, y1_ref, g_ref, o_ref):
    tq = x_ref.shape[0]
    parts = []
    for s in range(SLAB_ROWS):
        parts.append(y0_ref[pl.ds(s, tq, stride=SLAB_PITCH), :] + y1_ref[pl.ds(s, tq, stride=SLAB_PITCH), :])
    x = x_ref[...] + jnp.concatenate(parts, axis=1)
    ms = jnp.mean(x * x, axis=-1, keepdims=True)
    o_ref[...] = x * lax.rsqrt(ms + EPS) * g_ref[...]


def _final(x2, yt, g_final):
    t, d = x2.shape
    tq = min(128, t)
    nt = t // tq
    return pl.pallas_call(
        _final_kernel,
        grid=(nt,),
        in_specs=[
            pl.BlockSpec((tq, d), lambda i: (i, 0)),
            pl.BlockSpec((tq * SLAB_PITCH, V7X_LANES), lambda i: (i, 0)),
            pl.BlockSpec((tq * SLAB_PITCH, V7X_LANES), lambda i: (nt + i, 0)),
            pl.BlockSpec((1, d), lambda i: (0, 0)),
        ],
        out_specs=pl.BlockSpec((tq, d), lambda i: (i, 0)),
        out_shape=jax.ShapeDtypeStruct((t, d), F32),
        compiler_params=_cparams(("parallel",), VMEM_MID),
    )(x2, yt, yt, g_final.reshape(1, d))


def _alibi_slopes(n):
    return jnp.exp2(-8.0 * jnp.arange(1, n + 1, dtype=F32) / n)


def _hybrid_mixer(x, norm_g, w_in, w_out, pe_k, w1_k, w2_k, pe_v, w1_v, w2_v):
    t, d = x.shape
    assert t % (CMP_STRIDE * 8) == 0 and t // SEL_BLOCK <= V7X_LANES
    h = _rmsnorm(x, norm_g)

    g0 = NSA_Q_W + 6 * NSA_KV_W
    assert g0 == NSA_BLKS * HEAD_DIM
    nsa_scale = np.ones((g0,), np.float32)
    nsa_scale[QN_BLK * HEAD_DIM:KC_BLK * HEAD_DIM] = SCALE * LOG2E
    tn = 512
    proj = _matmul_wcast(h, w_in, jnp.asarray(nsa_scale), 0, g0 // tn, tn, BF16)
    gates = _matmul_wcast(h, w_in, jnp.ones((V7X_LANES,), F32), g0, 1, V7X_LANES, F32)
    dil_w = 3 * DIL_HEADS * HEAD_DIM
    dil_scale = np.ones((dil_w,), np.float32)
    dil_scale[:DIL_HEADS * HEAD_DIM] = SCALE * LOG2E
    proj_dil = _matmul_wcast(h, w_in, jnp.asarray(dil_scale), g0 + NSA_GATE_W, dil_w // tn, tn, BF16)
    slopes = _alibi_slopes(N_MIX_HEADS) * LOG2E

    nc = t // CMP_STRIDE
    kv_raw = proj[:, KC_BLK * HEAD_DIM:KS_BLK * HEAD_DIM]
    a2 = kv_raw.reshape(nc, CMP_STRIDE, 2, NSA_KV_GROUPS, HEAD_DIM).transpose(2, 3, 0, 1, 4)
    a2 = a2.reshape(2, NSA_KV_GROUPS, nc, CMP_STRIDE * HEAD_DIM)
    w1 = jnp.stack([w1_k, w1_v]).astype(BF16)
    w2 = jnp.stack([w2_k, w2_v]).astype(BF16)
    pe = jnp.stack([pe_k, pe_v]).reshape(2, 1, CMP_LEN * HEAD_DIM)
    pe = jnp.broadcast_to(pe, (2, 8, CMP_LEN * HEAD_DIM)).astype(BF16)
    kvc = _compress(a2, w1, pe, w2)

    n_blk = t // SEL_BLOCK
    n_sel = min(SEL_TOPK, n_blk)
    ci = CMP_STRIDE * np.arange(nc)[:, None]
    sj = SEL_BLOCK * np.arange(V7X_LANES)[None, :]
    overlap = jnp.asarray(((ci < sj + SEL_BLOCK) & (ci + CMP_LEN > sj)).astype(np.float32), dtype=BF16)
    o_cmp, sel = _cmp_attention(proj, kvc, slopes, overlap, n_sel)

    o_sel = _sel_attention(proj, sel, slopes)

    o_win = _win_attention(proj, slopes)
    o_dil = _dil_attention(proj_dil, slopes)
    o_nsa = _combine(gates, o_cmp, o_sel, o_win)
    return _matmul([o_nsa] + list(o_dil), w_out.astype(BF16), F32, residual=x)


def _cross_block(x, mem, norm_cross, norm_mem, wq, wkv, wo):
    h = _rmsnorm(x, norm_cross)
    m = _rmsnorm(mem, norm_mem)
    q = _matmul(h, wq.astype(BF16), BF16)
    kv = _matmul(m, wkv.astype(BF16), BF16)
    return _cross_attention(q, kv, wo.astype(BF16), x)


def _moe_tables(eid, gate, t):
    r = MOE_UNIT_ROWS
    n_assign = t * EXPERT_TOPK
    e_flat = eid[:, :EXPERT_TOPK].reshape(n_assign)
    w_flat = gate[:, :EXPERT_TOPK].reshape(n_assign)
    order = jnp.argsort(e_flat).astype(jnp.int32)
    e_s = e_flat[order]
    experts = jnp.arange(N_EXPERTS, dtype=jnp.int32)
    counts = jnp.sum((e_flat[None, :] == experts[:, None]).astype(jnp.int32), axis=1)
    starts = jnp.cumsum(counts) - counts
    pcounts = (counts + r - 1) // r * r
    pends = jnp.cumsum(pcounts)
    pstarts = pends - pcounts
    row = (pstarts[e_s] + (jnp.arange(n_assign, dtype=jnp.int32) - starts[e_s])).astype(jnp.int32)
    n_units = n_assign // r + N_EXPERTS
    n_rows = n_units * r
    tok_s = order // EXPERT_TOPK
    slot_s = (order % EXPERT_TOPK) * t + tok_s
    fields = jnp.zeros((n_rows, 2), jnp.int32).at[row].set(jnp.stack([tok_s, slot_s], axis=1))
    row_tok, row_slot = fields[:, 0], fields[:, 1]
    row_w = jnp.zeros((n_rows,), F32).at[row].set(w_flat[order])
    unit_start = jnp.arange(n_units, dtype=jnp.int32) * r
    active = unit_start < pends[-1]
    e_raw = jnp.minimum(jnp.sum(pends[None, :] <= unit_start[:, None], axis=1), N_EXPERTS - 1).astype(jnp.int32)
    e_last = e_raw[pends[-1] // r - 1]
    unit_e = jnp.where(active, e_raw, e_last)
    unit_n = jnp.where(active, jnp.clip(counts[unit_e] - (unit_start - pstarts[unit_e]), 0, r), 0).astype(jnp.int32)
    tab = jnp.concatenate([row_tok.reshape(n_units, 1, r), row_slot.reshape(n_units, 1, r)], axis=2)
    return tab, row_w.reshape(n_units, r, 1), unit_e, unit_n


def _moe_block(x, norm_ffn, w_rg, b_rg, w_re, b_re, w_gate, w_up, w_down, norm_final):
    t, d = x.shape
    pad = V7X_LANES - N_GROUPS - N_EXPERTS
    w_r = jnp.pad(jnp.concatenate([w_rg, w_re], axis=1), ((0, 0), (0, pad)))
    b_r = jnp.pad(jnp.concatenate([b_rg, b_re]), (0, pad)).reshape(1, V7X_LANES)
    w_hi, w_mid, w_lo = _split3(w_r)
    hp, eid, gate = _router(x, norm_ffn, w_hi, w_mid, w_lo, b_r)
    tab, row_w, unit_e, unit_n = _moe_tables(eid, gate, t)
    yt = _moe_experts(hp, tab, row_w, unit_e, unit_n, w_gate, w_up, w_down, t)
    return _final(x, yt, norm_final)


def kernel(x, mem, norm_mix, w_in, w_out, cmp_pe_k, cmp_w1_k, cmp_w2_k, cmp_pe_v, cmp_w1_v, cmp_w2_v, norm_cross, norm_mem, w_q_cross, w_kv_cross, w_o_cross, norm_ffn, w_router_group, b_router_group, w_router_expert, b_router_expert, w_gate, w_up, w_down, norm_final):
    b, t, d = x.shape
    depth = norm_mix.shape[0]
    assert b == 1 and depth == 1
    xs = x.reshape(t, d)
    l = 0
    xs = _hybrid_mixer(xs, norm_mix[l], w_in[l], w_out[l], cmp_pe_k[l], cmp_w1_k[l], cmp_w2_k[l],
                       cmp_pe_v[l], cmp_w1_v[l], cmp_w2_v[l])
    xs = _cross_block(xs, mem.reshape(mem.shape[1], d), norm_cross[l], norm_mem[l], w_q_cross[l],
                      w_kv_cross[l], w_o_cross[l])
    out = _moe_block(xs, norm_ffn[l], w_router_group[l], b_router_group[l], w_router_expert[l],
                     b_router_expert[l], w_gate[l], w_up[l], w_down[l], norm_final)
    return out.reshape(b, t, d)
```

```python
import functools

import numpy as np
import jax
import jax.numpy as jnp
from jax import lax
from jax.experimental import pallas as pl
from jax.experimental.pallas import tpu as pltpu

F32 = jnp.float32
BF16 = jnp.bfloat16

HEAD_DIM = 128
DIL_CONFIGS = ((128, 1), (512, 4), (2048, 16))
DIL_HEADS_PER_CFG = 4
DIL_HEADS = DIL_HEADS_PER_CFG * len(DIL_CONFIGS)
N_MIX_HEADS = 32
NSA_HEADS = N_MIX_HEADS - DIL_HEADS
NSA_KV_GROUPS = 4
NSA_REP = NSA_HEADS // NSA_KV_GROUPS
CMP_LEN = 32
CMP_STRIDE = 16
SEL_BLOCK = 64
SEL_TOPK = 16
WIN = 512
CROSS_HEADS = 4
N_GROUPS = 8
EXPERTS_PER_GROUP = 8
N_EXPERTS = N_GROUPS * EXPERTS_PER_GROUP
EXPERT_TOPK = 2
EPS = 1e-6
NEG = -1e30
FORCE = 1e4
SCALE = HEAD_DIM ** -0.5
LOG2E = float(np.log2(np.e))

V7X_LANES = 128
V7X_SUBLANES = 8
V7X_VMEM_BYTES = 64 * 1024 * 1024
VMEM_BIG = 56 * 1024 * 1024
VMEM_MID = 40 * 1024 * 1024

QN_BLK = 0
KC_BLK = NSA_HEADS
VC_BLK = KC_BLK + NSA_KV_GROUPS
KS_BLK = VC_BLK + NSA_KV_GROUPS
VS_BLK = KS_BLK + NSA_KV_GROUPS
KW_BLK = VS_BLK + NSA_KV_GROUPS
VW_BLK = KW_BLK + NSA_KV_GROUPS
NSA_BLKS = VW_BLK + NSA_KV_GROUPS
QD_BLK = 0
KD_BLK = QD_BLK + DIL_HEADS
VD_BLK = KD_BLK + DIL_HEADS
NSA_Q_W = NSA_HEADS * HEAD_DIM
NSA_KV_W = NSA_KV_GROUPS * HEAD_DIM
NSA_GATE_W = NSA_HEADS * 3
DIL_CFG_W = DIL_HEADS_PER_CFG * HEAD_DIM

SLAB_ROWS = 32
SLAB_PITCH = 40
MOE_UNIT_ROWS = 320
MOE_SUB_ROWS = 256


def _cparams(sem, vmem=None):
    return pltpu.CompilerParams(dimension_semantics=sem, vmem_limit_bytes=vmem)


def _dot(a, b):
    return jnp.dot(a, b, preferred_element_type=F32)


def _dot_nt(a, b):
    return lax.dot_general(a, b, (((1,), (1,)), ((), ())), preferred_element_type=F32)


def _split3(a):
    hi = a.astype(BF16)
    r1 = a - hi.astype(F32)
    mid = r1.astype(BF16)
    lo = (r1 - mid.astype(F32)).astype(BF16)
    return hi, mid, lo


def _rms_kernel(x_ref, g_ref, o_ref):
    x = x_ref[...]
    ms = jnp.mean(x * x, axis=-1, keepdims=True)
    o_ref[...] = (x * lax.rsqrt(ms + EPS) * g_ref[...]).astype(o_ref.dtype)


def _rmsnorm(x, g, out_dtype=BF16):
    n, d = x.shape
    tr = min(256, n)
    return pl.pallas_call(
        _rms_kernel,
        grid=(n // tr,),
        in_specs=[pl.BlockSpec((tr, d), lambda i: (i, 0)), pl.BlockSpec((1, d), lambda i: (0, 0))],
        out_specs=pl.BlockSpec((tr, d), lambda i: (i, 0)),
        out_shape=jax.ShapeDtypeStruct((n, d), out_dtype),
        compiler_params=_cparams(("parallel",)),
    )(x, g.reshape(1, d))


def _mm_kernel(*refs, widths, has_res):
    n_x = len(widths)
    x_refs = refs[:n_x]
    w_ref = refs[n_x]
    o_ref = refs[-1]
    acc = None
    off = 0
    for x_ref, kw in zip(x_refs, widths):
        part = _dot(x_ref[...], w_ref[off:off + kw, :])
        acc = part if acc is None else acc + part
        off += kw
    if has_res:
        acc = refs[n_x + 1][...] + acc
    o_ref[...] = acc.astype(o_ref.dtype)


def _matmul(xs, w, out_dtype, residual=None):
    if not isinstance(xs, (list, tuple)):
        xs = [xs]
    m = xs[0].shape[0]
    widths = tuple(x.shape[1] for x in xs)
    k, n = w.shape
    assert sum(widths) == k
    tm = min(1024, m)
    tn = min(512, n)
    in_specs = [pl.BlockSpec((tm, kw), lambda i, j: (i, 0)) for kw in widths]
    in_specs.append(pl.BlockSpec((k, tn), lambda i, j: (0, j)))
    args = list(xs) + [w]
    if residual is not None:
        in_specs.append(pl.BlockSpec((tm, tn), lambda i, j: (i, j)))
        args.append(residual)
    return pl.pallas_call(
        functools.partial(_mm_kernel, widths=widths, has_res=residual is not None),
        grid=(m // tm, n // tn),
        in_specs=in_specs,
        out_specs=pl.BlockSpec((tm, tn), lambda i, j: (i, j)),
        out_shape=jax.ShapeDtypeStruct((m, n), out_dtype),
        compiler_params=_cparams(("parallel", "parallel"), VMEM_BIG),
    )(*args)


def _mm_wcast_kernel(x_ref, w_ref, *rest, lane_off):
    if lane_off:
        wt_ref, s_ref, o_ref, wb_sc = rest
    else:
        s_ref, o_ref, wb_sc = rest
    tn = o_ref.shape[1]

    @pl.when(pl.program_id(1) == 0)
    def _():
        w = w_ref[...]
        if lane_off:
            w = jnp.concatenate([w, wt_ref[...]], axis=1)[:, lane_off:lane_off + tn]
        wb_sc[...] = (w * s_ref[...]).astype(BF16)

    o_ref[...] = _dot(x_ref[...], wb_sc[...]).astype(o_ref.dtype)


def _matmul_wcast(x, w, col_scale, col0, n_col_blks, tn, out_dtype):
    m, k = x.shape
    tm = min(1024, m)
    lane_off = col0 % tn
    assert lane_off < V7X_LANES and tn % V7X_LANES == 0
    blk0 = col0 // tn
    in_specs = [pl.BlockSpec((tm, k), lambda j, i: (i, 0)),
                pl.BlockSpec((k, tn), lambda j, i: (0, blk0 + j))]
    args = [x, w]
    if lane_off:
        lanes_per_blk = tn // V7X_LANES
        in_specs.append(pl.BlockSpec((k, V7X_LANES), lambda j, i: (0, (blk0 + j + 1) * lanes_per_blk)))
        args.append(w)
    in_specs.append(pl.BlockSpec((1, tn), lambda j, i: (0, j)))
    args.append(col_scale.reshape(1, n_col_blks * tn))
    return pl.pallas_call(
        functools.partial(_mm_wcast_kernel, lane_off=lane_off),
        grid=(n_col_blks, m // tm),
        in_specs=in_specs,
        out_specs=pl.BlockSpec((tm, tn), lambda j, i: (i, j)),
        out_shape=jax.ShapeDtypeStruct((m, n_col_blks * tn), out_dtype),
        scratch_shapes=[pltpu.VMEM((k, tn), BF16)],
        compiler_params=_cparams(("parallel", "arbitrary"), VMEM_BIG),
    )(*args)


def _gelu_tanh(x):
    return 0.5 * x * (1.0 + jnp.tanh(np.sqrt(2.0 / np.pi) * (x + 0.044715 * (x * x * x))))


def _cmp_kernel(a_ref, w1_ref, pe_ref, w2_ref, o_ref):
    a = a_ref[0, 0]
    nc = a.shape[0]
    half = CMP_STRIDE * HEAD_DIM
    ha = _dot(a, w1_ref[0, :half, :])
    hb = _dot(a, w1_ref[0, half:, :])
    c = _dot(pe_ref[0], w1_ref[0])[0:1]
    pre = ha + pltpu.roll(hb, nc - 1, 0) + c
    hid = _gelu_tanh(pre)
    o_ref[0, 0] = _dot(hid.astype(BF16), w2_ref[0]).astype(o_ref.dtype)


def _compress(a2, w1, pe, w2):
    _, g, nc, ck = a2.shape
    hid = w1.shape[-1]
    return pl.pallas_call(
        _cmp_kernel,
        grid=(2, g),
        in_specs=[
            pl.BlockSpec((1, 1, nc, ck), lambda s, gg: (s, gg, 0, 0)),
            pl.BlockSpec((1, 2 * ck, hid), lambda s, gg: (s, 0, 0)),
            pl.BlockSpec((1, 8, 2 * ck), lambda s, gg: (s, 0, 0)),
            pl.BlockSpec((1, hid, HEAD_DIM), lambda s, gg: (s, 0, 0)),
        ],
        out_specs=pl.BlockSpec((1, 1, nc, HEAD_DIM), lambda s, gg: (s, gg, 0, 0)),
        out_shape=jax.ShapeDtypeStruct((2, g, nc, HEAD_DIM), BF16),
        compiler_params=_cparams(("parallel", "parallel"), VMEM_MID),
    )(a2, w1, pe, w2)


def _cmpattn_kernel(slopes_ref, q_ref, kc_ref, vc_ref, ov_ref, o_ref, sel_ref, *, n_sel, row0):
    g = pl.program_id(0)
    i = pl.program_id(1)
    tq = q_ref.shape[0]
    nc = kc_ref.shape[2]
    t = row0 + i * tq + lax.broadcasted_iota(jnp.int32, (tq, 1), 0)
    n = lax.broadcasted_iota(jnp.int32, (1, nc), 1)
    dist = (t - (CMP_STRIDE * n + CMP_LEN - 1)).astype(F32)
    madd = jnp.where(dist >= 0, 0.0, NEG)
    has_block = t >= CMP_LEN - 1
    kc = kc_ref[0, 0]
    vc = vc_ref[0, 0]
    psum = jnp.zeros((tq, nc), F32)
    for r in range(NSA_REP):
        slope = slopes_ref[g * NSA_REP + r]
        s = (_dot_nt(q_ref[:, r * HEAD_DIM:(r + 1) * HEAD_DIM], kc) - slope * dist) + madd
        m = jnp.max(s, axis=-1, keepdims=True)
        e = jnp.exp2(s - m)
        l = jnp.sum(e, axis=-1, keepdims=True)
        p = e * jnp.where(has_block, 1.0 / l, 0.0)
        o_ref[:, r * HEAD_DIM:(r + 1) * HEAD_DIM] = _dot(p.astype(BF16), vc).astype(o_ref.dtype)
        psum = psum + p
    ov = ov_ref[...]
    hi, mid, lo = _split3(psum)
    imp = _dot(hi, ov) + _dot(mid, ov) + _dot(lo, ov)
    blk = lax.broadcasted_iota(jnp.int32, (1, V7X_LANES), 1)
    cur = t // SEL_BLOCK
    causal = blk * SEL_BLOCK <= t
    forced = (blk == 0) | (blk == cur) | (blk == cur - 1)
    work = jnp.where(causal, jnp.where(forced, FORCE, imp), NEG)
    sel = jnp.zeros((tq, V7X_LANES), F32)
    for _ in range(n_sel):
        mx = jnp.max(work, axis=-1, keepdims=True)
        idx = jnp.min(jnp.where(work == mx, blk, V7X_LANES), axis=-1, keepdims=True)
        pick = blk == idx
        sel = jnp.where(pick, 1.0, sel)
        work = jnp.where(pick, -jnp.inf, work)
    sel_ref[0] = jnp.where(causal, sel, 0.0)


CMP_SEGMENT = 2048


def _cmp_attention(proj, kvc, slopes, overlap, n_sel):
    t = proj.shape[0]
    g = NSA_KV_GROUPS
    tq = min(1024, t)
    qw = NSA_REP * HEAD_DIM
    seg = min(CMP_SEGMENT, t)
    assert t % seg == 0 and seg % tq == 0 and seg % (CMP_STRIDE * V7X_SUBLANES) == 0
    outs, sels = [], []
    for q in range(t // seg):
        nc = (q + 1) * seg // CMP_STRIDE
        tile0 = q * seg // tq
        o_q, sel_q = pl.pallas_call(
            functools.partial(_cmpattn_kernel, n_sel=n_sel, row0=q * seg),
            grid=(g, seg // tq),
            in_specs=[
                pl.BlockSpec(memory_space=pltpu.SMEM),
                pl.BlockSpec((tq, qw), lambda gg, i, tile0=tile0: (tile0 + i, gg)),
                pl.BlockSpec((1, 1, nc, HEAD_DIM), lambda gg, i: (0, gg, 0, 0)),
                pl.BlockSpec((1, 1, nc, HEAD_DIM), lambda gg, i: (1, gg, 0, 0)),
                pl.BlockSpec((nc, V7X_LANES), lambda gg, i: (0, 0)),
            ],
            out_specs=[
                pl.BlockSpec((tq, qw), lambda gg, i: (i, gg)),
                pl.BlockSpec((1, tq, V7X_LANES), lambda gg, i: (gg, i, 0)),
            ],
            out_shape=[
                jax.ShapeDtypeStruct((seg, NSA_Q_W), BF16),
                jax.ShapeDtypeStruct((g, seg, V7X_LANES), F32),
            ],
            compiler_params=_cparams(("parallel", "parallel"), VMEM_MID),
        )(slopes, proj, kvc, kvc, overlap)
        outs.append(o_q)
        sels.append(sel_q)
    return jnp.concatenate(outs, axis=0), jnp.concatenate(sels, axis=1)


def _flash_scratch(n_rows):
    return [pltpu.VMEM((n_rows, V7X_LANES), F32), pltpu.VMEM((n_rows, 2 * HEAD_DIM), F32)]


def _flash_init(m_sc, acc_sc):
    m_sc[...] = jnp.full(m_sc.shape, NEG, F32)
    acc_sc[...] = jnp.zeros(acc_sc.shape, F32)


def _with_ones(v):
    return jnp.concatenate([v, jnp.ones(v.shape, v.dtype)], axis=1)


def _tile_dist(q0, k0, rows, cols):
    return (q0 + lax.broadcasted_iota(jnp.int32, (rows, 1), 0)) - (k0 + lax.broadcasted_iota(jnp.int32, (1, cols), 1))


def _flash_update(rows, s, vx, m_sc, acc_sc):
    reps = s.shape[1] // V7X_LANES
    m_prev = m_sc[rows, :]
    m_new = jnp.maximum(m_prev, jnp.max(s, axis=-1, keepdims=True))
    alpha = jnp.exp2(m_prev - m_new)
    p = jnp.exp2(s - jnp.concatenate([m_new] * reps, axis=1))
    acc_sc[rows, :] = jnp.concatenate([alpha, alpha], axis=1) * acc_sc[rows, :] + _dot(p.astype(BF16), vx)
    m_sc[rows, :] = m_new


def _flash_out(rows, acc_sc):
    return acc_sc[rows, :HEAD_DIM] / acc_sc[rows, HEAD_DIM:]


def _flash_store(o_ref, acc_sc, n_rep, tq):
    for r in range(n_rep):
        o = _flash_out(slice(r * tq, (r + 1) * tq), acc_sc)
        o_ref[:, r * HEAD_DIM:(r + 1) * HEAD_DIM] = o.astype(o_ref.dtype)


MASK_BIG = 2.0 ** 60
SEL_TILE_BLOCKS = 8
SLOPE_LANE0 = SEL_TILE_BLOCKS


def _selattn_kernel(pi_ref, pj_ref, q_ref, k_ref, v_ref, sel_ref, shift_ref, kx_ref, sv_ref, o_ref, m_sc, acc_sc):
    step = pl.program_id(1)
    i = pi_ref[step]
    j = pj_ref[step]
    tq = q_ref.shape[0]
    tk = k_ref.shape[0]

    @pl.when(j == 0)
    def _():
        _flash_init(m_sc, acc_sc)

    def tile(diagonal):
        kk = jnp.concatenate([k_ref[...], kx_ref[0]], axis=1)
        vx = _with_ones(v_ref[...])
        bits = _dot(sel_ref[0].astype(BF16), shift_ref[0])
        lane = lax.broadcasted_iota(jnp.int32, (1, V7X_LANES), 1)
        mask_cols = jnp.where(lane < SEL_TILE_BLOCKS, (bits - 1.0) * MASK_BIG, 0.0)
        if diagonal:
            keep = _tile_dist(0, 0, tq, tk) >= 0
        for r in range(NSA_REP):
            extra = (mask_cols + sv_ref[0, r:r + 1, :].astype(F32)).astype(BF16)
            qq = jnp.concatenate([q_ref[:, r * HEAD_DIM:(r + 1) * HEAD_DIM], extra], axis=1)
            s = _dot_nt(qq, kk)
            if diagonal:
                s = jnp.where(keep, s, -MASK_BIG)
            _flash_update(slice(r * tq, (r + 1) * tq), s, vx, m_sc, acc_sc)

    @pl.when(j < i)
    def _():
        tile(False)

    @pl.when(j == i)
    def _():
        tile(True)
        _flash_store(o_ref, acc_sc, NSA_REP, tq)


def _fold_tables(slopes, n_tiles, tq):
    lane = np.arange(V7X_LANES)
    col = np.arange(tq)[:, None]
    kx = np.zeros((n_tiles, tq, V7X_LANES), np.float32)
    for dtile in range(n_tiles):
        rel = col - dtile * tq
        a = 64 * np.floor_divide(rel, 64)
        kx[dtile, :, :SEL_TILE_BLOCKS] = (col // SEL_BLOCK == lane[None, :SEL_TILE_BLOCKS])
        kx[dtile, :, SLOPE_LANE0:SLOPE_LANE0 + 3] = a
        kx[dtile, :, SLOPE_LANE0 + 3:SLOPE_LANE0 + 6] = rel - a
    s3 = jnp.stack(_split3(slopes[:NSA_HEADS]), axis=1).astype(F32)
    sv = jnp.zeros((NSA_HEADS, V7X_LANES), F32)
    sv = sv.at[:, SLOPE_LANE0:SLOPE_LANE0 + 3].set(s3).at[:, SLOPE_LANE0 + 3:SLOPE_LANE0 + 6].set(s3)
    sv = sv.reshape(NSA_KV_GROUPS, NSA_REP, V7X_LANES)
    sv = jnp.pad(sv, ((0, 0), (0, V7X_SUBLANES - NSA_REP), (0, 0))).astype(BF16)
    return jnp.asarray(kx, BF16), sv


def _sel_attention(proj, sel, slopes):
    t = proj.shape[0]
    g = NSA_KV_GROUPS
    tq = min(512, t)
    assert tq == SEL_TILE_BLOCKS * SEL_BLOCK
    nq = t // tq
    qw = NSA_REP * HEAD_DIM
    pairs = [(i, j) for i in range(nq) for j in range(i + 1)]
    pi = jnp.asarray([p[0] for p in pairs], jnp.int32)
    pj = jnp.asarray([p[1] for p in pairs], jnp.int32)
    lane = np.arange(V7X_LANES)[None, :]
    blk = np.arange(V7X_LANES)[:, None]
    shift = np.stack([(blk == SEL_TILE_BLOCKS * j + lane) & (lane < SEL_TILE_BLOCKS)
                      for j in range(nq)]).astype(np.float32)
    kx, sv = _fold_tables(slopes, nq, tq)
    return pl.pallas_call(
        _selattn_kernel,
        grid_spec=pltpu.PrefetchScalarGridSpec(
            num_scalar_prefetch=2,
            grid=(g, len(pairs)),
            in_specs=[
                pl.BlockSpec((tq, qw), lambda gg, s, pi, pj: (pi[s], gg)),
                pl.BlockSpec((tq, HEAD_DIM), lambda gg, s, pi, pj: (pj[s], KS_BLK + gg)),
                pl.BlockSpec((tq, HEAD_DIM), lambda gg, s, pi, pj: (pj[s], VS_BLK + gg)),
                pl.BlockSpec((1, tq, V7X_LANES), lambda gg, s, pi, pj: (gg, pi[s], 0)),
                pl.BlockSpec((1, V7X_LANES, V7X_LANES), lambda gg, s, pi, pj: (pj[s], 0, 0)),
                pl.BlockSpec((1, tq, V7X_LANES), lambda gg, s, pi, pj: (pi[s] - pj[s], 0, 0)),
                pl.BlockSpec((1, V7X_SUBLANES, V7X_LANES), lambda gg, s, pi, pj: (gg, 0, 0)),
            ],
            out_specs=pl.BlockSpec((tq, qw), lambda gg, s, pi, pj: (pi[s], gg)),
            scratch_shapes=_flash_scratch(NSA_REP * tq),
        ),
        out_shape=jax.ShapeDtypeStruct((t, NSA_Q_W), BF16),
        compiler_params=_cparams(("parallel", "arbitrary"), VMEM_MID),
    )(pi, pj, proj, proj, proj, sel, jnp.asarray(shift, BF16), kx, sv)


def _win_kernel(q_ref, k_ref, v_ref, kx_ref, sv_ref, o_ref, m_sc, acc_sc):
    i = pl.program_id(1)
    j = pl.program_id(2)
    tq = q_ref.shape[0]

    @pl.when(j == 0)
    def _():
        _flash_init(m_sc, acc_sc)

    def tile(current):
        kk = jnp.concatenate([k_ref[...], kx_ref[0]], axis=1)
        vx = _with_ones(v_ref[...])
        rc = _tile_dist(0, 0, tq, tq)
        keep = (rc >= 0) if current else (rc <= WIN - 1 - tq)
        for r in range(NSA_REP):
            extra = jnp.broadcast_to(sv_ref[0, r:r + 1, :], (tq, V7X_LANES))
            qq = jnp.concatenate([q_ref[:, r * HEAD_DIM:(r + 1) * HEAD_DIM], extra], axis=1)
            s = jnp.where(keep, _dot_nt(qq, kk), -MASK_BIG)
            _flash_update(slice(r * tq, (r + 1) * tq), s, vx, m_sc, acc_sc)

    @pl.when((j == 0) & (i > 0))
    def _():
        tile(False)

    @pl.when(j == 1)
    def _():
        tile(True)
        _flash_store(o_ref, acc_sc, NSA_REP, tq)


def _win_attention(proj, slopes):
    t = proj.shape[0]
    tq = min(512, t)
    assert WIN - 1 <= tq
    qw = NSA_REP * HEAD_DIM
    kv_tile = lambda i, j: jnp.maximum(i - 1 + j, 0)
    kx, sv = _fold_tables(slopes, 2, tq)
    return pl.pallas_call(
        _win_kernel,
        grid=(NSA_KV_GROUPS, t // tq, 2),
        in_specs=[
            pl.BlockSpec((tq, qw), lambda g, i, j: (i, g)),
            pl.BlockSpec((tq, HEAD_DIM), lambda g, i, j: (kv_tile(i, j), KW_BLK + g)),
            pl.BlockSpec((tq, HEAD_DIM), lambda g, i, j: (kv_tile(i, j), VW_BLK + g)),
            pl.BlockSpec((1, tq, V7X_LANES), lambda g, i, j: (1 - j, 0, 0)),
            pl.BlockSpec((1, V7X_SUBLANES, V7X_LANES), lambda g, i, j: (g, 0, 0)),
        ],
        out_specs=pl.BlockSpec((tq, qw), lambda g, i, j: (i, g)),
        out_shape=jax.ShapeDtypeStruct((t, NSA_Q_W), BF16),
        scratch_shapes=_flash_scratch(NSA_REP * tq),
        compiler_params=_cparams(("parallel", "parallel", "arbitrary"), VMEM_MID),
    )(proj, proj, proj, kx, sv)


def _dil_tiles(tq):
    return tuple(-(-window // tq) + 1 for window, _ in DIL_CONFIGS)


def _dil_kernel(slopes_ref, *refs):
    n_cfg = len(DIL_CONFIGS)
    q_refs = refs[0:n_cfg]
    k_refs = refs[n_cfg:2 * n_cfg]
    v_refs = refs[2 * n_cfg:3 * n_cfg]
    o_refs = refs[3 * n_cfg:4 * n_cfg]
    m_sc, acc_sc = refs[4 * n_cfg:]
    h = pl.program_id(0)
    i = pl.program_id(1)
    j = pl.program_id(2)
    tq = q_refs[0].shape[0]
    tiles = _dil_tiles(tq)
    n_steps = max(tiles)
    kt = i - (n_steps - 1) + j

    @pl.when(j == 0)
    def _():
        _flash_init(m_sc, acc_sc)

    for c, (window, dil) in enumerate(DIL_CONFIGS):
        @pl.when((j >= n_steps - tiles[c]) & (kt >= 0))
        def _(c=c, window=window, dil=dil):
            dist = _tile_dist(i * tq, kt * tq, tq, tq)
            ok = (dist >= 0) & (dist <= window) & ((dist & (dil - 1)) == 0)
            slope = slopes_ref[NSA_HEADS + c * DIL_HEADS_PER_CFG + h]
            s = _dot_nt(q_refs[c][...], k_refs[c][...]) - slope * dist.astype(F32)
            s = jnp.where(ok, s, NEG)
            _flash_update(slice(c * tq, (c + 1) * tq), s, _with_ones(v_refs[c][...]), m_sc, acc_sc)

    @pl.when(j == n_steps - 1)
    def _():
        rows = [slice(c * tq, (c + 1) * tq) for c in range(n_cfg)]
        lses = [m_sc[rw, :] + jnp.log2(acc_sc[rw, HEAD_DIM:]) for rw in rows]
        mx = functools.reduce(jnp.maximum, lses)
        es = [jnp.exp2(x - mx) for x in lses]
        den = functools.reduce(lambda a, b: a + b, es)
        for c in range(n_cfg):
            o_refs[c][...] = (_flash_out(rows[c], acc_sc) * (es[c] / den)).astype(o_refs[c].dtype)


def _dil_attention(proj, slopes):
    t = proj.shape[0]
    tq = min(512, t)
    n_cfg = len(DIL_CONFIGS)
    for _, dil in DIL_CONFIGS:
        assert dil & (dil - 1) == 0
    tiles = _dil_tiles(tq)
    n_steps = max(tiles)

    def kv_spec(col0, c):
        def idx(h, i, j):
            step = jnp.maximum(j - (n_steps - tiles[c]), 0)
            return (jnp.maximum(i - (tiles[c] - 1) + step, 0), col0 + c * DIL_HEADS_PER_CFG + h)
        return pl.BlockSpec((tq, HEAD_DIM), idx)

    q_specs = [pl.BlockSpec((tq, HEAD_DIM), lambda h, i, j, c=c: (i, QD_BLK + c * DIL_HEADS_PER_CFG + h))
               for c in range(n_cfg)]
    return pl.pallas_call(
        _dil_kernel,
        grid=(DIL_HEADS_PER_CFG, t // tq, n_steps),
        in_specs=[pl.BlockSpec(memory_space=pltpu.SMEM)] + q_specs
        + [kv_spec(KD_BLK, c) for c in range(n_cfg)] + [kv_spec(VD_BLK, c) for c in range(n_cfg)],
        out_specs=[pl.BlockSpec((tq, HEAD_DIM), lambda h, i, j: (i, h)) for _ in range(n_cfg)],
        out_shape=[jax.ShapeDtypeStruct((t, DIL_CFG_W), BF16) for _ in range(n_cfg)],
        scratch_shapes=_flash_scratch(n_cfg * tq),
        compiler_params=_cparams(("parallel", "parallel", "arbitrary"), VMEM_MID),
    )(slopes, *([proj] * (3 * n_cfg)))


def _combine_kernel(gates_ref, ocmp_ref, osel_ref, owin_ref, o_ref):
    sg = 1.0 / (1.0 + jnp.exp(-gates_ref[...]))
    for h in range(NSA_HEADS):
        cs = slice(h * HEAD_DIM, (h + 1) * HEAD_DIM)
        o = (sg[:, 3 * h:3 * h + 1] * ocmp_ref[:, cs].astype(F32)
             + sg[:, 3 * h + 1:3 * h + 2] * osel_ref[:, cs].astype(F32)
             + sg[:, 3 * h + 2:3 * h + 3] * owin_ref[:, cs].astype(F32))
        o_ref[:, cs] = o.astype(o_ref.dtype)


def _combine(gates, o_cmp, o_sel, o_win):
    t = gates.shape[0]
    tr = min(256, t)
    row = lambda w: pl.BlockSpec((tr, w), lambda i: (i, 0))
    return pl.pallas_call(
        _combine_kernel,
        grid=(t // tr,),
        in_specs=[row(V7X_LANES), row(NSA_Q_W), row(NSA_Q_W), row(NSA_Q_W)],
        out_specs=row(NSA_Q_W),
        out_shape=jax.ShapeDtypeStruct((t, NSA_Q_W), BF16),
        compiler_params=_cparams(("parallel",), VMEM_MID),
    )(gates, o_cmp, o_sel, o_win)


def _cross_kernel(q_ref, kv_ref, wo_ref, x_ref, o_ref):
    cw = CROSS_HEADS * HEAD_DIM
    outs = []
    for h in range(CROSS_HEADS):
        q = q_ref[:, h * HEAD_DIM:(h + 1) * HEAD_DIM]
        k = kv_ref[:, h * HEAD_DIM:(h + 1) * HEAD_DIM]
        v = kv_ref[:, cw + h * HEAD_DIM:cw + (h + 1) * HEAD_DIM]
        s = _dot_nt(q, k) * SCALE
        m = jnp.max(s, axis=-1, keepdims=True)
        p = jnp.exp(s - m)
        l = jnp.sum(p, axis=-1, keepdims=True)
        outs.append((_dot(p.astype(BF16), v) / l).astype(BF16))
    o = jnp.concatenate(outs, axis=-1)
    o_ref[...] = x_ref[...] + _dot(o, wo_ref[...])


def _cross_attention(q, kv, wo, x):
    t, d = x.shape
    s_len = kv.shape[0]
    cw = CROSS_HEADS * HEAD_DIM
    tq = min(256, t)
    return pl.pallas_call(
        _cross_kernel,
        grid=(t // tq,),
        in_specs=[
            pl.BlockSpec((tq, cw), lambda i: (i, 0)),
            pl.BlockSpec((s_len, 2 * cw), lambda i: (0, 0)),
            pl.BlockSpec((cw, d), lambda i: (0, 0)),
            pl.BlockSpec((tq, d), lambda i: (i, 0)),
        ],
        out_specs=pl.BlockSpec((tq, d), lambda i: (i, 0)),
        out_shape=jax.ShapeDtypeStruct((t, d), F32),
        compiler_params=_cparams(("parallel",), VMEM_MID),
    )(q, kv, wo, x)


def _router_kernel(x_ref, g_ref, whi_ref, wmid_ref, wlo_ref, b_ref, h_ref, eid_ref, gate_ref):
    x = x_ref[...]
    tr, d = x.shape
    ms = jnp.mean(x * x, axis=-1, keepdims=True)
    h = x * lax.rsqrt(ms + EPS) * g_ref[...]
    h_ref[...] = jnp.zeros(h_ref.shape, h_ref.dtype)
    for s in range(SLAB_ROWS):
        h_ref[pl.ds(s, tr, stride=SLAB_PITCH), :] = h[:, s * V7X_LANES:(s + 1) * V7X_LANES]
    hh, hm, hl = _split3(h)
    whi, wmid, wlo = whi_ref[...], wmid_ref[...], wlo_ref[...]
    logits = (_dot(hh, whi) + (_dot(hh, wmid) + _dot(hm, whi))
              + (_dot(hh, wlo) + _dot(hm, wmid) + _dot(hl, whi))) + b_ref[...]
    lane = lax.broadcasted_iota(jnp.int32, (1, V7X_LANES), 1)
    ninf = -jnp.inf
    is_g = lane < N_GROUPS
    lg = jnp.where(is_g, logits, ninf)
    mg = jnp.max(lg, axis=-1, keepdims=True)
    grp = jnp.min(jnp.where(lg == mg, lane, V7X_LANES), axis=-1, keepdims=True)
    p_grp = 1.0 / jnp.sum(jnp.where(is_g, jnp.exp(lg - mg), 0.0), axis=-1, keepdims=True)
    lo_lane = N_GROUPS + grp * EXPERTS_PER_GROUP
    in_grp = (lane >= lo_lane) & (lane < lo_lane + EXPERTS_PER_GROUP)
    le = jnp.where(in_grp, logits, ninf)
    v1 = jnp.max(le, axis=-1, keepdims=True)
    i1 = jnp.min(jnp.where(le == v1, lane, V7X_LANES), axis=-1, keepdims=True)
    le2 = jnp.where(lane == i1, ninf, le)
    v2 = jnp.max(le2, axis=-1, keepdims=True)
    i2 = jnp.min(jnp.where(le2 == v2, lane, V7X_LANES), axis=-1, keepdims=True)
    e2 = jnp.exp(v2 - v1)
    den = 1.0 + e2
    eid_ref[...] = jnp.where(lane == 0, i1 - N_GROUPS, jnp.where(lane == 1, i2 - N_GROUPS, 0))
    gate_ref[...] = jnp.where(lane == 0, p_grp / den, jnp.where(lane == 1, p_grp * e2 / den, 0.0))


def _router(x, g, w_hi, w_mid, w_lo, b):
    t, d = x.shape
    assert d == SLAB_ROWS * V7X_LANES
    tr = min(256, t)
    full = lambda shp: pl.BlockSpec(shp, lambda i: (0, 0))
    return pl.pallas_call(
        _router_kernel,
        grid=(t // tr,),
        in_specs=[pl.BlockSpec((tr, d), lambda i: (i, 0)), full((1, d)),
                  full((d, V7X_LANES)), full((d, V7X_LANES)), full((d, V7X_LANES)), full((1, V7X_LANES))],
        out_specs=[pl.BlockSpec((tr * SLAB_PITCH, V7X_LANES), lambda i: (i, 0)),
                   pl.BlockSpec((tr, V7X_LANES), lambda i: (i, 0)),
                   pl.BlockSpec((tr, V7X_LANES), lambda i: (i, 0))],
        out_shape=[jax.ShapeDtypeStruct((t * SLAB_PITCH, V7X_LANES), F32),
                   jax.ShapeDtypeStruct((t, V7X_LANES), jnp.int32),
                   jax.ShapeDtypeStruct((t, V7X_LANES), F32)],
        compiler_params=_cparams(("parallel",), VMEM_MID),
    )(x, g.reshape(1, d), w_hi, w_mid, w_lo, b)


def _slab_copy(src_ref, dst_ref, sem, src_slab, dst_slab, src_pitch, dst_pitch, rows):
    src0 = pl.multiple_of(src_slab * src_pitch, V7X_SUBLANES)
    dst0 = pl.multiple_of(dst_slab * dst_pitch, V7X_SUBLANES)
    return pltpu.make_async_copy(src_ref.at[pl.ds(src0, rows)], dst_ref.at[pl.ds(dst0, rows)], sem)


def _unpack_rows(xg, r0, rows):
    parts = [xg[pl.ds(r0 * SLAB_PITCH + s, rows, stride=SLAB_PITCH), :] for s in range(SLAB_ROWS)]
    return jnp.concatenate(parts, axis=1)


def _moe_kernel(ue_ref, un_ref, tab_ref, tabn_ref, hp_ref, wg_ref, wu_ref, wd_ref, rw_ref, yt_ref,
                xg, xb, acc, yst, gsem, ssem, cnt):
    u = pl.program_id(0)
    f = pl.program_id(1)
    n_units = pl.num_programs(0)
    n_f = pl.num_programs(1)
    n = un_ref[u]
    n_next = jnp.where(u + 1 < n_units, un_ref[jnp.minimum(u + 1, n_units - 1)], 0)
    unit_rows = acc.shape[0]

    def x_copy(k, tok):
        return _slab_copy(hp_ref, xg, gsem, tok, k, SLAB_PITCH, SLAB_PITCH, SLAB_ROWS)

    def y_copy(k, slot):
        return _slab_copy(yst, yt_ref, ssem, k, slot, SLAB_PITCH, SLAB_PITCH, SLAB_PITCH)

    def issue_gather(table_ref, count):
        def body(k, c):
            x_copy(k, table_ref[0, 0, k]).start()
            return c
        lax.fori_loop(0, count, body, 0)

    def drain_scatter():
        def body(k, c):
            y_copy(0, 0).wait()
            return c
        lax.fori_loop(0, cnt[0], body, 0)
        cnt[0] = 0

    @pl.when((u == 0) & (f == 0))
    def _():
        xg[...] = jnp.zeros(xg.shape, xg.dtype)
        yst[...] = jnp.zeros(yst.shape, yst.dtype)
        cnt[0] = 0
        issue_gather(tab_ref, n)

    @pl.when((f == 0) & (n > 0))
    def _():
        def drain(k, c):
            x_copy(0, 0).wait()
            return c
        lax.fori_loop(0, n, drain, 0)
        xb[...] = _unpack_rows(xg, 0, unit_rows)
        acc[...] = jnp.zeros(acc.shape, acc.dtype)

    @pl.when((f == 1) & (n_next > 0))
    def _():
        issue_gather(tabn_ref, n_next)

    @pl.when(n > 0)
    def _():
        wg = wg_ref[0].astype(BF16)
        wu = wu_ref[0].astype(BF16)
        wd = wd_ref[0].astype(BF16)
        d = acc.shape[1]
        col_chunk = min(1024, d)

        def sub_block(r0, rows):
            x = xb[r0:r0 + rows, :].astype(BF16)
            gt = _dot(x, wg)
            up = _dot(x, wu)
            a = (gt * (1.0 / (1.0 + jnp.exp(-gt))) * up).astype(BF16)
            for c0 in range(0, d, col_chunk):
                acc[r0:r0 + rows, c0:c0 + col_chunk] += _dot(a, wd[:, c0:c0 + col_chunk])

        sub_block(0, MOE_SUB_ROWS)

        @pl.when(n > MOE_SUB_ROWS)
        def _():
            sub_block(MOE_SUB_ROWS, unit_rows - MOE_SUB_ROWS)

    @pl.when((f == n_f - 1) & (n > 0))
    def _():
        drain_scatter()
        rw = rw_ref[0]
        for s in range(SLAB_ROWS):
            yst[pl.ds(s, unit_rows, stride=SLAB_PITCH), :] = acc[:, s * V7X_LANES:(s + 1) * V7X_LANES] * rw

        def scatter(k, c):
            y_copy(k, tab_ref[0, 0, unit_rows + k]).start()
            return c
        lax.fori_loop(0, n, scatter, 0)
        cnt[0] = n

    @pl.when((u == n_units - 1) & (f == n_f - 1))
    def _():
        drain_scatter()


def _moe_experts(hp, tab, row_w, unit_e, unit_n, w_gate, w_up, w_down, t):
    n_units = unit_e.shape[0]
    r = MOE_UNIT_ROWS
    _, d, d_exp = w_gate.shape
    assert d == SLAB_ROWS * V7X_LANES
    tf = min(256, d_exp)
    nf = d_exp // tf
    assert nf >= 2

    def wchunk(u, f, ue, un):
        return jnp.where(un[u] > 0, f, nf - 1)

    return pl.pallas_call(
        _moe_kernel,
        grid_spec=pltpu.PrefetchScalarGridSpec(
            num_scalar_prefetch=2,
            grid=(n_units, nf),
            in_specs=[
                pl.BlockSpec((1, 1, 2 * r), lambda u, f, ue, un: (u, 0, 0), memory_space=pltpu.SMEM),
                pl.BlockSpec((1, 1, 2 * r), lambda u, f, ue, un: (jnp.minimum(u + 1, n_units - 1), 0, 0),
                             memory_space=pltpu.SMEM),
                pl.BlockSpec(memory_space=pl.ANY),
                pl.BlockSpec((1, d, tf), lambda u, f, ue, un: (ue[u], 0, wchunk(u, f, ue, un))),
                pl.BlockSpec((1, d, tf), lambda u, f, ue, un: (ue[u], 0, wchunk(u, f, ue, un))),
                pl.BlockSpec((1, tf, d), lambda u, f, ue, un: (ue[u], wchunk(u, f, ue, un), 0)),
                pl.BlockSpec((1, r, 1), lambda u, f, ue, un: (u, 0, 0)),
            ],
            out_specs=pl.BlockSpec(memory_space=pl.ANY),
            scratch_shapes=[
                pltpu.VMEM((r * SLAB_PITCH, V7X_LANES), F32),
                pltpu.VMEM((r, d), F32),
                pltpu.VMEM((r, d), F32),
                pltpu.VMEM((r * SLAB_PITCH, V7X_LANES), F32),
                pltpu.SemaphoreType.DMA(()),
                pltpu.SemaphoreType.DMA(()),
                pltpu.SMEM((1,), jnp.int32),
            ],
        ),
        out_shape=jax.ShapeDtypeStruct((EXPERT_TOPK * t * SLAB_PITCH, V7X_LANES), F32),
        compiler_params=_cparams(("arbitrary", "arbitrary"), VMEM_BIG),
    )(unit_e, unit_n, tab, tab, hp, w_gate, w_up, w_down, row_w)


def _final_kernel(x_ref, y0_ref, y1_ref, g_ref, o_ref):
    tq = x_ref.shape[0]
    parts = []
    for s in range(SLAB_ROWS):
        parts.append(y0_ref[pl.ds(s, tq, stride=SLAB_PITCH), :] + y1_ref[pl.ds(s, tq, stride=SLAB_PITCH), :])
    x = x_ref[...] + jnp.concatenate(parts, axis=1)
    ms = jnp.mean(x * x, axis=-1, keepdims=True)
    o_ref[...] = x * lax.rsqrt(ms + EPS) * g_ref[...]


def _final(x2, yt, g_final):
    t, d = x2.shape
    tq = min(128, t)
    nt = t // tq
    return pl.pallas_call(
        _final_kernel,
        grid=(nt,),
        in_specs=[
            pl.BlockSpec((tq, d), lambda i: (i, 0)),
            pl.BlockSpec((tq * SLAB_PITCH, V7X_LANES), lambda i: (i, 0)),
            pl.BlockSpec((tq * SLAB_PITCH, V7X_LANES), lambda i: (nt + i, 0)),
            pl.BlockSpec((1, d), lambda i: (0, 0)),
        ],
        out_specs=pl.BlockSpec((tq, d), lambda i: (i, 0)),
        out_shape=jax.ShapeDtypeStruct((t, d), F32),
        compiler_params=_cparams(("parallel",), VMEM_MID),
    )(x2, yt, yt, g_final.reshape(1, d))


def _alibi_slopes(n):
    return jnp.exp2(-8.0 * jnp.arange(1, n + 1, dtype=F32) / n)


def _hybrid_mixer(x, norm_g, w_in, w_out, pe_k, w1_k, w2_k, pe_v, w1_v, w2_v):
    t, d = x.shape
    assert t % (CMP_STRIDE * 8) == 0 and t // SEL_BLOCK <= V7X_LANES
    h = _rmsnorm(x, norm_g)

    g0 = NSA_Q_W + 6 * NSA_KV_W
    assert g0 == NSA_BLKS * HEAD_DIM
    nsa_scale = np.ones((g0,), np.float32)
    nsa_scale[QN_BLK * HEAD_DIM:KC_BLK * HEAD_DIM] = SCALE * LOG2E
    tn = 512
    proj = _matmul_wcast(h, w_in, jnp.asarray(nsa_scale), 0, g0 // tn, tn, BF16)
    gates = _matmul_wcast(h, w_in, jnp.ones((V7X_LANES,), F32), g0, 1, V7X_LANES, F32)
    dil_w = 3 * DIL_HEADS * HEAD_DIM
    dil_scale = np.ones((dil_w,), np.float32)
    dil_scale[:DIL_HEADS * HEAD_DIM] = SCALE * LOG2E
    proj_dil = _matmul_wcast(h, w_in, jnp.asarray(dil_scale), g0 + NSA_GATE_W, dil_w // tn, tn, BF16)
    slopes = _alibi_slopes(N_MIX_HEADS) * LOG2E

    nc = t // CMP_STRIDE
    kv_raw = proj[:, KC_BLK * HEAD_DIM:KS_BLK * HEAD_DIM]
    a2 = kv_raw.reshape(nc, CMP_STRIDE, 2, NSA_KV_GROUPS, HEAD_DIM).transpose(2, 3, 0, 1, 4)
    a2 = a2.reshape(2, NSA_KV_GROUPS, nc, CMP_STRIDE * HEAD_DIM)
    w1 = jnp.stack([w1_k, w1_v]).astype(BF16)
    w2 = jnp.stack([w2_k, w2_v]).astype(BF16)
    pe = jnp.stack([pe_k, pe_v]).reshape(2, 1, CMP_LEN * HEAD_DIM)
    pe = jnp.broadcast_to(pe, (2, 8, CMP_LEN * HEAD_DIM)).astype(BF16)
    kvc = _compress(a2, w1, pe, w2)

    n_blk = t // SEL_BLOCK
    n_sel = min(SEL_TOPK, n_blk)
    ci = CMP_STRIDE * np.arange(nc)[:, None]
    sj = SEL_BLOCK * np.arange(V7X_LANES)[None, :]
    overlap = jnp.asarray(((ci < sj + SEL_BLOCK) & (ci + CMP_LEN > sj)).astype(np.float32), dtype=BF16)
    o_cmp, sel = _cmp_attention(proj, kvc, slopes, overlap, n_sel)

    o_sel = _sel_attention(proj, sel, slopes)

    o_win = _win_attention(proj, slopes)
    o_dil = _dil_attention(proj_dil, slopes)
    o_nsa = _combine(gates, o_cmp, o_sel, o_win)
    return _matmul([o_nsa] + list(o_dil), w_out.astype(BF16), F32, residual=x)


def _cross_block(x, mem, norm_cross, norm_mem, wq, wkv, wo):
    h = _rmsnorm(x, norm_cross)
    m = _rmsnorm(mem, norm_mem)
    q = _matmul(h, wq.astype(BF16), BF16)
    kv = _matmul(m, wkv.astype(BF16), BF16)
    return _cross_attention(q, kv, wo.astype(BF16), x)


def _moe_tables(eid, gate, t):
    r = MOE_UNIT_ROWS
    n_assign = t * EXPERT_TOPK
    e_flat = eid[:, :EXPERT_TOPK].reshape(n_assign)
    w_flat = gate[:, :EXPERT_TOPK].reshape(n_assign)
    order = jnp.argsort(e_flat).astype(jnp.int32)
    e_s = e_flat[order]
    experts = jnp.arange(N_EXPERTS, dtype=jnp.int32)
    counts = jnp.sum((e_flat[None, :] == experts[:, None]).astype(jnp.int32), axis=1)
    starts = jnp.cumsum(counts) - counts
    pcounts = (counts + r - 1) // r * r
    pends = jnp.cumsum(pcounts)
    pstarts = pends - pcounts
    row = (pstarts[e_s] + (jnp.arange(n_assign, dtype=jnp.int32) - starts[e_s])).astype(jnp.int32)
    n_units = n_assign // r + N_EXPERTS
    n_rows = n_units * r
    tok_s = order // EXPERT_TOPK
    slot_s = (order % EXPERT_TOPK) * t + tok_s
    fields = jnp.zeros((n_rows, 2), jnp.int32).at[row].set(jnp.stack([tok_s, slot_s], axis=1))
    row_tok, row_slot = fields[:, 0], fields[:, 1]
    row_w = jnp.zeros((n_rows,), F32).at[row].set(w_flat[order])
    unit_start = jnp.arange(n_units, dtype=jnp.int32) * r
    active = unit_start < pends[-1]
    e_raw = jnp.minimum(jnp.sum(pends[None, :] <= unit_start[:, None], axis=1), N_EXPERTS - 1).astype(jnp.int32)
    e_last = e_raw[pends[-1] // r - 1]
    unit_e = jnp.where(active, e_raw, e_last)
    unit_n = jnp.where(active, jnp.clip(counts[unit_e] - (unit_start - pstarts[unit_e]), 0, r), 0).astype(jnp.int32)
    tab = jnp.concatenate([row_tok.reshape(n_units, 1, r), row_slot.reshape(n_units, 1, r)], axis=2)
    return tab, row_w.reshape(n_units, r, 1), unit_e, unit_n


def _moe_block(x, norm_ffn, w_rg, b_rg, w_re, b_re, w_gate, w_up, w_down, norm_final):
    t, d = x.shape
    pad = V7X_LANES - N_GROUPS - N_EXPERTS
    w_r = jnp.pad(jnp.concatenate([w_rg, w_re], axis=1), ((0, 0), (0, pad)))
    b_r = jnp.pad(jnp.concatenate([b_rg, b_re]), (0, pad)).reshape(1, V7X_LANES)
    w_hi, w_mid, w_lo = _split3(w_r)
    hp, eid, gate = _router(x, norm_ffn, w_hi, w_mid, w_lo, b_r)
    tab, row_w, unit_e, unit_n = _moe_tables(eid, gate, t)
    yt = _moe_experts(hp, tab, row_w, unit_e, unit_n, w_gate, w_up, w_down, t)
    return _final(x, yt, norm_final)


def kernel(x, mem, norm_mix, w_in, w_out, cmp_pe_k, cmp_w1_k, cmp_w2_k, cmp_pe_v, cmp_w1_v, cmp_w2_v, norm_cross, norm_mem, w_q_cross, w_kv_cross, w_o_cross, norm_ffn, w_router_group, b_router_group, w_router_expert, b_router_expert, w_gate, w_up, w_down, norm_final):
    b, t, d = x.shape
    depth = norm_mix.shape[0]
    assert b == 1 and depth == 1
    xs = x.reshape(t, d)
    l = 0
    xs = _hybrid_mixer(xs, norm_mix[l], w_in[l], w_out[l], cmp_pe_k[l], cmp_w1_k[l], cmp_w2_k[l],
                       cmp_pe_v[l], cmp_w1_v[l], cmp_w2_v[l])
    xs = _cross_block(xs, mem.reshape(mem.shape[1], d), norm_cross[l], norm_mem[l], w_q_cross[l],
                      w_kv_cross[l], w_o_cross[l])
    out = _moe_block(xs, norm_ffn[l], w_router_group[l], b_router_group[l], w_router_expert[l],
                     b_router_expert[l], w_gate[l], w_up[l], w_down[l], norm_final)
    return out.reshape(b, t, d)
```
